```python
import math
import jax, jax.numpy as jnp
from jax import lax
import numpy as np

D_MODEL = 2048
BATCH = 8
SEQ = 2048
DEPTH = 2
DEC_BATCH = 32
DEC_SEQ = 1
PAST_LEN = 8192
PAGE_SIZE = 128

A_WIDTH = D_MODEL // 2
A_HEAD = 64
A_HEADS = A_WIDTH // A_HEAD
DECAY_RANK = 64
AAA_RANK = 64
GATE_RANK = 160
A_COLS = 3 * A_WIDTH + DECAY_RANK + AAA_RANK + GATE_RANK
GN_EPS = 64e-5
B_WIDTH = D_MODEL - A_WIDTH
CONV_W = 31
C_HEADS = 16
C_HEAD_DIM = D_MODEL // C_HEADS
C_KV_HEADS = 4
C_GROUP = C_HEADS // C_KV_HEADS
C_Q_W = C_HEADS * C_HEAD_DIM
C_KV_W = C_KV_HEADS * C_HEAD_DIM
MOBA_BLOCK = 256
MOBA_TOPK = 3
Q_BLOCK = 128
MEM_LEN = 256
X_HEADS = 4
X_HEAD_DIM = 128
X_W = X_HEADS * X_HEAD_DIM
D_FF = -(-8 * D_MODEL // (3 * 256)) * 256
RMS_EPS = 1e-6
LN_EPS = 1e-5
NEG = -1e30
N_A_LAYERS = (DEPTH + 1) // 2
N_C_LAYERS = DEPTH // 2

kernel_name = 'rwkv7_conformer_moba_hybrid_step'


def rms_norm(x, g):
    xf = x.astype(jnp.float32)
    y = xf * lax.rsqrt(jnp.mean(xf * xf, -1, keepdims=True) + RMS_EPS)
    return (y * g.astype(jnp.float32)).astype(x.dtype)


def layer_norm(x, g, b, eps):
    xf = x.astype(jnp.float32)
    xc = xf - jnp.mean(xf, -1, keepdims=True)
    var = jnp.mean(xc * xc, -1, keepdims=True)
    return (xc * lax.rsqrt(var + eps) * g.astype(jnp.float32) + b.astype(jnp.float32)).astype(x.dtype)


def wkv7_scan(s0, r, w, k, v, kk, a):
    def step(s, inp):
        r_t, w_t, k_t, v_t, kk_t, a_t = inp
        sa = jnp.einsum('bhvk,bhk->bhv', s, -kk_t)
        s = s * w_t[:, :, None, :] + sa[..., None] * (kk_t * a_t)[:, :, None, :] + v_t[..., None] * k_t[:, :, None, :]
        return s, jnp.einsum('bhvk,bhk->bhv', s, r_t)
    xs = tuple(jnp.moveaxis(t, 1, 0) for t in (r, w, k, v, kk, a))
    s, y = lax.scan(step, s0, xs)
    return jnp.moveaxis(y, 0, 1), s


def mixer_ab(h, shift_prev, wkv0, conv_buf, w_in, w_out, mu, w0, w2, a0, a2, g2, k_k, k_a, r_k,
             lnx_g, lnx_b, dw_w, dw_b, cn_g, cn_b):
    f32 = jnp.float32
    nb_, T, _ = h.shape
    z = h @ w_in
    za, zb = z[..., :A_COLS], z[..., A_COLS:]
    za_prev = jnp.concatenate([shift_prev[:, None, :].astype(za.dtype), za[:, :-1]], axis=1)
    za_mix = (za + (za_prev - za) * mu).astype(f32)
    i3 = 3 * A_WIDTH
    r, k, v, wl, al, gl = jnp.split(za_mix, [A_WIDTH, 2 * A_WIDTH, i3, i3 + DECAY_RANK, i3 + DECAY_RANK + AAA_RANK], axis=-1)
    w_log = -jax.nn.softplus(-(w0 + jnp.tanh(wl) @ w2)) - 0.5
    decay = jnp.exp(-jnp.exp(w_log))
    a = jax.nn.sigmoid(a0 + al @ a2)
    g = jax.nn.sigmoid(gl) @ g2
    hd = lambda t: t.reshape(nb_, T, A_HEADS, A_HEAD)
    kk = hd(k * k_k)
    kk = kk / jnp.maximum(jnp.linalg.norm(kk, axis=-1, keepdims=True), 1e-12)
    k = k * (1.0 + (a - 1.0) * k_a)
    rh, kh, vh = hd(r), hd(k), hd(v)
    yh, wkv_new = wkv7_scan(wkv0.astype(f32), rh, hd(decay), kh, vh, kk, hd(a))
    yh = layer_norm(yh, lnx_g.reshape(A_HEADS, A_HEAD), lnx_b.reshape(A_HEADS, A_HEAD), GN_EPS)
    yh = yh + jnp.sum(rh * kh * r_k, -1, keepdims=True) * vh
    ya = yh.reshape(nb_, T, A_WIDTH) * g
    glu = zb[..., :B_WIDTH] * jax.nn.sigmoid(zb[..., B_WIDTH:])
    seq = jnp.concatenate([conv_buf.astype(glu.dtype), glu], axis=1)
    conv = lax.conv_general_dilated(seq, dw_w[:, None, :].astype(seq.dtype), (1,), 'VALID',
                                    dimension_numbers=('NWC', 'WIO', 'NWC'),
                                    feature_group_count=B_WIDTH) + dw_b
    yb = jax.nn.silu(layer_norm(conv, cn_g, cn_b, LN_EPS))
    y = jnp.concatenate([ya.astype(h.dtype), yb.astype(h.dtype)], axis=-1) @ w_out
    return y, za[:, -1], wkv_new, seq[:, -(CONV_W - 1):]


def alibi_slopes(n):
    return 2.0 ** (-(8.0 / n) * jnp.arange(1, n + 1, dtype=jnp.float32))


def moba_attention(q, k_all, v_all, q_offset):
    f32 = jnp.float32
    nb_, T = q.shape[0], q.shape[1]
    L = k_all.shape[1]
    nb = -(-L // MOBA_BLOCK)
    pad = nb * MOBA_BLOCK - L
    kb = jnp.pad(k_all, ((0, 0), (0, pad), (0, 0), (0, 0))).reshape(nb_, nb, MOBA_BLOCK, C_KV_HEADS, C_HEAD_DIM)
    vb = jnp.pad(v_all, ((0, 0), (0, pad), (0, 0), (0, 0))).reshape(nb_, nb, MOBA_BLOCK, C_KV_HEADS, C_HEAD_DIM)
    kmean = jnp.mean(kb.astype(f32), axis=2)
    n_sel = min(MOBA_TOPK, nb)
    qb = min(Q_BLOCK, T)
    nq = T // qb
    qg = q.reshape(nb_, nq, qb, C_KV_HEADS, C_GROUP, C_HEAD_DIM)
    slopes = alibi_slopes(C_HEADS).reshape(C_KV_HEADS, C_GROUP)
    h_idx = jnp.arange(C_KV_HEADS)[None, :, None]
    t_idx = jnp.arange(MOBA_BLOCK)
    scale = 1.0 / math.sqrt(C_HEAD_DIM)

    def one(i):
        b = i // nq
        c = i % nq
        qc = qg[b, c]
        kbb, vbb, km = kb[b], vb[b], kmean[b]
        pos = q_offset + c * qb + jnp.arange(qb)
        own = pos // MOBA_BLOCK
        own4 = own[:, None, None, None]
        pos4 = pos[:, None, None, None]
        gate = jnp.einsum('qhgd,nhd->qhgn', qc.astype(f32), km)
        gate = jnp.where(jnp.arange(nb) < own4, gate, NEG)
        _, sel = lax.top_k(gate, n_sel)
        blocks = jnp.concatenate([sel, jnp.broadcast_to(own4, sel.shape[:3] + (1,)).astype(sel.dtype)], -1)
        slot_ok = jnp.concatenate([jnp.arange(n_sel) < own[:, None], jnp.ones((qb, 1), bool)], -1)
        scores = []
        for r in range(n_sel + 1):
            blk = blocks[..., r]
            ksel = kbb[blk, :, h_idx]
            s = jnp.einsum('qhgd,qhgtd->qhgt', qc, ksel, preferred_element_type=f32) * scale
            kpos = blk[..., None] * MOBA_BLOCK + t_idx
            dist = (pos4 - kpos).astype(f32)
            ok = slot_ok[:, r, None, None, None] & (kpos <= pos4)
            scores.append(jnp.where(ok, s - slopes[..., None] * dist, NEG))
        n_slots = n_sel + 1
        p = jax.nn.softmax(jnp.stack(scores, 3).reshape(qb, C_KV_HEADS, C_GROUP, n_slots * MOBA_BLOCK), -1)
        p = p.reshape(qb, C_KV_HEADS, C_GROUP, n_slots, MOBA_BLOCK)
        out = None
        for r in range(n_slots):
            vsel = vbb[blocks[..., r], :, h_idx]
            term = jnp.einsum('qhgt,qhgtd->qhgd', p[:, :, :, r].astype(vsel.dtype), vsel, preferred_element_type=f32)
            out = term if out is None else out + term
        return out

    o = lax.map(one, jnp.arange(nb_ * nq))
    return o.reshape(nb_, T, C_HEADS * C_HEAD_DIM)


def mixer_c(h, past_k, past_v, q_offset, w_in, w_out):
    nb_, T, _ = h.shape
    z = h @ w_in
    q = z[..., :C_Q_W].reshape(nb_, T, C_HEADS, C_HEAD_DIM)
    k = z[..., C_Q_W:C_Q_W + C_KV_W].reshape(nb_, T, C_KV_HEADS, C_HEAD_DIM)
    v = z[..., C_Q_W + C_KV_W:].reshape(nb_, T, C_KV_HEADS, C_HEAD_DIM)
    k_all = jnp.concatenate([past_k.astype(k.dtype), k], axis=1)
    v_all = jnp.concatenate([past_v.astype(v.dtype), v], axis=1)
    o = moba_attention(q, k_all, v_all, q_offset)
    return o.astype(h.dtype) @ w_out, k, v


def mem_kv(mem, g, w_xkv):
    kv = rms_norm(mem, g) @ w_xkv
    nb_, m = mem.shape[0], mem.shape[1]
    return (kv[..., :X_W].reshape(nb_, m, X_HEADS, X_HEAD_DIM),
            kv[..., X_W:].reshape(nb_, m, X_HEADS, X_HEAD_DIM))


def cross_attn(h, mk, mv, w_xq, w_xo):
    nb_, T, _ = h.shape
    q = (h @ w_xq).reshape(nb_, T, X_HEADS, X_HEAD_DIM)
    s = jnp.einsum('bthd,bmhd->bhtm', q, mk, preferred_element_type=jnp.float32) / math.sqrt(X_HEAD_DIM)
    p = jax.nn.softmax(s, -1)
    o = jnp.einsum('bhtm,bmhd->bthd', p.astype(mv.dtype), mv)
    return o.reshape(nb_, T, X_W).astype(h.dtype) @ w_xo


def swiglu(h, w_gu, w_down):
    gu = h @ w_gu
    return (jax.nn.silu(gu[..., :D_FF]) * gu[..., D_FF:]) @ w_down


def setup_inputs(seed: int = 0) -> dict:
    key = jax.random.key(seed)
    ks = iter(jax.random.split(key, 64))
    f32 = jnp.float32

    def nrm(shape, scale):
        return jax.random.normal(next(ks), shape, f32) * scale

    def gain(shape):
        return 1.0 + nrm(shape, 0.05)

    n_pages = PAST_LEN // PAGE_SIZE
    n_used = DEC_BATCH * n_pages
    n_phys = n_used + (n_used + 3) // 4
    page_table = jax.random.permutation(next(ks), n_phys)[:n_used].reshape(DEC_BATCH, n_pages).astype(jnp.int32)
    NA, NC, L = N_A_LAYERS, N_C_LAYERS, DEPTH
    return {
        'x_prompt': nrm((BATCH, SEQ, D_MODEL), 1.0),
        'x_sample': nrm((DEC_BATCH, DEC_SEQ, D_MODEL), 1.0),
        'cache_k': nrm((NC, n_phys, PAGE_SIZE, C_KV_HEADS, C_HEAD_DIM), 1.0),
        'cache_v': nrm((NC, n_phys, PAGE_SIZE, C_KV_HEADS, C_HEAD_DIM), 1.0),
        'state_wkv': nrm((NA, DEC_BATCH, A_HEADS, A_HEAD, A_HEAD), 0.3),
        'state_shift': nrm((NA, DEC_BATCH, A_COLS), 1.0),
        'state_conv': nrm((NA, DEC_BATCH, CONV_W - 1, B_WIDTH), 0.5),
        'cache_mem_k': nrm((L, DEC_BATCH, MEM_LEN, X_HEADS, X_HEAD_DIM), 1.0),
        'cache_mem_v': nrm((L, DEC_BATCH, MEM_LEN, X_HEADS, X_HEAD_DIM), 1.0),
        'page_table': page_table,
        'mem_prompt': nrm((BATCH, MEM_LEN, D_MODEL), 1.0),
        'w_in_ab': nrm((NA, D_MODEL, A_COLS + 2 * B_WIDTH), D_MODEL ** -0.5),
        'w_out_ab': nrm((NA, A_WIDTH + B_WIDTH, D_MODEL), (A_WIDTH + B_WIDTH) ** -0.5),
        'mu_ab': jax.random.uniform(next(ks), (NA, A_COLS), f32),
        'w0': nrm((NA, A_WIDTH), 0.5) - 0.5,
        'w2': nrm((NA, DECAY_RANK, A_WIDTH), 0.5 * DECAY_RANK ** -0.5),
        'a0': nrm((NA, A_WIDTH), 0.1),
        'a2': nrm((NA, AAA_RANK, A_WIDTH), 0.5 * AAA_RANK ** -0.5),
        'g2': nrm((NA, GATE_RANK, A_WIDTH), GATE_RANK ** -0.5),
        'k_k': 0.85 + nrm((NA, A_WIDTH), 0.05),
        'k_a': 1.0 + nrm((NA, A_WIDTH), 0.05),
        'r_k': nrm((NA, A_HEADS, A_HEAD), 0.1),
        'lnx_g': gain((NA, A_WIDTH)),
        'lnx_b': nrm((NA, A_WIDTH), 0.01),
        'dw_w': nrm((NA, CONV_W, B_WIDTH), CONV_W ** -0.5),
        'dw_b': nrm((NA, B_WIDTH), 0.01),
        'cn_g': gain((NA, B_WIDTH)),
        'cn_b': nrm((NA, B_WIDTH), 0.01),
        'w_in_c': nrm((NC, D_MODEL, C_Q_W + 2 * C_KV_W), D_MODEL ** -0.5),
        'w_out_c': nrm((NC, C_Q_W, D_MODEL), C_Q_W ** -0.5),
        'g_mix_pre': gain((L, D_MODEL)),
        'g_mix_post': gain((L, D_MODEL)),
        'g_x_pre': gain((L, D_MODEL)),
        'g_x_post': gain((L, D_MODEL)),
        'g_mem': gain((L, D_MODEL)),
        'w_xq': nrm((L, D_MODEL, X_W), D_MODEL ** -0.5),
        'w_xkv': nrm((L, D_MODEL, 2 * X_W), D_MODEL ** -0.5),
        'w_xo': nrm((L, X_W, D_MODEL), X_W ** -0.5),
        'g_ffn_pre': gain((L, D_MODEL)),
        'g_ffn_post': gain((L, D_MODEL)),
        'w_ffn_gu': nrm((L, D_MODEL, 2 * D_FF), D_MODEL ** -0.5),
        'w_ffn_down': nrm((L, D_FF, D_MODEL), D_FF ** -0.5),
    }


def reference(x_prompt, x_sample, cache_k, cache_v, state_wkv, state_shift, state_conv, cache_mem_k, cache_mem_v,
              page_table, mem_prompt, w_in_ab, w_out_ab, mu_ab, w0, w2, a0, a2, g2, k_k, k_a, r_k, lnx_g, lnx_b,
              dw_w, dw_b, cn_g, cn_b, w_in_c, w_out_c, g_mix_pre, g_mix_post, g_x_pre, g_x_post, g_mem,
              w_xq, w_xkv, w_xo, g_ffn_pre, g_ffn_post, w_ffn_gu, w_ffn_down):
    bp, bs = x_prompt.shape[0], x_sample.shape[0]
    past_len = page_table.shape[1] * PAGE_SIZE
    xp, xs = x_prompt, x_sample
    kp_l, vp_l, ks_l, vs_l = [], [], [], []
    wkvp_l, shp_l, cvp_l, wkvs_l, shs_l, cvs_l = [], [], [], [], [], []
    mkp_l, mvp_l = [], []
    for l in range(DEPTH):
        i = l // 2
        hp = rms_norm(xp, g_mix_pre[l])
        hs = rms_norm(xs, g_mix_pre[l])
        if l % 2 == 0:
            pa = (w_in_ab[i], w_out_ab[i], mu_ab[i], w0[i], w2[i], a0[i], a2[i], g2[i], k_k[i], k_a[i], r_k[i],
                  lnx_g[i], lnx_b[i], dw_w[i], dw_b[i], cn_g[i], cn_b[i])
            mp, shp, wkvp, cvp = mixer_ab(hp, jnp.zeros((bp, A_COLS), hp.dtype),
                                          jnp.zeros((bp, A_HEADS, A_HEAD, A_HEAD), jnp.float32),
                                          jnp.zeros((bp, CONV_W - 1, B_WIDTH), hp.dtype), *pa)
            ms, shs, wkvs, cvs = mixer_ab(hs, state_shift[i], state_wkv[i], state_conv[i], *pa)
            shp_l.append(shp); wkvp_l.append(wkvp); cvp_l.append(cvp)
            shs_l.append(shs); wkvs_l.append(wkvs); cvs_l.append(cvs)
        else:
            empty = jnp.zeros((bp, 0, C_KV_HEADS, C_HEAD_DIM), hp.dtype)
            mp, kp, vp = mixer_c(hp, empty, empty, 0, w_in_c[i], w_out_c[i])
            pk = cache_k[i][page_table].reshape(bs, past_len, C_KV_HEADS, C_HEAD_DIM)
            pv = cache_v[i][page_table].reshape(bs, past_len, C_KV_HEADS, C_HEAD_DIM)
            ms, ks, vs = mixer_c(hs, pk, pv, past_len, w_in_c[i], w_out_c[i])
            kp_l.append(kp); vp_l.append(vp); ks_l.append(ks); vs_l.append(vs)
        xp = xp + rms_norm(mp, g_mix_post[l])
        xs = xs + rms_norm(ms, g_mix_post[l])
        mk, mv = mem_kv(mem_prompt, g_mem[l], w_xkv[l])
        mkp_l.append(mk); mvp_l.append(mv)
        xp = xp + rms_norm(cross_attn(rms_norm(xp, g_x_pre[l]), mk, mv, w_xq[l], w_xo[l]), g_x_post[l])
        xs = xs + rms_norm(cross_attn(rms_norm(xs, g_x_pre[l]), cache_mem_k[l], cache_mem_v[l], w_xq[l], w_xo[l]), g_x_post[l])
        xp = xp + rms_norm(swiglu(rms_norm(xp, g_ffn_pre[l]), w_ffn_gu[l], w_ffn_down[l]), g_ffn_post[l])
        xs = xs + rms_norm(swiglu(rms_norm(xs, g_ffn_pre[l]), w_ffn_gu[l], w_ffn_down[l]), g_ffn_post[l])
    return (xp, xs, jnp.stack(kp_l), jnp.stack(vp_l), jnp.stack(wkvp_l), jnp.stack(shp_l), jnp.stack(cvp_l),
            jnp.stack(mkp_l), jnp.stack(mvp_l), jnp.stack(ks_l), jnp.stack(vs_l), jnp.stack(wkvs_l),
            jnp.stack(shs_l), jnp.stack(cvs_l))
```

```python
import functools
import math

import jax
import jax.numpy as jnp
from jax import lax
from jax.experimental import pallas as pl
from jax.experimental.pallas import tpu as pltpu

F32 = jnp.float32
BF16 = jnp.bfloat16

D_MODEL = 2048
PAGE_SIZE = 128
A_WIDTH = 1024
A_HEAD = 64
A_HEADS = 16
DECAY_RANK = 64
AAA_RANK = 64
GATE_RANK = 160
A_COLS = 3 * A_WIDTH + DECAY_RANK + AAA_RANK + GATE_RANK
GN_EPS = 64e-5
B_WIDTH = 1024
CONV_W = 31
C_HEADS = 16
C_HEAD_DIM = 128
C_KV_HEADS = 4
C_GROUP = 4
C_Q_W = 2048
C_KV_W = 512
MOBA_BLOCK = 256
MOBA_TOPK = 3
Q_BLOCK = 128
MEM_LEN = 256
X_HEADS = 4
X_HEAD_DIM = 128
X_W = 512
D_FF = 5632
RMS_EPS = 1e-6
LN_EPS = 1e-5
NEG = -1e30

LANES = 128
A_OFF_W = 3 * A_WIDTH
A_OFF_A = A_OFF_W + LANES
A_OFF_G = A_OFF_A + LANES
A_PAD = A_OFF_G + 2 * LANES
WKV_CHUNK = 64
VMEM_LIMIT = 56 * 1024 * 1024


def _cparams(sem):
    return pltpu.CompilerParams(dimension_semantics=sem, vmem_limit_bytes=VMEM_LIMIT)


def _dot(a, b):
    return jnp.dot(a, b, preferred_element_type=F32)


def _dot_nt(a, b):
    return lax.dot_general(a, b, (((1,), (1,)), ((), ())), preferred_element_type=F32)


def _dot_tn(a, b):
    return lax.dot_general(a, b, (((0,), (0,)), ((), ())), preferred_element_type=F32)


def _split2(x):
    hi = x.astype(BF16)
    lo = (x - hi.astype(F32)).astype(BF16)
    return hi, lo


def _split3(x):
    hi = x.astype(BF16)
    r = x - hi.astype(F32)
    mid = r.astype(BF16)
    lo = (r - mid.astype(F32)).astype(BF16)
    return hi, mid, lo


def _dot_x3(dot, a, b):
    ah, al = _split2(a)
    bh, bl = _split2(b)
    return dot(ah, bh) + (dot(ah, bl) + dot(al, bh))


def _dot_exact_rhs(a, b_bf16):
    hi, mid, lo = _split3(a)
    return _dot(hi, b_bf16) + (_dot(mid, b_bf16) + _dot(lo, b_bf16))


def _dot_exact_lhs(a_bf16, b):
    hi, mid, lo = _split3(b)
    return _dot(a_bf16, hi) + (_dot(a_bf16, mid) + _dot(a_bf16, lo))


def _sigmoid(x):
    return 1.0 / (1.0 + jnp.exp(-x))


def _norm_mm_kernel(x_ref, g_ref, *refs, n_w):
    w_refs, o_ref, xn_ref = refs[:n_w], refs[n_w], refs[n_w + 1]

    @pl.when(pl.program_id(1) == 0)
    def _():
        x = x_ref[...]
        ms = jnp.mean(x * x, axis=-1, keepdims=True)
        xn_ref[...] = (x * lax.rsqrt(ms + RMS_EPS) * g_ref[...]).astype(BF16)

    xn = xn_ref[...]
    if n_w == 1:
        y = _dot(xn, w_refs[0][...])
    else:
        gate = _dot(xn, w_refs[0][...])
        y = gate * _sigmoid(gate) * _dot(xn, w_refs[1][...])
    o_ref[...] = y.astype(o_ref.dtype)


def _norm_matmul(x, g, w, out_dtype, *, tn, swiglu_half=None):
    m, d = x.shape
    n = swiglu_half if swiglu_half else w.shape[1]
    tm = min(m, 1024)
    assert m % tm == 0 and n % tn == 0
    w_specs = [pl.BlockSpec((d, tn), lambda i, j: (0, j))]
    operands = [x, g.reshape(1, d), w]
    if swiglu_half:
        off = swiglu_half // tn
        w_specs.append(pl.BlockSpec((d, tn), lambda i, j: (0, j + off)))
        operands.append(w)
    return pl.pallas_call(
        functools.partial(_norm_mm_kernel, n_w=len(w_specs)),
        out_shape=jax.ShapeDtypeStruct((m, n), out_dtype),
        grid=(m // tm, n // tn),
        in_specs=[pl.BlockSpec((tm, d), lambda i, j: (i, 0)),
                  pl.BlockSpec((1, d), lambda i, j: (0, 0))] + w_specs,
        out_specs=pl.BlockSpec((tm, tn), lambda i, j: (i, j)),
        scratch_shapes=[pltpu.VMEM((tm, d), BF16)],
        compiler_params=_cparams(("parallel", "arbitrary")),
        name="norm_matmul",
    )(*operands)


def _mm_norm_res_kernel(x_ref, w_ref, g_ref, r_ref, o_ref, acc_ref):
    k = pl.program_id(1)

    @pl.when(k == 0)
    def _():
        acc_ref[...] = jnp.zeros_like(acc_ref)

    acc_ref[...] += _dot(x_ref[...], w_ref[...])

    @pl.when(k == pl.num_programs(1) - 1)
    def _():
        y = acc_ref[...]
        ms = jnp.mean(y * y, axis=-1, keepdims=True)
        o_ref[...] = r_ref[...] + y * lax.rsqrt(ms + RMS_EPS) * g_ref[...]


def _matmul_norm_res(x, w, g, res, *, tk):
    m, kd = x.shape
    n = w.shape[1]
    tm = min(m, 512)
    assert m % tm == 0 and kd % tk == 0
    return pl.pallas_call(
        _mm_norm_res_kernel,
        out_shape=jax.ShapeDtypeStruct((m, n), F32),
        grid=(m // tm, kd // tk),
        in_specs=[pl.BlockSpec((tm, tk), lambda i, k: (i, k)),
                  pl.BlockSpec((tk, n), lambda i, k: (k, 0)),
                  pl.BlockSpec((1, n), lambda i, k: (0, 0)),
                  pl.BlockSpec((tm, n), lambda i, k: (i, 0))],
        out_specs=pl.BlockSpec((tm, n), lambda i, k: (i, 0)),
        scratch_shapes=[pltpu.VMEM((tm, n), F32)],
        compiler_params=_cparams(("parallel", "arbitrary")),
        name="matmul_norm_res",
    )(x, w, g.reshape(1, n), res)


def _xattn_kernel(q_ref, k_ref, v_ref, o_ref):
    scale = 1.0 / math.sqrt(X_HEAD_DIM)
    for h in range(X_HEADS):
        sl = slice(h * X_HEAD_DIM, (h + 1) * X_HEAD_DIM)
        s = _dot_nt(q_ref[:, sl], k_ref[:, sl].astype(BF16)) * scale
        p = jnp.exp(s - jnp.max(s, axis=-1, keepdims=True))
        l = jnp.sum(p, axis=-1, keepdims=True)
        o = _dot(p.astype(BF16), v_ref[:, sl].astype(BF16))
        o_ref[:, sl] = (o / l).astype(o_ref.dtype)


def _cross_attention(q, k_arr, v_arr, k_col, v_col, *, tq):
    b, t, _ = q.shape
    assert t % tq == 0
    return pl.pallas_call(
        _xattn_kernel,
        out_shape=jax.ShapeDtypeStruct((b, t, X_W), BF16),
        grid=(b, t // tq),
        in_specs=[pl.BlockSpec((None, tq, X_W), lambda i, j: (i, j, 0)),
                  pl.BlockSpec((MEM_LEN, X_W), lambda i, j: (i, k_col)),
                  pl.BlockSpec((MEM_LEN, X_W), lambda i, j: (i, v_col))],
        out_specs=pl.BlockSpec((None, tq, X_W), lambda i, j: (i, j, 0)),
        compiler_params=_cparams(("parallel", "arbitrary")),
        name="cross_attention",
    )(q, k_arr, v_arr)


def _wkv_tile_kernel(za_ref, sp_ref, s0_ref, mu_ref, w0_ref, a0_ref, kk_ref, ka_ref, rk_ref,
                     lng_ref, lnb_ref, w2_ref, a2_ref, g2_ref, seg_ref, segt_ref,
                     ya_ref, shift_ref, s_ref,
                     rh_ref, kap_ref, kh_ref, bh_ref, v_ref, khp_ref, bhp_ref, pt_ref, y_ref,
                     *, tt, t_valid):
    c = WKV_CHUNK
    ti = pl.program_id(1)

    @pl.when(ti == 0)
    def _():
        s_ref[...] = s0_ref[...]
        shift_ref[...] = sp_ref[...]

    za = za_ref[...]
    row = lax.broadcasted_iota(jnp.int32, (tt, 1), 0)
    prev = jnp.where(row == 0, shift_ref[...], pltpu.roll(za, 1, axis=0))
    last = (t_valid - 1) % tt
    shift_ref[...] = za[last:last + 1, :]
    mix = za + (prev - za) * mu_ref[...]
    valid = (ti * tt + row) < t_valid

    r = mix[:, 0:A_WIDTH]
    k = mix[:, A_WIDTH:2 * A_WIDTH]
    v = mix[:, 2 * A_WIDTH:3 * A_WIDTH]
    wl = mix[:, A_OFF_W:A_OFF_A]
    al = mix[:, A_OFF_A:A_OFF_G]
    gl = mix[:, A_OFF_G:A_PAD]
    xw = w0_ref[...] + _dot(jnp.tanh(wl).astype(BF16), w2_ref[...])
    lw = jnp.where(valid, -_sigmoid(xw) * math.exp(-0.5), 0.0)
    a = _sigmoid(a0_ref[...] + _dot(al.astype(BF16), a2_ref[...]))
    g = _dot(_sigmoid(gl).astype(BF16), g2_ref[...])

    seg = seg_ref[...]
    segt = segt_ref[...]

    def head_sum(x):
        return _dot_exact_rhs(_dot_exact_rhs(x, seg), segt)

    kk = k * kk_ref[...]
    kk = kk / jnp.maximum(jnp.sqrt(head_sum(kk * kk)), 1e-12)
    kk = jnp.where(valid, kk, 0.0)
    kt = jnp.where(valid, k * (1.0 + (a - 1.0) * ka_ref[...]), 0.0)
    b = kk * a
    v = jnp.where(valid, v, 0.0)

    ri = lax.broadcasted_iota(jnp.int32, (tt, tt), 0)
    ci = lax.broadcasted_iota(jnp.int32, (tt, tt), 1)
    same = (ri // c) == (ci // c)
    cum = _dot_exact_lhs(jnp.where(same & (ci <= ri), 1.0, 0.0).astype(BF16), lw)
    tot = _dot_exact_lhs(jnp.where(same, 1.0, 0.0).astype(BF16), lw)
    p_inv = jnp.exp(-cum)
    p_end = jnp.exp(tot - cum)
    rh_ref[...] = r * jnp.exp(cum)
    kap_ref[...] = kk * jnp.exp(cum - lw)
    kh_ref[...] = kt * p_inv
    bh_ref[...] = b * p_inv
    v_ref[...] = v
    khp_ref[...] = kt * p_end
    bhp_ref[...] = b * p_end
    pt_ref[...] = jnp.exp(tot)

    lane = lax.broadcasted_iota(jnp.int32, (1, LANES), 1)
    m0 = (lane < A_HEAD).astype(F32)
    m1 = 1.0 - m0
    r2 = lax.broadcasted_iota(jnp.int32, (2 * c, 2 * c), 0)
    c2 = lax.broadcasted_iota(jnp.int32, (2 * c, 2 * c), 1)
    strict = (r2 % c) > (c2 % c)
    incl = (r2 % c) >= (c2 % c)
    blockdiag = (r2 // A_HEAD) == (c2 // A_HEAD)
    mm = functools.partial(_dot_x3, _dot)
    mm_nt = functools.partial(_dot_x3, _dot_nt)
    mm_tn = functools.partial(_dot_x3, _dot_tn)

    def stack(x):
        return jnp.concatenate([x * m0, x * m1], axis=0)

    def chunk(ci_, carry):
        rows = pl.ds(pl.multiple_of(ci_ * c, c), c)
        for p in range(A_HEADS // 2):
            cols = slice(p * LANES, (p + 1) * LANES)
            rst = stack(rh_ref[rows, cols])
            kst = stack(kap_ref[rows, cols])
            khst = stack(kh_ref[rows, cols])
            bst = stack(bh_ref[rows, cols])
            vst = stack(v_ref[rows, cols])
            s0 = s_ref[p]
            g_kb = mm_nt(kst, bst)
            g_kk = mm_nt(kst, khst)
            g_rk = mm_nt(rst, khst)
            g_rb = mm_nt(rst, bst)
            x = mm_nt(kst, s0) + mm(jnp.where(strict, g_kk, 0.0), vst)
            lp = jnp.where(strict, g_kb, 0.0)
            x = x - mm(lp, x)
            n = 2
            while n < c:
                lp = mm(lp, lp)
                x = x + mm(lp, x)
                n *= 2
            y = (mm_nt(rst, s0) + mm(jnp.where(incl, g_rk, 0.0), vst)
                 - mm(jnp.where(incl, g_rb, 0.0), x))
            y_ref[rows, cols] = y[:c] + y[c:]
            s_new = (s0 * pt_ref[pl.ds(pl.multiple_of(ci_ * c, c), 1), cols]
                     + mm_tn(vst, stack(khp_ref[rows, cols]))
                     - mm_tn(x, stack(bhp_ref[rows, cols])))
            s_ref[p] = jnp.where(blockdiag, s_new, 0.0)
        return carry

    lax.fori_loop(0, tt // c, chunk, 0)

    y = y_ref[...]
    mean = head_sum(y) * (1.0 / A_HEAD)
    yc = y - mean
    var = head_sum(yc * yc) * (1.0 / A_HEAD)
    yn = yc * lax.rsqrt(var + GN_EPS) * lng_ref[...] + lnb_ref[...]
    yn = yn + head_sum(r * kt * rk_ref[...]) * v
    ya_ref[...] = (yn * g).astype(ya_ref.dtype)


def _wkv_mix(za, shift_prev, s0, prm, *, nb, t_pad, t_valid, tt):
    nt = t_pad // tt
    vec = lambda n: pl.BlockSpec((1, n), lambda i, j: (0, 0))
    full = lambda a: pl.BlockSpec(a.shape, lambda i, j: (0,) * a.ndim)
    state_spec = pl.BlockSpec((None, A_HEADS // 2, LANES, LANES), lambda i, j: (i, 0, 0, 0))
    shift_spec = pl.BlockSpec((None, 1, A_PAD), lambda i, j: (i, 0, 0))
    vecs = [prm[n] for n in ("mu", "w0", "a0", "k_k", "k_a", "r_k", "lnx_g", "lnx_b")]
    mats = [prm[n] for n in ("w2", "a2", "g2", "seg", "segt")]
    return pl.pallas_call(
        functools.partial(_wkv_tile_kernel, tt=tt, t_valid=t_valid),
        out_shape=(jax.ShapeDtypeStruct((nb * t_pad, A_WIDTH), BF16),
                   jax.ShapeDtypeStruct((nb, 1, A_PAD), F32),
                   jax.ShapeDtypeStruct((nb, A_HEADS // 2, LANES, LANES), F32)),
        grid=(nb, nt),
        in_specs=[pl.BlockSpec((tt, A_PAD), lambda i, j: (i * nt + j, 0)), shift_spec, state_spec]
                 + [vec(x.shape[1]) for x in vecs] + [full(x) for x in mats],
        out_specs=(pl.BlockSpec((tt, A_WIDTH), lambda i, j: (i * nt + j, 0)), shift_spec, state_spec),
        scratch_shapes=[pltpu.VMEM((tt, A_WIDTH), F32)] * 9,
        compiler_params=_cparams(("parallel", "arbitrary")),
        name="wkv_mix",
    )(za, shift_prev, s0, *vecs, *mats)


CONV_HALO = 32


def _conv_kernel(zb_ref, cb_ref, w_ref, b_ref, g_ref, be_ref, o_ref, st_ref, ext_ref, conv_ref,
                 *, tt, t_last):
    ti = pl.program_id(1)
    pad = CONV_HALO - (CONV_W - 1)

    @pl.when(ti == 0)
    def _():
        ext_ref[0:CONV_HALO, :] = cb_ref[...]

    zb = zb_ref[...]
    ext_ref[CONV_HALO:CONV_HALO + tt, :] = zb[:, :B_WIDTH] * _sigmoid(zb[:, B_WIDTH:])
    rb = min(tt, 64)
    for r0 in range(0, tt, rb):
        for c0 in range(0, B_WIDTH, LANES):
            cols = slice(c0, c0 + LANES)
            acc = jnp.broadcast_to(b_ref[:, cols], (rb, LANES))
            for j in range(CONV_W):
                acc = acc + ext_ref[r0 + pad + j:r0 + pad + j + rb, cols] * w_ref[j:j + 1, cols]
            conv_ref[r0:r0 + rb, cols] = acc
    conv = conv_ref[...]
    mean = jnp.mean(conv, axis=-1, keepdims=True)
    xc = conv - mean
    var = jnp.mean(xc * xc, axis=-1, keepdims=True)
    y = xc * lax.rsqrt(var + LN_EPS) * g_ref[...] + be_ref[...]
    o_ref[...] = (y * _sigmoid(y)).astype(o_ref.dtype)
    st_ref[...] = ext_ref[t_last + pad:t_last + pad + CONV_W - 1, :]
    if tt >= CONV_HALO:
        ext_ref[0:CONV_HALO, :] = ext_ref[tt:tt + CONV_HALO, :]


def _conv_module(zb, conv_buf, prm, *, nb, t_pad, t_valid, tt):
    nt = t_pad // tt
    assert nt == 1 or tt >= CONV_HALO
    vec = pl.BlockSpec((1, B_WIDTH), lambda i, j: (0, 0))
    return pl.pallas_call(
        functools.partial(_conv_kernel, tt=tt, t_last=t_valid - (nt - 1) * tt),
        out_shape=(jax.ShapeDtypeStruct((nb * t_pad, B_WIDTH), BF16),
                   jax.ShapeDtypeStruct((nb, CONV_W - 1, B_WIDTH), F32)),
        grid=(nb, nt),
        in_specs=[pl.BlockSpec((tt, 2 * B_WIDTH), lambda i, j: (i * nt + j, 0)),
                  pl.BlockSpec((None, CONV_HALO, B_WIDTH), lambda i, j: (i, 0, 0)),
                  pl.BlockSpec((CONV_W, B_WIDTH), lambda i, j: (0, 0)), vec, vec, vec],
        out_specs=(pl.BlockSpec((tt, B_WIDTH), lambda i, j: (i * nt + j, 0)),
                   pl.BlockSpec((None, CONV_W - 1, B_WIDTH), lambda i, j: (i, 0, 0))),
        scratch_shapes=[pltpu.VMEM((CONV_HALO + tt, B_WIDTH), F32), pltpu.VMEM((tt, B_WIDTH), F32)],
        compiler_params=_cparams(("parallel", "arbitrary")),
        name="conv_module",
    )(zb, conv_buf, prm["dw_w"], prm["dw_b"], prm["cn_g"], prm["cn_b"])


def _alibi_slopes(kvh, rows):
    g = lax.broadcasted_iota(jnp.int32, (rows, 1), 0) // (rows // C_GROUP)
    h = (kvh * C_GROUP + g + 1).astype(F32)
    return jnp.exp(h * (-(8.0 / C_HEADS) * math.log(2.0)))


def _top_blocks(gate, lane, n_sel):
    sel = []
    for _ in range(n_sel):
        m = jnp.max(gate, axis=-1, keepdims=True)
        idx = jnp.min(jnp.where(gate == m, lane, LANES), axis=-1, keepdims=True)
        sel.append(idx)
        gate = jnp.where(lane == idx, -jnp.inf, gate)
    return sel


def _moba_prompt_kernel(q_ref, k_ref, v_ref, o_ref, kb_ref, vb_ref, km_ref, *, t):
    kvh = pl.program_id(1)
    qi = pl.program_id(2)
    nb = t // MOBA_BLOCK
    rows = C_GROUP * Q_BLOCK

    @pl.when(qi == 0)
    def _():
        k = k_ref[...]
        kb_ref[...] = k.astype(BF16)
        vb_ref[...] = v_ref[...].astype(BF16)
        km_ref[...] = jnp.zeros_like(km_ref)
        for n in range(nb):
            km_ref[n:n + 1, :] = jnp.mean(k[n * MOBA_BLOCK:(n + 1) * MOBA_BLOCK], axis=0, keepdims=True)

    q = q_ref[...]
    qs = jnp.concatenate([q[:, g * C_HEAD_DIM:(g + 1) * C_HEAD_DIM] for g in range(C_GROUP)], axis=0)
    own = (qi * Q_BLOCK) // MOBA_BLOCK
    lane = lax.broadcasted_iota(jnp.int32, (1, LANES), 1)
    gate = _dot_x3(_dot_nt, qs, km_ref[...])
    gate = jnp.where(lane < own, gate, jnp.where(lane < nb, NEG, -jnp.inf))
    sel = _top_blocks(gate, lane, MOBA_TOPK)
    sel = [jnp.where(own > r, sel[r], -1) for r in range(MOBA_TOPK)]
    own_col = jnp.zeros((rows, 1), jnp.int32) + own

    qb = qs.astype(BF16)
    slope = _alibi_slopes(kvh, rows)
    pos = qi * Q_BLOCK + lax.broadcasted_iota(jnp.int32, (rows, 1), 0) % Q_BLOCK
    tcol = lax.broadcasted_iota(jnp.int32, (1, MOBA_BLOCK), 1)
    scale = 1.0 / math.sqrt(C_HEAD_DIM)

    def block(n, carry):
        m, l, acc = carry
        start = pl.multiple_of(n * MOBA_BLOCK, MOBA_BLOCK)
        kpos = n * MOBA_BLOCK + tcol
        s = _dot_nt(qb, kb_ref[pl.ds(start, MOBA_BLOCK), :]) * scale
        s = s - slope * (pos - kpos).astype(F32)
        picked = own_col == n
        for r in range(MOBA_TOPK):
            picked = picked | (sel[r] == n)
        ok = picked & (kpos <= jnp.where(n == own, pos, t))
        s = jnp.where(ok, s, NEG)
        m_new = jnp.maximum(m, jnp.max(s, axis=-1, keepdims=True))
        alpha = jnp.exp(m - m_new)
        p = jnp.where(ok, jnp.exp(s - m_new), 0.0)
        l = alpha * l + jnp.sum(p, axis=-1, keepdims=True)
        acc = alpha * acc + _dot(p.astype(BF16), vb_ref[pl.ds(start, MOBA_BLOCK), :])
        return m_new, l, acc

    m0 = jnp.full((rows, 1), NEG, F32)
    l0 = jnp.zeros((rows, 1), F32)
    acc0 = jnp.zeros((rows, C_HEAD_DIM), F32)
    _, l, acc = lax.fori_loop(0, own + 1, block, (m0, l0, acc0))
    o = acc / l
    for g in range(C_GROUP):
        o_ref[:, g * C_HEAD_DIM:(g + 1) * C_HEAD_DIM] = o[g * Q_BLOCK:(g + 1) * Q_BLOCK].astype(o_ref.dtype)


def _moba_prompt(z, *, nb, t):
    nq = t // Q_BLOCK
    kcol = C_Q_W // C_HEAD_DIM
    vcol = (C_Q_W + C_KV_W) // C_HEAD_DIM
    return pl.pallas_call(
        functools.partial(_moba_prompt_kernel, t=t),
        out_shape=jax.ShapeDtypeStruct((nb * t, C_Q_W), BF16),
        grid=(nb, C_KV_HEADS, nq),
        in_specs=[pl.BlockSpec((Q_BLOCK, C_GROUP * C_HEAD_DIM), lambda b, h, i: (b * nq + i, h)),
                  pl.BlockSpec((t, C_HEAD_DIM), lambda b, h, i: (b, kcol + h)),
                  pl.BlockSpec((t, C_HEAD_DIM), lambda b, h, i: (b, vcol + h))],
        out_specs=pl.BlockSpec((Q_BLOCK, C_GROUP * C_HEAD_DIM), lambda b, h, i: (b * nq + i, h)),
        scratch_shapes=[pltpu.VMEM((t, C_HEAD_DIM), BF16), pltpu.VMEM((t, C_HEAD_DIM), BF16),
                        pltpu.VMEM((LANES, C_HEAD_DIM), F32)],
        compiler_params=_cparams(("parallel", "parallel", "arbitrary")),
        name="moba_prompt",
    )(z, z, z)


PAGES_PER_STEP = 2 * MOBA_BLOCK // PAGE_SIZE


def _moba_decode_kernel(pt_ref, z_ref, *refs, n_blocks):
    del pt_ref
    npg = PAGES_PER_STEP
    k_refs, v_refs = refs[:npg], refs[npg:2 * npg]
    o_ref = refs[2 * npg]
    m_ref, l_ref, acc_ref, gate_ref = refs[2 * npg + 1:]
    j = pl.program_id(1)
    ppb = MOBA_BLOCK // PAGE_SIZE
    past = n_blocks * MOBA_BLOCK
    scale = 1.0 / math.sqrt(C_HEAD_DIM)
    hrow = lax.broadcasted_iota(jnp.int32, (C_HEADS, 1), 0)
    slope = jnp.exp((hrow + 1).astype(F32) * (-(8.0 / C_HEADS) * math.log(2.0)))
    tcol = lax.broadcasted_iota(jnp.int32, (1, MOBA_BLOCK), 1)

    z = z_ref[...]
    qrows = jnp.concatenate([z[:, h * C_HEAD_DIM:(h + 1) * C_HEAD_DIM] for h in range(C_HEADS)], axis=0)
    qb = qrows.astype(BF16)

    for blk in range(npg // ppb):
        n = j * (npg // ppb) + blk
        kblk = jnp.concatenate([k_refs[blk * ppb + i][...] for i in range(ppb)], axis=0)
        vblk = jnp.concatenate([v_refs[blk * ppb + i][...] for i in range(ppb)], axis=0)
        kmean = jnp.mean(kblk, axis=0, keepdims=True)
        dist = (past - (n * MOBA_BLOCK + tcol)).astype(F32)
        for kvh in range(C_KV_HEADS):
            hs = slice(kvh * C_GROUP, (kvh + 1) * C_GROUP)
            cs = slice(kvh * C_HEAD_DIM, (kvh + 1) * C_HEAD_DIM)
            s = _dot_nt(qb[hs], kblk[:, cs].astype(BF16)) * scale - slope[hs] * dist
            m = jnp.max(s, axis=-1, keepdims=True)
            p = jnp.exp(s - m)
            m_ref[pl.ds(n, 1), hs, :] = m[None]
            l_ref[pl.ds(n, 1), hs, :] = jnp.sum(p, axis=-1, keepdims=True)[None]
            acc_ref[pl.ds(n, 1), hs, :] = _dot(p.astype(BF16), vblk[:, cs].astype(BF16))[None]
            gate_ref[pl.ds(n, 1), hs, :] = jnp.sum(qrows[hs] * kmean[:, cs], axis=-1, keepdims=True)[None]

    @pl.when(j == pl.num_programs(1) - 1)
    def _():
        gates = gate_ref[...]
        ms = m_ref[...]
        bidx = lax.broadcasted_iota(jnp.int32, gates.shape, 0)
        picked = jnp.zeros(gates.shape, jnp.bool_)
        for _ in range(min(MOBA_TOPK, n_blocks)):
            gmax = jnp.max(gates, axis=0, keepdims=True)
            first = jnp.min(jnp.where(gates == gmax, bidx, n_blocks), axis=0, keepdims=True)
            hit = bidx == first
            picked = picked | hit
            gates = jnp.where(hit, -jnp.inf, gates)
        knew = jnp.concatenate(
            [z[:, C_Q_W + kvh * C_HEAD_DIM:C_Q_W + (kvh + 1) * C_HEAD_DIM] for kvh in range(C_KV_HEADS)
             for _ in range(C_GROUP)], axis=0)
        vnew = jnp.concatenate(
            [z[:, C_Q_W + C_KV_W + kvh * C_HEAD_DIM:C_Q_W + C_KV_W + (kvh + 1) * C_HEAD_DIM]
             for kvh in range(C_KV_HEADS) for _ in range(C_GROUP)], axis=0)
        s_own = jnp.sum(qb.astype(F32) * knew.astype(BF16).astype(F32), axis=-1, keepdims=True) * scale
        mtot = jnp.maximum(jnp.max(jnp.where(picked, ms, NEG), axis=0), s_own)
        wgt = jnp.where(picked, jnp.exp(ms - mtot[None]), 0.0)
        p_own = jnp.exp(s_own - mtot)
        l = jnp.sum(wgt * l_ref[...], axis=0) + p_own
        acc = jnp.sum(wgt * acc_ref[...], axis=0) + p_own.astype(BF16).astype(F32) * vnew.astype(BF16).astype(F32)
        o_ref[...] = (acc / l).astype(o_ref.dtype)


def _moba_decode(z, cache_k, cache_v, page_table):
    nb, n_pages = page_table.shape
    assert n_pages % PAGES_PER_STEP == 0
    n_blocks = n_pages * PAGE_SIZE // MOBA_BLOCK
    steps = n_pages // PAGES_PER_STEP

    def page_spec(i):
        return pl.BlockSpec((None, PAGE_SIZE, C_KV_W),
                            lambda b, j, pt: (pt[b, j * PAGES_PER_STEP + i], 0, 0))

    pages = [page_spec(i) for i in range(PAGES_PER_STEP)]
    zw = z.shape[-1]
    return pl.pallas_call(
        functools.partial(_moba_decode_kernel, n_blocks=n_blocks),
        out_shape=jax.ShapeDtypeStruct((nb, C_HEADS, C_HEAD_DIM), BF16),
        grid_spec=pltpu.PrefetchScalarGridSpec(
            num_scalar_prefetch=1,
            grid=(nb, steps),
            in_specs=[pl.BlockSpec((None, 1, zw), lambda b, j, pt: (b, 0, 0))] + pages + pages,
            out_specs=pl.BlockSpec((None, C_HEADS, C_HEAD_DIM), lambda b, j, pt: (b, 0, 0)),
            scratch_shapes=[pltpu.VMEM((n_blocks, C_HEADS, 1), F32), pltpu.VMEM((n_blocks, C_HEADS, 1), F32),
                            pltpu.VMEM((n_blocks, C_HEADS, C_HEAD_DIM), F32),
                            pltpu.VMEM((n_blocks, C_HEADS, 1), F32)]),
        compiler_params=_cparams(("parallel", "arbitrary")),
        name="moba_decode",
    )(page_table, z, *([cache_k] * PAGES_PER_STEP), *([cache_v] * PAGES_PER_STEP))


def _pad_a_cols(x):
    i3 = 3 * A_WIDTH
    z = lambda n: jnp.zeros(x.shape[:-1] + (n,), x.dtype)
    return jnp.concatenate([
        x[..., :i3],
        x[..., i3:i3 + DECAY_RANK], z(LANES - DECAY_RANK),
        x[..., i3 + DECAY_RANK:i3 + DECAY_RANK + AAA_RANK], z(LANES - AAA_RANK),
        x[..., i3 + DECAY_RANK + AAA_RANK:], z(2 * LANES - GATE_RANK)], axis=-1)


def _unpad_a_cols(x):
    return jnp.concatenate([x[..., :A_OFF_W + DECAY_RANK], x[..., A_OFF_A:A_OFF_A + AAA_RANK],
                            x[..., A_OFF_G:A_OFF_G + GATE_RANK]], axis=-1)


def _pad_rows(w, n):
    return jnp.concatenate([w, jnp.zeros((n - w.shape[0],) + w.shape[1:], w.dtype)], axis=0)


def _ab_params(i, w_in_ab, mu_ab, w0, w2, a0, a2, g2, k_k, k_a, r_k, lnx_g, lnx_b, dw_w, dw_b, cn_g, cn_b):
    row = lambda x: x.reshape(1, -1)
    head = jnp.arange(A_WIDTH)[:, None] // A_HEAD == jnp.arange(LANES)[None, :]
    return {
        "w_a": _pad_a_cols(w_in_ab[i][:, :A_COLS]).astype(BF16),
        "w_b": w_in_ab[i][:, A_COLS:].astype(BF16),
        "mu": row(_pad_a_cols(mu_ab[i])),
        "w0": row(w0[i]), "a0": row(a0[i]), "k_k": row(k_k[i]), "k_a": row(k_a[i]), "r_k": row(r_k[i]),
        "lnx_g": row(lnx_g[i]), "lnx_b": row(lnx_b[i]),
        "w2": _pad_rows(w2[i], LANES).astype(BF16), "a2": _pad_rows(a2[i], LANES).astype(BF16),
        "g2": _pad_rows(g2[i], 2 * LANES).astype(BF16),
        "seg": head.astype(BF16), "segt": head.T.astype(BF16),
        "dw_w": dw_w[i], "dw_b": row(dw_b[i]), "cn_g": row(cn_g[i]), "cn_b": row(cn_b[i]),
    }


def _state_to_pairs(s):
    nb = s.shape[0]
    s = s.reshape(nb, A_HEADS // 2, 2, A_HEAD, A_HEAD)
    z = jnp.zeros_like(s[:, :, 0])
    top = jnp.concatenate([s[:, :, 0], z], axis=-1)
    bot = jnp.concatenate([z, s[:, :, 1]], axis=-1)
    return jnp.concatenate([top, bot], axis=-2)


def _pairs_to_state(s):
    nb = s.shape[0]
    return jnp.stack([s[:, :, :A_HEAD, :A_HEAD], s[:, :, A_HEAD:, A_HEAD:]], axis=2).reshape(
        nb, A_HEADS, A_HEAD, A_HEAD)


def _mixer_ab(x, g_pre, shift_prev, wkv0, conv_buf, prm, *, nb, t):
    if t >= WKV_CHUNK:
        t_pad, tt_a, tt_b, xp = t, min(t, 256), min(t, 128), x
    else:
        t_pad, tt_a, tt_b = WKV_CHUNK, WKV_CHUNK, WKV_CHUNK
        xp = jnp.pad(x.reshape(nb, t, -1), ((0, 0), (0, t_pad - t), (0, 0))).reshape(nb * t_pad, -1)
    za = _norm_matmul(xp, g_pre, prm["w_a"], F32, tn=512)
    zb = _norm_matmul(xp, g_pre, prm["w_b"], F32, tn=512)
    ya, shift, s_new = _wkv_mix(za, _pad_a_cols(shift_prev)[:, None, :], _state_to_pairs(wkv0), prm,
                                nb=nb, t_pad=t_pad, t_valid=t, tt=tt_a)
    hist = jnp.pad(conv_buf, ((0, 0), (CONV_HALO - (CONV_W - 1), 0), (0, 0)))
    yb, conv_new = _conv_module(zb, hist, prm, nb=nb, t_pad=t_pad, t_valid=t, tt=tt_b)
    y = jnp.concatenate([ya, yb], axis=-1)
    if t_pad != t:
        y = y.reshape(nb, t_pad, -1)[:, :t].reshape(nb * t, -1)
    return y, _unpad_a_cols(shift[:, 0]), _pairs_to_state(s_new), conv_new


def _rest_of_layer(x, l, k_arr, v_arr, k_col, v_col, wts, *, nb, t):
    (g_x_pre, g_x_post, w_xq, w_xo, g_ffn_pre, g_ffn_post, w_gu, w_down) = wts
    q = _norm_matmul(x, g_x_pre[l], w_xq, BF16, tn=X_W)
    rep = 1 if t >= 16 else 16
    q3 = q.reshape(nb, t, X_W)
    if rep > 1:
        q3 = jnp.repeat(q3, rep, axis=1)
    o = _cross_attention(q3, k_arr, v_arr, k_col, v_col, tq=min(t * rep, 512))
    o = o[:, ::rep].reshape(nb * t, X_W)
    x = _matmul_norm_res(o, w_xo, g_x_post[l], x, tk=X_W)
    h = _norm_matmul(x, g_ffn_pre[l], w_gu, BF16, tn=512, swiglu_half=D_FF)
    return _matmul_norm_res(h, w_down, g_ffn_post[l], x, tk=512)


def kernel(x_prompt, x_sample, cache_k, cache_v, state_wkv, state_shift, state_conv, cache_mem_k, cache_mem_v, page_table, mem_prompt, w_in_ab, w_out_ab, mu_ab, w0, w2, a0, a2, g2, k_k, k_a, r_k, lnx_g, lnx_b, dw_w, dw_b, cn_g, cn_b, w_in_c, w_out_c, g_mix_pre, g_mix_post, g_x_pre, g_x_post, g_mem, w_xq, w_xkv, w_xo, g_ffn_pre, g_ffn_post, w_ffn_gu, w_ffn_down):
    bp, tp, d = x_prompt.shape
    bs, ts, _ = x_sample.shape
    depth = g_mix_pre.shape[0]
    xp = x_prompt.reshape(bp * tp, d)
    xs = x_sample.reshape(bs * ts, d)
    mem = mem_prompt.reshape(bp * MEM_LEN, d)
    outs = {n: [] for n in ("kp", "vp", "wkvp", "shp", "cvp", "mkp", "mvp", "ks", "vs", "wkvs", "shs", "cvs")}
    for l in range(depth):
        i = l // 2
        if l % 2 == 0:
            prm = _ab_params(i, w_in_ab, mu_ab, w0, w2, a0, a2, g2, k_k, k_a, r_k, lnx_g, lnx_b,
                             dw_w, dw_b, cn_g, cn_b)
            w_out = w_out_ab[i].astype(BF16)
            fp, shp, wkvp, cvp = _mixer_ab(
                xp, g_mix_pre[l], jnp.zeros((bp, A_COLS), F32), jnp.zeros((bp, A_HEADS, A_HEAD, A_HEAD), F32),
                jnp.zeros((bp, CONV_W - 1, B_WIDTH), F32), prm, nb=bp, t=tp)
            fs, shs, wkvs, cvs = _mixer_ab(xs, g_mix_pre[l], state_shift[i], state_wkv[i], state_conv[i], prm,
                                           nb=bs, t=ts)
            for n, val in (("shp", shp), ("wkvp", wkvp), ("cvp", cvp), ("shs", shs), ("wkvs", wkvs), ("cvs", cvs)):
                outs[n].append(val)
        else:
            w_in = w_in_c[i].astype(BF16)
            w_out = w_out_c[i].astype(BF16)
            zp = _norm_matmul(xp, g_mix_pre[l], w_in, F32, tn=512)
            fp = _moba_prompt(zp, nb=bp, t=tp)
            zs = _norm_matmul(xs, g_mix_pre[l], w_in, F32, tn=512)
            n_phys = cache_k.shape[1]
            fs = _moba_decode(zs.reshape(bs, ts, -1), cache_k[i].reshape(n_phys, PAGE_SIZE, C_KV_W),
                              cache_v[i].reshape(n_phys, PAGE_SIZE, C_KV_W), page_table).reshape(bs * ts, C_Q_W)
            kv = lambda z, nb, t, off: z[:, off:off + C_KV_W].reshape(nb, t, C_KV_HEADS, C_HEAD_DIM)
            outs["kp"].append(kv(zp, bp, tp, C_Q_W))
            outs["vp"].append(kv(zp, bp, tp, C_Q_W + C_KV_W))
            outs["ks"].append(kv(zs, bs, ts, C_Q_W))
            outs["vs"].append(kv(zs, bs, ts, C_Q_W + C_KV_W))
        xp = _matmul_norm_res(fp, w_out, g_mix_post[l], xp, tk=512)
        xs = _matmul_norm_res(fs, w_out, g_mix_post[l], xs, tk=512)
        mkv = _norm_matmul(mem, g_mem[l], w_xkv[l].astype(BF16), F32, tn=512)
        outs["mkp"].append(mkv[:, :X_W].reshape(bp, MEM_LEN, X_HEADS, X_HEAD_DIM))
        outs["mvp"].append(mkv[:, X_W:].reshape(bp, MEM_LEN, X_HEADS, X_HEAD_DIM))
        wts = (g_x_pre, g_x_post, w_xq[l].astype(BF16), w_xo[l].astype(BF16), g_ffn_pre, g_ffn_post,
               w_ffn_gu[l].astype(BF16), w_ffn_down[l].astype(BF16))
        xp = _rest_of_layer(xp, l, mkv, mkv, 0, 1, wts, nb=bp, t=tp)
        xs = _rest_of_layer(xs, l, cache_mem_k[l].reshape(bs * MEM_LEN, X_W),
                            cache_mem_v[l].reshape(bs * MEM_LEN, X_W), 0, 0, wts, nb=bs, t=ts)
    st = lambda n: jnp.stack(outs[n])
    return (xp.reshape(bp, tp, d), xs.reshape(bs, ts, d), st("kp"), st("vp"), st("wkvp"), st("shp"), st("cvp"),
            st("mkp"), st("mvp"), st("ks"), st("vs"), st("wkvs"), st("shs"), st("cvs"))
```

```python
import functools
import math

import jax
import jax.numpy as jnp
from jax import lax
from jax.experimental import pallas as pl
from jax.experimental.pallas import tpu as pltpu

F32 = jnp.float32
BF16 = jnp.bfloat16

D_MODEL = 2048
PAGE_SIZE = 128
A_WIDTH = 1024
A_HEAD = 64
A_HEADS = 16
DECAY_RANK = 64
AAA_RANK = 64
GATE_RANK = 160
A_COLS = 3 * A_WIDTH + DECAY_RANK + AAA_RANK + GATE_RANK
GN_EPS = 64e-5
B_WIDTH = 1024
CONV_W = 31
C_HEADS = 16
C_HEAD_DIM = 128
C_KV_HEADS = 4
C_GROUP = 4
C_Q_W = 2048
C_KV_W = 512
MOBA_BLOCK = 256
MOBA_TOPK = 3
Q_BLOCK = MOBA_BLOCK
MEM_LEN = 256
X_HEADS = 4
X_HEAD_DIM = 128
X_W = 512
D_FF = 5632
RMS_EPS = 1e-6
LN_EPS = 1e-5
NEG = -1e30

LANES = 128
A_OFF_W = 3 * A_WIDTH
A_OFF_A = A_OFF_W + LANES
A_OFF_G = A_OFF_A + LANES
A_PAD = A_OFF_G + 2 * LANES
WKV_CHUNK = 64
VMEM_LIMIT = 56 * 1024 * 1024


def _cparams(sem):
    return pltpu.CompilerParams(dimension_semantics=sem, vmem_limit_bytes=VMEM_LIMIT)


def _dot(a, b):
    return jnp.dot(a, b, preferred_element_type=F32)


def _dot_nt(a, b):
    return lax.dot_general(a, b, (((1,), (1,)), ((), ())), preferred_element_type=F32)


def _dot_tn(a, b):
    return lax.dot_general(a, b, (((0,), (0,)), ((), ())), preferred_element_type=F32)


def _split2(x):
    hi = x.astype(BF16)
    lo = (x - hi.astype(F32)).astype(BF16)
    return hi, lo


def _dot_x3(dot, a, b):
    ah, al = _split2(a)
    bh, bl = _split2(b)
    return dot(ah, bh) + (dot(ah, bl) + dot(al, bh))


def _dot_exact_rhs(a, b_bf16):
    hi, lo = _split2(a)
    return _dot(hi, b_bf16) + _dot(lo, b_bf16)


def _dot_exact_lhs(a_bf16, b):
    hi, lo = _split2(b)
    return _dot(a_bf16, hi) + _dot(a_bf16, lo)


def _sigmoid(x):
    return 1.0 / (1.0 + jnp.exp(-x))


def _norm_mm_kernel(x_ref, g_ref, *refs, n_w):
    w_refs, o_ref, xn_ref = refs[:n_w], refs[n_w], refs[n_w + 1]

    @pl.when(pl.program_id(1) == 0)
    def _():
        x = x_ref[...]
        ms = jnp.mean(x * x, axis=-1, keepdims=True)
        xn_ref[...] = (x * lax.rsqrt(ms + RMS_EPS) * g_ref[...]).astype(BF16)

    xn = xn_ref[...]
    if n_w == 1:
        y = _dot(xn, w_refs[0][...])
    else:
        gate = _dot(xn, w_refs[0][...])
        y = gate * _sigmoid(gate) * _dot(xn, w_refs[1][...])
    o_ref[...] = y.astype(o_ref.dtype)


def _norm_matmul(x, g, w, out_dtype, *, tn, swiglu_half=None):
    m, d = x.shape
    n = swiglu_half if swiglu_half else w.shape[1]
    tm = min(m, 1024)
    assert m % tm == 0 and n % tn == 0
    w_specs = [pl.BlockSpec((d, tn), lambda i, j: (0, j))]
    operands = [x, g.reshape(1, d), w]
    if swiglu_half:
        off = swiglu_half // tn
        w_specs.append(pl.BlockSpec((d, tn), lambda i, j: (0, j + off)))
        operands.append(w)
    return pl.pallas_call(
        functools.partial(_norm_mm_kernel, n_w=len(w_specs)),
        out_shape=jax.ShapeDtypeStruct((m, n), out_dtype),
        grid=(m // tm, n // tn),
        in_specs=[pl.BlockSpec((tm, d), lambda i, j: (i, 0)),
                  pl.BlockSpec((1, d), lambda i, j: (0, 0))] + w_specs,
        out_specs=pl.BlockSpec((tm, tn), lambda i, j: (i, j)),
        scratch_shapes=[pltpu.VMEM((tm, d), BF16)],
        compiler_params=_cparams(("parallel", "arbitrary")),
        name="norm_matmul",
    )(*operands)


def _mm_norm_res_kernel(x_ref, w_ref, g_ref, r_ref, o_ref):
    k = pl.program_id(1)
    y = _dot(x_ref[...], w_ref[...])

    @pl.when(k == 0)
    def _():
        o_ref[...] = y

    @pl.when(k > 0)
    def _():
        o_ref[...] += y

    @pl.when(k == pl.num_programs(1) - 1)
    def _():
        acc = o_ref[...]
        ms = jnp.mean(acc * acc, axis=-1, keepdims=True)
        o_ref[...] = r_ref[...] + acc * lax.rsqrt(ms + RMS_EPS) * g_ref[...]


def _matmul_norm_res(x, w, g, res):
    m, kd = x.shape
    n = w.shape[1]
    tm = min(m, 512)
    tk = kd if kd <= 2048 else kd // 4
    assert m % tm == 0 and kd % tk == 0 and tk % LANES == 0
    return pl.pallas_call(
        _mm_norm_res_kernel,
        out_shape=jax.ShapeDtypeStruct((m, n), F32),
        grid=(m // tm, kd // tk),
        in_specs=[pl.BlockSpec((tm, tk), lambda i, k: (i, k)),
                  pl.BlockSpec((tk, n), lambda i, k: (k, 0)),
                  pl.BlockSpec((1, n), lambda i, k: (0, 0)),
                  pl.BlockSpec((tm, n), lambda i, k: (i, 0))],
        out_specs=pl.BlockSpec((tm, n), lambda i, k: (i, 0)),
        compiler_params=_cparams(("parallel", "arbitrary")),
        name="matmul_norm_res",
    )(x, w, g.reshape(1, n), res)


def _xattn_kernel(q_ref, k_ref, v_ref, o_ref):
    scale = 1.0 / math.sqrt(X_HEAD_DIM)
    for h in range(X_HEADS):
        sl = slice(h * X_HEAD_DIM, (h + 1) * X_HEAD_DIM)
        s = _dot_nt(q_ref[:, sl], k_ref[:, sl].astype(BF16)) * scale
        p = jnp.exp(s - jnp.max(s, axis=-1, keepdims=True))
        l = jnp.sum(p, axis=-1, keepdims=True)
        o = _dot(p.astype(BF16), v_ref[:, sl].astype(BF16))
        o_ref[:, sl] = (o / l).astype(o_ref.dtype)


def _cross_attention(q, k_arr, v_arr, k_col, v_col, *, tq):
    b, t, _ = q.shape
    assert t % tq == 0
    return pl.pallas_call(
        _xattn_kernel,
        out_shape=jax.ShapeDtypeStruct((b, t, X_W), BF16),
        grid=(b, t // tq),
        in_specs=[pl.BlockSpec((None, tq, X_W), lambda i, j: (i, j, 0)),
                  pl.BlockSpec((MEM_LEN, X_W), lambda i, j: (i, k_col)),
                  pl.BlockSpec((MEM_LEN, X_W), lambda i, j: (i, v_col))],
        out_specs=pl.BlockSpec((None, tq, X_W), lambda i, j: (i, j, 0)),
        compiler_params=_cparams(("parallel", "arbitrary")),
        name="cross_attention",
    )(q, k_arr, v_arr)


def _wkv_tile_kernel(za_ref, sp_ref, s0_ref, mu_ref, w0_ref, a0_ref, kk_ref, ka_ref, rk_ref,
                     lng_ref, lnb_ref, w2_ref, a2_ref, g2_ref, seg_ref, segt_ref,
                     ya_ref, shift_ref, s_ref,
                     rh_ref, kap_ref, kh_ref, bh_ref, v_ref, khp_ref, bhp_ref, pt_ref, y_ref,
                     *, tt, t_valid):
    c = WKV_CHUNK
    ti = pl.program_id(1)

    @pl.when(ti == 0)
    def _():
        s_ref[...] = s0_ref[...]
        shift_ref[...] = sp_ref[...]

    za = za_ref[...]
    row = lax.broadcasted_iota(jnp.int32, (tt, 1), 0)
    prev = jnp.where(row == 0, shift_ref[...], pltpu.roll(za, 1, axis=0))
    last = (t_valid - 1) % tt
    shift_ref[...] = za[last:last + 1, :]
    mix = za + (prev - za) * mu_ref[...]
    valid = (ti * tt + row) < t_valid

    r = mix[:, 0:A_WIDTH]
    k = mix[:, A_WIDTH:2 * A_WIDTH]
    v = mix[:, 2 * A_WIDTH:3 * A_WIDTH]
    wl = mix[:, A_OFF_W:A_OFF_A]
    al = mix[:, A_OFF_A:A_OFF_G]
    gl = mix[:, A_OFF_G:A_PAD]
    xw = w0_ref[...] + _dot(jnp.tanh(wl).astype(BF16), w2_ref[...])
    lw = jnp.where(valid, -_sigmoid(xw) * math.exp(-0.5), 0.0)
    a = _sigmoid(a0_ref[...] + _dot(al.astype(BF16), a2_ref[...]))
    g = _dot(_sigmoid(gl).astype(BF16), g2_ref[...])

    seg = seg_ref[...]
    segt = segt_ref[...]

    def head_sum(x):
        return _dot_exact_rhs(_dot_exact_rhs(x, seg), segt)

    kk = k * kk_ref[...]
    kk = kk / jnp.maximum(jnp.sqrt(head_sum(kk * kk)), 1e-12)
    kk = jnp.where(valid, kk, 0.0)
    kt = jnp.where(valid, k * (1.0 + (a - 1.0) * ka_ref[...]), 0.0)
    b = kk * a
    v = jnp.where(valid, v, 0.0)

    ri = lax.broadcasted_iota(jnp.int32, (tt, tt), 0)
    ci = lax.broadcasted_iota(jnp.int32, (tt, tt), 1)
    same = (ri // c) == (ci // c)
    cum = _dot_exact_lhs(jnp.where(same & (ci <= ri), 1.0, 0.0).astype(BF16), lw)
    tot = _dot_exact_lhs(jnp.where(same, 1.0, 0.0).astype(BF16), lw)
    p_inv = jnp.exp(-cum)
    p_end = jnp.exp(tot - cum)
    rh_ref[...] = r * jnp.exp(cum)
    kap_ref[...] = kk * jnp.exp(cum - lw)
    kh_ref[...] = kt * p_inv
    bh_ref[...] = b * p_inv
    v_ref[...] = v
    khp_ref[...] = kt * p_end
    bhp_ref[...] = b * p_end
    pt_ref[...] = jnp.exp(tot)

    lane = lax.broadcasted_iota(jnp.int32, (1, LANES), 1)
    m0 = (lane < A_HEAD).astype(F32)
    m1 = 1.0 - m0
    r2 = lax.broadcasted_iota(jnp.int32, (2 * c, 2 * c), 0)
    c2 = lax.broadcasted_iota(jnp.int32, (2 * c, 2 * c), 1)
    strict = (r2 % c) > (c2 % c)
    incl = (r2 % c) >= (c2 % c)
    c2x = 2 * c
    bf = lambda x: x.astype(BF16)

    def stack(x):
        return jnp.concatenate([x * m0, x * m1], axis=0)

    def chunk(ci_, carry):
        rows = pl.ds(pl.multiple_of(ci_ * c, c), c)
        pairs = range(A_HEADS // 2)
        cols = [slice(p * LANES, (p + 1) * LANES) for p in pairs]
        vst = [stack(v_ref[rows, cols[p]]) for p in pairs]
        s0 = [s_ref[p] for p in pairs]
        lhs = [bf(jnp.concatenate([stack(kap_ref[rows, cols[p]]), stack(rh_ref[rows, cols[p]])], axis=0))
               for p in pairs]
        gram = [_dot_nt(lhs[p], bf(jnp.concatenate([stack(kh_ref[rows, cols[p]]), stack(bh_ref[rows, cols[p]])],
                                                   axis=0))) for p in pairs]
        from_s0 = [_dot_nt(lhs[p], bf(s0[p])) for p in pairs]
        x = [from_s0[p][:c2x] + _dot(bf(jnp.where(strict, gram[p][:c2x, :c2x], 0.0)), bf(vst[p])) for p in pairs]
        lp = [jnp.where(strict, gram[p][:c2x, c2x:], 0.0) for p in pairs]
        sq = [_dot(bf(lp[p]), bf(jnp.concatenate([lp[p], x[p]], axis=1))) for p in pairs]
        lp = [sq[p][:, :c2x] for p in pairs]
        x = [x[p] - sq[p][:, c2x:] for p in pairs]
        n = 2
        while 2 * n < c:
            sq = [_dot(bf(lp[p]), bf(jnp.concatenate([lp[p], x[p]], axis=1))) for p in pairs]
            lp = [sq[p][:, :c2x] for p in pairs]
            x = [x[p] + sq[p][:, c2x:] for p in pairs]
            n *= 2
        x = [x[p] + _dot(bf(lp[p]), bf(x[p])) for p in pairs]
        for p in pairs:
            tri = jnp.concatenate([jnp.where(incl, gram[p][c2x:, :c2x], 0.0),
                                   jnp.where(incl, -gram[p][c2x:, c2x:], 0.0)], axis=1)
            y = from_s0[p][c2x:] + _dot(bf(tri), bf(jnp.concatenate([vst[p], x[p]], axis=0)))
            y_ref[rows, cols[p]] = y[:c] + y[c:]
            decayed = jnp.concatenate([stack(khp_ref[rows, cols[p]]), stack(bhp_ref[rows, cols[p]])], axis=0)
            s_ref[p] = (s0[p] * pt_ref[pl.ds(pl.multiple_of(ci_ * c, c), 1), cols[p]]
                        + _dot_tn(bf(jnp.concatenate([vst[p], -x[p]], axis=0)), bf(decayed)))
        return carry

    lax.fori_loop(0, tt // c, chunk, 0)

    y = y_ref[...]
    mean = head_sum(y) * (1.0 / A_HEAD)
    yc = y - mean
    var = head_sum(yc * yc) * (1.0 / A_HEAD)
    yn = yc * lax.rsqrt(var + GN_EPS) * lng_ref[...] + lnb_ref[...]
    yn = yn + head_sum(r * kt * rk_ref[...]) * v
    ya_ref[...] = (yn * g).astype(ya_ref.dtype)


def _wkv_mix(za, shift_prev, s0, prm, *, nb, t_pad, t_valid, tt):
    nt = t_pad // tt
    vec = lambda n: pl.BlockSpec((1, n), lambda i, j: (0, 0))
    full = lambda a: pl.BlockSpec(a.shape, lambda i, j: (0,) * a.ndim)
    state_spec = pl.BlockSpec((None, A_HEADS // 2, LANES, LANES), lambda i, j: (i, 0, 0, 0))
    shift_spec = pl.BlockSpec((None, 1, A_PAD), lambda i, j: (i, 0, 0))
    vecs = [prm[n] for n in ("mu", "w0", "a0", "k_k", "k_a", "r_k", "lnx_g", "lnx_b")]
    mats = [prm[n] for n in ("w2", "a2", "g2", "seg", "segt")]
    return pl.pallas_call(
        functools.partial(_wkv_tile_kernel, tt=tt, t_valid=t_valid),
        out_shape=(jax.ShapeDtypeStruct((nb * t_pad, A_WIDTH), BF16),
                   jax.ShapeDtypeStruct((nb, 1, A_PAD), F32),
                   jax.ShapeDtypeStruct((nb, A_HEADS // 2, LANES, LANES), F32)),
        grid=(nb, nt),
        in_specs=[pl.BlockSpec((tt, A_PAD), lambda i, j: (i * nt + j, 0)), shift_spec, state_spec]
                 + [vec(x.shape[1]) for x in vecs] + [full(x) for x in mats],
        out_specs=(pl.BlockSpec((tt, A_WIDTH), lambda i, j: (i * nt + j, 0)), shift_spec, state_spec),
        scratch_shapes=[pltpu.VMEM((tt, A_WIDTH), F32)] * 9,
        compiler_params=_cparams(("parallel", "arbitrary")),
        name="wkv_mix",
    )(za, shift_prev, s0, *vecs, *mats)


CONV_HALO = 32


def _conv_kernel(zb_ref, cb_ref, w_ref, b_ref, g_ref, be_ref, o_ref, st_ref, ext_ref, conv_ref,
                 *, tt, t_last):
    ti = pl.program_id(1)
    pad = CONV_HALO - (CONV_W - 1)

    @pl.when(ti == 0)
    def _():
        ext_ref[0:CONV_HALO, :] = cb_ref[...]

    zb = zb_ref[...]
    ext_ref[CONV_HALO:CONV_HALO + tt, :] = zb[:, :B_WIDTH] * _sigmoid(zb[:, B_WIDTH:])
    rb = min(tt, 64)
    for r0 in range(0, tt, rb):
        for c0 in range(0, B_WIDTH, LANES):
            cols = slice(c0, c0 + LANES)
            acc = jnp.broadcast_to(b_ref[:, cols], (rb, LANES))
            for j in range(CONV_W):
                acc = acc + ext_ref[r0 + pad + j:r0 + pad + j + rb, cols] * w_ref[j:j + 1, cols]
            conv_ref[r0:r0 + rb, cols] = acc
    conv = conv_ref[...]
    mean = jnp.mean(conv, axis=-1, keepdims=True)
    xc = conv - mean
    var = jnp.mean(xc * xc, axis=-1, keepdims=True)
    y = xc * lax.rsqrt(var + LN_EPS) * g_ref[...] + be_ref[...]
    o_ref[...] = (y * _sigmoid(y)).astype(o_ref.dtype)
    st_ref[...] = ext_ref[t_last + pad:t_last + pad + CONV_W - 1, :]
    if tt >= CONV_HALO:
        ext_ref[0:CONV_HALO, :] = ext_ref[tt:tt + CONV_HALO, :]


def _conv_module(zb, conv_buf, prm, *, nb, t_pad, t_valid, tt):
    nt = t_pad // tt
    assert nt == 1 or tt >= CONV_HALO
    vec = pl.BlockSpec((1, B_WIDTH), lambda i, j: (0, 0))
    return pl.pallas_call(
        functools.partial(_conv_kernel, tt=tt, t_last=t_valid - (nt - 1) * tt),
        out_shape=(jax.ShapeDtypeStruct((nb * t_pad, B_WIDTH), BF16),
                   jax.ShapeDtypeStruct((nb, CONV_W - 1, B_WIDTH), F32)),
        grid=(nb, nt),
        in_specs=[pl.BlockSpec((tt, 2 * B_WIDTH), lambda i, j: (i * nt + j, 0)),
                  pl.BlockSpec((None, CONV_HALO, B_WIDTH), lambda i, j: (i, 0, 0)),
                  pl.BlockSpec((CONV_W, B_WIDTH), lambda i, j: (0, 0)), vec, vec, vec],
        out_specs=(pl.BlockSpec((tt, B_WIDTH), lambda i, j: (i * nt + j, 0)),
                   pl.BlockSpec((None, CONV_W - 1, B_WIDTH), lambda i, j: (i, 0, 0))),
        scratch_shapes=[pltpu.VMEM((CONV_HALO + tt, B_WIDTH), F32), pltpu.VMEM((tt, B_WIDTH), F32)],
        compiler_params=_cparams(("parallel", "arbitrary")),
        name="conv_module",
    )(zb, conv_buf, prm["dw_w"], prm["dw_b"], prm["cn_g"], prm["cn_b"])


def _top_blocks(gate, blk, n_sel):
    sel = []
    for _ in range(n_sel):
        m = jnp.max(gate, axis=0, keepdims=True)
        idx = jnp.min(jnp.where(gate == m, blk, gate.shape[0]), axis=0, keepdims=True)
        sel.append(idx)
        gate = jnp.where(blk == idx, -jnp.inf, gate)
    return sel


def _moba_prompt_kernel(q_ref, k_ref, v_ref, o_ref, kb_ref, vt_ref, km_ref, *, t):
    kvh = pl.program_id(1)
    qi = pl.program_id(2)
    nb = t // MOBA_BLOCK
    nq = C_GROUP * Q_BLOCK

    @pl.when(qi == 0)
    def _():
        km_ref[...] = jnp.zeros_like(km_ref)
        for n in range(nb):
            kblk = k_ref[n * MOBA_BLOCK:(n + 1) * MOBA_BLOCK, :]
            kb_ref[n] = kblk.astype(BF16)
            vt_ref[n] = v_ref[n * MOBA_BLOCK:(n + 1) * MOBA_BLOCK, :].T.astype(BF16)
            km_ref[n:n + 1, :] = jnp.mean(kblk, axis=0, keepdims=True)

    q = q_ref[...]
    qs = jnp.concatenate([q[:, g * C_HEAD_DIM:(g + 1) * C_HEAD_DIM] for g in range(C_GROUP)], axis=0)
    own = (qi * Q_BLOCK) // MOBA_BLOCK
    blk = lax.broadcasted_iota(jnp.int32, (km_ref.shape[0], 1), 0)
    gate = _dot_x3(_dot_nt, km_ref[...], qs)
    gate = jnp.where(blk < own, gate, jnp.where(blk < nb, NEG, -jnp.inf))
    sel = _top_blocks(gate, blk, MOBA_TOPK)
    sel = [jnp.where(own > r, sel[r], -1) for r in range(MOBA_TOPK)]

    col = lax.broadcasted_iota(jnp.int32, (1, nq), 1)
    head = (kvh * C_GROUP + col // Q_BLOCK + 1).astype(F32)
    slope = jnp.exp(head * (-(8.0 / C_HEADS) * math.log(2.0)))
    pos = qi * Q_BLOCK + col % Q_BLOCK
    trow = lax.broadcasted_iota(jnp.int32, (MOBA_BLOCK, 1), 0)
    tab = trow.astype(F32) * slope
    qb = (qs * (1.0 / math.sqrt(C_HEAD_DIM))).astype(BF16)

    def attend(n, s, carry):
        m, l, acc = carry
        m_new = jnp.maximum(m, jnp.max(s, axis=0, keepdims=True))
        alpha = jnp.exp(m - m_new)
        p = jnp.exp(s - m_new)
        l = alpha * l + jnp.sum(p, axis=0, keepdims=True)
        acc = alpha * acc + _dot(vt_ref[n], p.astype(BF16))
        return m_new, l, acc

    def past_block(n, carry):
        picked = (sel[0] == n) | (sel[1] == n) | (sel[2] == n)
        cvec = jnp.where(picked, slope * (n * MOBA_BLOCK - pos).astype(F32), NEG)
        return attend(n, _dot_nt(kb_ref[n], qb) + tab + cvec, carry)

    init = (jnp.full((1, nq), NEG, F32), jnp.zeros((1, nq), F32), jnp.zeros((C_HEAD_DIM, nq), F32))
    carry = lax.fori_loop(0, own, past_block, init)
    s = _dot_nt(kb_ref[own], qb) + tab + slope * (own * MOBA_BLOCK - pos).astype(F32)
    s = jnp.where(trow <= pos - own * MOBA_BLOCK, s, NEG)
    _, l, acc = attend(own, s, carry)
    o = acc / l
    for g in range(C_GROUP):
        cols = slice(g * Q_BLOCK, (g + 1) * Q_BLOCK)
        o_ref[:, g * C_HEAD_DIM:(g + 1) * C_HEAD_DIM] = o[:, cols].T.astype(o_ref.dtype)


def _moba_prompt(z, *, nb, t):
    nq = t // Q_BLOCK
    kcol = C_Q_W // C_HEAD_DIM
    vcol = (C_Q_W + C_KV_W) // C_HEAD_DIM
    return pl.pallas_call(
        functools.partial(_moba_prompt_kernel, t=t),
        out_shape=jax.ShapeDtypeStruct((nb * t, C_Q_W), BF16),
        grid=(nb, C_KV_HEADS, nq),
        in_specs=[pl.BlockSpec((Q_BLOCK, C_GROUP * C_HEAD_DIM), lambda b, h, i: (b * nq + i, h)),
                  pl.BlockSpec((t, C_HEAD_DIM), lambda b, h, i: (b, kcol + h)),
                  pl.BlockSpec((t, C_HEAD_DIM), lambda b, h, i: (b, vcol + h))],
        out_specs=pl.BlockSpec((Q_BLOCK, C_GROUP * C_HEAD_DIM), lambda b, h, i: (b * nq + i, h)),
        scratch_shapes=[pltpu.VMEM((t // MOBA_BLOCK, MOBA_BLOCK, C_HEAD_DIM), BF16),
                        pltpu.VMEM((t // MOBA_BLOCK, C_HEAD_DIM, MOBA_BLOCK), BF16),
                        pltpu.VMEM((-(-t // MOBA_BLOCK // 8) * 8, C_HEAD_DIM), F32)],
        compiler_params=_cparams(("parallel", "parallel", "arbitrary")),
        name="moba_prompt",
    )(z, z, z)


PAGES_PER_STEP = 2 * MOBA_BLOCK // PAGE_SIZE


def _moba_decode_kernel(pt_ref, z_ref, *refs, n_blocks):
    del pt_ref
    npg = PAGES_PER_STEP
    k_refs, v_refs = refs[:npg], refs[npg:2 * npg]
    o_ref = refs[2 * npg]
    m_ref, l_ref, acc_ref, gate_ref = refs[2 * npg + 1:]
    j = pl.program_id(1)
    ppb = MOBA_BLOCK // PAGE_SIZE
    past = n_blocks * MOBA_BLOCK
    scale = 1.0 / math.sqrt(C_HEAD_DIM)
    hrow = lax.broadcasted_iota(jnp.int32, (C_HEADS, 1), 0)
    slope = jnp.exp((hrow + 1).astype(F32) * (-(8.0 / C_HEADS) * math.log(2.0)))
    tcol = lax.broadcasted_iota(jnp.int32, (1, MOBA_BLOCK), 1)

    z = z_ref[...]
    qrows = jnp.concatenate([z[:, h * C_HEAD_DIM:(h + 1) * C_HEAD_DIM] for h in range(C_HEADS)], axis=0)
    qb = qrows.astype(BF16)
    qbd = jnp.concatenate([jnp.where(hrow // C_GROUP == kvh, qrows, 0.0) for kvh in range(C_KV_HEADS)],
                          axis=1).astype(BF16)

    def own_kv_cols(x):
        return jnp.concatenate([x[kvh * C_GROUP:(kvh + 1) * C_GROUP, kvh * C_HEAD_DIM:(kvh + 1) * C_HEAD_DIM]
                                for kvh in range(C_KV_HEADS)], axis=0)

    for blk in range(npg // ppb):
        n = j * (npg // ppb) + blk
        kblk = jnp.concatenate([k_refs[blk * ppb + i][...] for i in range(ppb)], axis=0)
        vblk = jnp.concatenate([v_refs[blk * ppb + i][...] for i in range(ppb)], axis=0)
        kmean = jnp.mean(kblk, axis=0, keepdims=True)
        dist = (past - (n * MOBA_BLOCK + tcol)).astype(F32)
        s = _dot_nt(qbd, kblk.astype(BF16)) * scale - slope * dist
        m = jnp.max(s, axis=-1, keepdims=True)
        p = jnp.exp(s - m)
        m_ref[pl.ds(n, 1)] = m[None]
        l_ref[pl.ds(n, 1)] = jnp.sum(p, axis=-1, keepdims=True)[None]
        acc_ref[pl.ds(n, 1)] = own_kv_cols(_dot(p.astype(BF16), vblk.astype(BF16)))[None]
        kmrows = own_kv_cols(jnp.broadcast_to(kmean, (C_HEADS, C_KV_W)))
        gate_ref[pl.ds(n, 1)] = jnp.sum(qrows * kmrows, axis=-1, keepdims=True)[None]

    @pl.when(j == pl.num_programs(1) - 1)
    def _():
        gates = gate_ref[...]
        ms = m_ref[...]
        bidx = lax.broadcasted_iota(jnp.int32, gates.shape, 0)
        picked = jnp.zeros(gates.shape, jnp.bool_)
        for _ in range(min(MOBA_TOPK, n_blocks)):
            gmax = jnp.max(gates, axis=0, keepdims=True)
            first = jnp.min(jnp.where(gates == gmax, bidx, n_blocks), axis=0, keepdims=True)
            hit = bidx == first
            picked = picked | hit
            gates = jnp.where(hit, -jnp.inf, gates)
        knew = jnp.concatenate(
            [z[:, C_Q_W + kvh * C_HEAD_DIM:C_Q_W + (kvh + 1) * C_HEAD_DIM] for kvh in range(C_KV_HEADS)
             for _ in range(C_GROUP)], axis=0)
        vnew = jnp.concatenate(
            [z[:, C_Q_W + C_KV_W + kvh * C_HEAD_DIM:C_Q_W + C_KV_W + (kvh + 1) * C_HEAD_DIM]
             for kvh in range(C_KV_HEADS) for _ in range(C_GROUP)], axis=0)
        s_own = jnp.sum(qb.astype(F32) * knew.astype(BF16).astype(F32), axis=-1, keepdims=True) * scale
        mtot = jnp.maximum(jnp.max(jnp.where(picked, ms, NEG), axis=0), s_own)
        wgt = jnp.where(picked, jnp.exp(ms - mtot[None]), 0.0)
        p_own = jnp.exp(s_own - mtot)
        l = jnp.sum(wgt * l_ref[...], axis=0) + p_own
        acc = jnp.sum(wgt * acc_ref[...], axis=0) + p_own.astype(BF16).astype(F32) * vnew.astype(BF16).astype(F32)
        o_ref[...] = (acc / l).astype(o_ref.dtype)


def _moba_decode(z, cache_k, cache_v, page_table):
    nb, n_pages = page_table.shape
    assert n_pages % PAGES_PER_STEP == 0
    n_blocks = n_pages * PAGE_SIZE // MOBA_BLOCK
    steps = n_pages // PAGES_PER_STEP

    def page_spec(i):
        return pl.BlockSpec((None, PAGE_SIZE, C_KV_W),
                            lambda b, j, pt: (pt[b, j * PAGES_PER_STEP + i], 0, 0))

    pages = [page_spec(i) for i in range(PAGES_PER_STEP)]
    zw = z.shape[-1]
    return pl.pallas_call(
        functools.partial(_moba_decode_kernel, n_blocks=n_blocks),
        out_shape=jax.ShapeDtypeStruct((nb, C_HEADS, C_HEAD_DIM), BF16),
        grid_spec=pltpu.PrefetchScalarGridSpec(
            num_scalar_prefetch=1,
            grid=(nb, steps),
            in_specs=[pl.BlockSpec((None, 1, zw), lambda b, j, pt: (b, 0, 0))] + pages + pages,
            out_specs=pl.BlockSpec((None, C_HEADS, C_HEAD_DIM), lambda b, j, pt: (b, 0, 0)),
            scratch_shapes=[pltpu.VMEM((n_blocks, C_HEADS, 1), F32), pltpu.VMEM((n_blocks, C_HEADS, 1), F32),
                            pltpu.VMEM((n_blocks, C_HEADS, C_HEAD_DIM), F32),
                            pltpu.VMEM((n_blocks, C_HEADS, 1), F32)]),
        compiler_params=_cparams(("parallel", "arbitrary")),
        name="moba_decode",
    )(page_table, z, *([cache_k] * PAGES_PER_STEP), *([cache_v] * PAGES_PER_STEP))


def _pad_a_cols(x):
    i3 = 3 * A_WIDTH
    z = lambda n: jnp.zeros(x.shape[:-1] + (n,), x.dtype)
    return jnp.concatenate([
        x[..., :i3],
        x[..., i3:i3 + DECAY_RANK], z(LANES - DECAY_RANK),
        x[..., i3 + DECAY_RANK:i3 + DECAY_RANK + AAA_RANK], z(LANES - AAA_RANK),
        x[..., i3 + DECAY_RANK + AAA_RANK:], z(2 * LANES - GATE_RANK)], axis=-1)


def _unpad_a_cols(x):
    return jnp.concatenate([x[..., :A_OFF_W + DECAY_RANK], x[..., A_OFF_A:A_OFF_A + AAA_RANK],
                            x[..., A_OFF_G:A_OFF_G + GATE_RANK]], axis=-1)


def _pad_rows(w, n):
    return jnp.concatenate([w, jnp.zeros((n - w.shape[0],) + w.shape[1:], w.dtype)], axis=0)


def _ab_params(i, w_in_ab, mu_ab, w0, w2, a0, a2, g2, k_k, k_a, r_k, lnx_g, lnx_b, dw_w, dw_b, cn_g, cn_b):
    row = lambda x: x.reshape(1, -1)
    head = jnp.arange(A_WIDTH)[:, None] // A_HEAD == jnp.arange(LANES)[None, :]
    return {
        "w_a": _pad_a_cols(w_in_ab[i][:, :A_COLS]).astype(BF16),
        "w_b": w_in_ab[i][:, A_COLS:].astype(BF16),
        "mu": row(_pad_a_cols(mu_ab[i])),
        "w0": row(w0[i]), "a0": row(a0[i]), "k_k": row(k_k[i]), "k_a": row(k_a[i]), "r_k": row(r_k[i]),
        "lnx_g": row(lnx_g[i]), "lnx_b": row(lnx_b[i]),
        "w2": _pad_rows(w2[i], LANES).astype(BF16), "a2": _pad_rows(a2[i], LANES).astype(BF16),
        "g2": _pad_rows(g2[i], 2 * LANES).astype(BF16),
        "seg": head.astype(BF16), "segt": head.T.astype(BF16),
        "dw_w": dw_w[i], "dw_b": row(dw_b[i]), "cn_g": row(cn_g[i]), "cn_b": row(cn_b[i]),
    }


def _state_to_pairs(s):
    nb = s.shape[0]
    s = s.reshape(nb, A_HEADS // 2, 2, A_HEAD, A_HEAD)
    z = jnp.zeros_like(s[:, :, 0])
    top = jnp.concatenate([s[:, :, 0], z], axis=-1)
    bot = jnp.concatenate([z, s[:, :, 1]], axis=-1)
    return jnp.concatenate([top, bot], axis=-2)


def _pairs_to_state(s):
    nb = s.shape[0]
    return jnp.stack([s[:, :, :A_HEAD, :A_HEAD], s[:, :, A_HEAD:, A_HEAD:]], axis=2).reshape(
        nb, A_HEADS, A_HEAD, A_HEAD)


def _mixer_ab(x, g_pre, shift_prev, wkv0, conv_buf, prm, *, nb, t):
    if t >= WKV_CHUNK:
        t_pad, tt_a, tt_b, xp = t, min(t, 256), min(t, 128), x
    else:
        t_pad, tt_a, tt_b = WKV_CHUNK, WKV_CHUNK, WKV_CHUNK
        xp = jnp.pad(x.reshape(nb, t, -1), ((0, 0), (0, t_pad - t), (0, 0))).reshape(nb * t_pad, -1)
    za = _norm_matmul(xp, g_pre, prm["w_a"], F32, tn=512)
    zb = _norm_matmul(xp, g_pre, prm["w_b"], F32, tn=512)
    ya, shift, s_new = _wkv_mix(za, _pad_a_cols(shift_prev)[:, None, :], _state_to_pairs(wkv0), prm,
                                nb=nb, t_pad=t_pad, t_valid=t, tt=tt_a)
    hist = jnp.pad(conv_buf, ((0, 0), (CONV_HALO - (CONV_W - 1), 0), (0, 0)))
    yb, conv_new = _conv_module(zb, hist, prm, nb=nb, t_pad=t_pad, t_valid=t, tt=tt_b)
    y = jnp.concatenate([ya, yb], axis=-1)
    if t_pad != t:
        y = y.reshape(nb, t_pad, -1)[:, :t].reshape(nb * t, -1)
    return y, _unpad_a_cols(shift[:, 0]), _pairs_to_state(s_new), conv_new


def _rest_of_layer(x, l, k_arr, v_arr, k_col, v_col, wts, *, nb, t):
    (g_x_pre, g_x_post, w_xq, w_xo, g_ffn_pre, g_ffn_post, w_gu, w_down) = wts
    q = _norm_matmul(x, g_x_pre[l], w_xq, BF16, tn=X_W)
    rep = 1 if t >= 16 else 16
    q3 = q.reshape(nb, t, X_W)
    if rep > 1:
        q3 = jnp.repeat(q3, rep, axis=1)
    o = _cross_attention(q3, k_arr, v_arr, k_col, v_col, tq=min(t * rep, 512))
    o = o[:, ::rep].reshape(nb * t, X_W)
    x = _matmul_norm_res(o, w_xo, g_x_post[l], x)
    h = _norm_matmul(x, g_ffn_pre[l], w_gu, BF16, tn=512, swiglu_half=D_FF)
    return _matmul_norm_res(h, w_down, g_ffn_post[l], x)


def kernel(x_prompt, x_sample, cache_k, cache_v, state_wkv, state_shift, state_conv, cache_mem_k, cache_mem_v, page_table, mem_prompt, w_in_ab, w_out_ab, mu_ab, w0, w2, a0, a2, g2, k_k, k_a, r_k, lnx_g, lnx_b, dw_w, dw_b, cn_g, cn_b, w_in_c, w_out_c, g_mix_pre, g_mix_post, g_x_pre, g_x_post, g_mem, w_xq, w_xkv, w_xo, g_ffn_pre, g_ffn_post, w_ffn_gu, w_ffn_down):
    bp, tp, d = x_prompt.shape
    bs, ts, _ = x_sample.shape
    depth = g_mix_pre.shape[0]
    xp = x_prompt.reshape(bp * tp, d)
    xs = x_sample.reshape(bs * ts, d)
    mem = mem_prompt.reshape(bp * MEM_LEN, d)
    outs = {n: [] for n in ("kp", "vp", "wkvp", "shp", "cvp", "mkp", "mvp", "ks", "vs", "wkvs", "shs", "cvs")}
    for l in range(depth):
        i = l // 2
        if l % 2 == 0:
            prm = _ab_params(i, w_in_ab, mu_ab, w0, w2, a0, a2, g2, k_k, k_a, r_k, lnx_g, lnx_b,
                             dw_w, dw_b, cn_g, cn_b)
            w_out = w_out_ab[i].astype(BF16)
            fp, shp, wkvp, cvp = _mixer_ab(
                xp, g_mix_pre[l], jnp.zeros((bp, A_COLS), F32), jnp.zeros((bp, A_HEADS, A_HEAD, A_HEAD), F32),
                jnp.zeros((bp, CONV_W - 1, B_WIDTH), F32), prm, nb=bp, t=tp)
            fs, shs, wkvs, cvs = _mixer_ab(xs, g_mix_pre[l], state_shift[i], state_wkv[i], state_conv[i], prm,
                                           nb=bs, t=ts)
            for n, val in (("shp", shp), ("wkvp", wkvp), ("cvp", cvp), ("shs", shs), ("wkvs", wkvs), ("cvs", cvs)):
                outs[n].append(val)
        else:
            w_in = w_in_c[i].astype(BF16)
            w_out = w_out_c[i].astype(BF16)
            zp = _norm_matmul(xp, g_mix_pre[l], w_in, F32, tn=512)
            fp = _moba_prompt(zp, nb=bp, t=tp)
            zs = _norm_matmul(xs, g_mix_pre[l], w_in, F32, tn=512)
            n_phys = cache_k.shape[1]
            fs = _moba_decode(zs.reshape(bs, ts, -1), cache_k.reshape(-1, PAGE_SIZE, C_KV_W),
                              cache_v.reshape(-1, PAGE_SIZE, C_KV_W), page_table + i * n_phys).reshape(bs * ts, C_Q_W)
            kv = lambda z, nb, t, off: z[:, off:off + C_KV_W].reshape(nb, t, C_KV_HEADS, C_HEAD_DIM)
            outs["kp"].append(kv(zp, bp, tp, C_Q_W))
            outs["vp"].append(kv(zp, bp, tp, C_Q_W + C_KV_W))
            outs["ks"].append(kv(zs, bs, ts, C_Q_W))
            outs["vs"].append(kv(zs, bs, ts, C_Q_W + C_KV_W))
        xp = _matmul_norm_res(fp, w_out, g_mix_post[l], xp)
        xs = _matmul_norm_res(fs, w_out, g_mix_post[l], xs)
        mkv = _norm_matmul(mem, g_mem[l], w_xkv[l].astype(BF16), F32, tn=512)
        outs["mkp"].append(mkv[:, :X_W].reshape(bp, MEM_LEN, X_HEADS, X_HEAD_DIM))
        outs["mvp"].append(mkv[:, X_W:].reshape(bp, MEM_LEN, X_HEADS, X_HEAD_DIM))
        wts = (g_x_pre, g_x_post, w_xq[l].astype(BF16), w_xo[l].astype(BF16), g_ffn_pre, g_ffn_post,
               w_ffn_gu[l].astype(BF16), w_ffn_down[l].astype(BF16))
        xp = _rest_of_layer(xp, l, mkv, mkv, 0, 1, wts, nb=bp, t=tp)
        xs = _rest_of_layer(xs, l, cache_mem_k[l].reshape(bs * MEM_LEN, X_W),
                            cache_mem_v[l].reshape(bs * MEM_LEN, X_W), 0, 0, wts, nb=bs, t=ts)
    st = lambda n: jnp.stack(outs[n])
    return (xp.reshape(bp, tp, d), xs.reshape(bs, ts, d), st("kp"), st("vp"), st("wkvp"), st("shp"), st("cvp"),
            st("mkp"), st("mvp"), st("ks"), st("vs"), st("wkvs"), st("shs"), st("cvs"))
```

```python
import functools
import math

import jax
import jax.numpy as jnp
from jax import lax
from jax.experimental import pallas as pl
from jax.experimental.pallas import tpu as pltpu

F32 = jnp.float32
BF16 = jnp.bfloat16

D_MODEL = 2048
PAGE_SIZE = 128
A_WIDTH = 1024
A_HEAD = 64
A_HEADS = 16
DECAY_RANK = 64
AAA_RANK = 64
GATE_RANK = 160
A_COLS = 3 * A_WIDTH + DECAY_RANK + AAA_RANK + GATE_RANK
GN_EPS = 64e-5
B_WIDTH = 1024
CONV_W = 31
C_HEADS = 16
C_HEAD_DIM = 128
C_KV_HEADS = 4
C_GROUP = 4
C_Q_W = 2048
C_KV_W = 512
MOBA_BLOCK = 256
MOBA_TOPK = 3
Q_BLOCK = MOBA_BLOCK
MEM_LEN = 256
X_HEADS = 4
X_HEAD_DIM = 128
X_W = 512
D_FF = 5632
RMS_EPS = 1e-6
LN_EPS = 1e-5
NEG = -1e30

SUBLANES = 8
LANES = 128
A_OFF_W = 3 * A_WIDTH
A_OFF_A = A_OFF_W + LANES
A_OFF_G = A_OFF_A + LANES
A_PAD = A_OFF_G + 2 * LANES
WKV_CHUNK = 64
VMEM_LIMIT = 56 * 1024 * 1024


def _cparams(sem):
    return pltpu.CompilerParams(dimension_semantics=sem, vmem_limit_bytes=VMEM_LIMIT)


def _dot(a, b):
    return jnp.dot(a, b, preferred_element_type=F32)


def _dot_nt(a, b):
    return lax.dot_general(a, b, (((1,), (1,)), ((), ())), preferred_element_type=F32)


def _dot_tn(a, b):
    return lax.dot_general(a, b, (((0,), (0,)), ((), ())), preferred_element_type=F32)


def _split2(x):
    hi = x.astype(BF16)
    lo = (x - hi.astype(F32)).astype(BF16)
    return hi, lo


def _dot_x3(dot, a, b):
    ah, al = _split2(a)
    bh, bl = _split2(b)
    return dot(ah, bh) + (dot(ah, bl) + dot(al, bh))


def _dot_exact_rhs(a, b_bf16):
    hi, lo = _split2(a)
    return _dot(hi, b_bf16) + _dot(lo, b_bf16)


def _dot_exact_lhs(a_bf16, b):
    hi, lo = _split2(b)
    return _dot(a_bf16, hi) + _dot(a_bf16, lo)


def _sigmoid(x):
    return 1.0 / (1.0 + jnp.exp(-x))


def _norm_mm_kernel(x_ref, g_ref, *refs, n_w):
    w_refs, o_ref, xn_ref = refs[:n_w], refs[n_w], refs[n_w + 1]

    @pl.when(pl.program_id(1) == 0)
    def _():
        x = x_ref[...]
        ms = jnp.mean(x * x, axis=-1, keepdims=True)
        xn_ref[...] = (x * lax.rsqrt(ms + RMS_EPS) * g_ref[...]).astype(BF16)

    xn = xn_ref[...]
    if n_w == 1:
        y = _dot(xn, w_refs[0][...])
    else:
        gate = _dot(xn, w_refs[0][...])
        y = gate * _sigmoid(gate) * _dot(xn, w_refs[1][...])
    o_ref[...] = y.astype(o_ref.dtype)


def _norm_matmul(x, g, w, out_dtype, *, tn, swiglu_half=None):
    m, d = x.shape
    n = swiglu_half if swiglu_half else w.shape[1]
    tm = min(m, 1024)
    assert m % tm == 0 and n % tn == 0
    w_specs = [pl.BlockSpec((d, tn), lambda i, j: (0, j))]
    operands = [x, g.reshape(1, d), w]
    if swiglu_half:
        off = swiglu_half // tn
        w_specs.append(pl.BlockSpec((d, tn), lambda i, j: (0, j + off)))
        operands.append(w)
    return pl.pallas_call(
        functools.partial(_norm_mm_kernel, n_w=len(w_specs)),
        out_shape=jax.ShapeDtypeStruct((m, n), out_dtype),
        grid=(m // tm, n // tn),
        in_specs=[pl.BlockSpec((tm, d), lambda i, j: (i, 0)),
                  pl.BlockSpec((1, d), lambda i, j: (0, 0))] + w_specs,
        out_specs=pl.BlockSpec((tm, tn), lambda i, j: (i, j)),
        scratch_shapes=[pltpu.VMEM((tm, d), BF16)],
        compiler_params=_cparams(("parallel", "arbitrary")),
        name="norm_matmul",
    )(*operands)


def _mm_norm_res_kernel(x_ref, w_ref, g_ref, r_ref, o_ref):
    y = _dot(x_ref[...], w_ref[...])
    ms = jnp.mean(y * y, axis=-1, keepdims=True)
    o_ref[...] = r_ref[...] + y * lax.rsqrt(ms + RMS_EPS) * g_ref[...]


def _matmul_norm_res(x, w, g, res):
    m, kd = x.shape
    n = w.shape[1]
    tm = min(m, 512 if kd * n * 2 <= 8 * 1024 * 1024 else 256)
    assert m % tm == 0
    return pl.pallas_call(
        _mm_norm_res_kernel,
        out_shape=jax.ShapeDtypeStruct((m, n), F32),
        grid=(m // tm,),
        in_specs=[pl.BlockSpec((tm, kd), lambda i: (i, 0)),
                  pl.BlockSpec((kd, n), lambda i: (0, 0), pipeline_mode=pl.Buffered(1)),
                  pl.BlockSpec((1, n), lambda i: (0, 0)),
                  pl.BlockSpec((tm, n), lambda i: (i, 0))],
        out_specs=pl.BlockSpec((tm, n), lambda i: (i, 0)),
        compiler_params=_cparams(("parallel",)),
        name="matmul_norm_res",
    )(x, w, g.reshape(1, n), res)


def _xattn_kernel(q_ref, k_ref, v_ref, o_ref):
    scale = 1.0 / math.sqrt(X_HEAD_DIM)
    for h in range(X_HEADS):
        sl = slice(h * X_HEAD_DIM, (h + 1) * X_HEAD_DIM)
        s = _dot_nt(q_ref[:, sl], k_ref[:, sl].astype(BF16)) * scale
        p = jnp.exp(s - jnp.max(s, axis=-1, keepdims=True))
        l = jnp.sum(p, axis=-1, keepdims=True)
        o = _dot(p.astype(BF16), v_ref[:, sl].astype(BF16))
        o_ref[:, sl] = (o / l).astype(o_ref.dtype)


def _cross_attention(q, k_arr, v_arr, k_col, v_col, *, tq):
    b, t, _ = q.shape
    assert t % tq == 0
    return pl.pallas_call(
        _xattn_kernel,
        out_shape=jax.ShapeDtypeStruct((b, t, X_W), BF16),
        grid=(b, t // tq),
        in_specs=[pl.BlockSpec((None, tq, X_W), lambda i, j: (i, j, 0)),
                  pl.BlockSpec((MEM_LEN, X_W), lambda i, j: (i, k_col)),
                  pl.BlockSpec((MEM_LEN, X_W), lambda i, j: (i, v_col))],
        out_specs=pl.BlockSpec((None, tq, X_W), lambda i, j: (i, j, 0)),
        compiler_params=_cparams(("parallel", "arbitrary")),
        name="cross_attention",
    )(q, k_arr, v_arr)


def _wkv_tile_kernel(za_ref, sp_ref, s0_ref, mu_ref, w0_ref, a0_ref, kk_ref, ka_ref, rk_ref,
                     lng_ref, lnb_ref, w2_ref, a2_ref, g2_ref, seg_ref, segt_ref,
                     ya_ref, shift_ref, s_ref,
                     rh_ref, kap_ref, kh_ref, bh_ref, v_ref, khp_ref, bhp_ref, pt_ref, y_ref,
                     *, tt, t_valid):
    c = WKV_CHUNK
    ti = pl.program_id(1)

    @pl.when(ti == 0)
    def _():
        s_ref[...] = s0_ref[...]
        shift_ref[...] = sp_ref[...]

    za = za_ref[...]
    row = lax.broadcasted_iota(jnp.int32, (tt, 1), 0)
    prev = jnp.where(row == 0, shift_ref[...], pltpu.roll(za, 1, axis=0))
    last = (t_valid - 1) % tt
    shift_ref[...] = za[last:last + 1, :]
    mix = za + (prev - za) * mu_ref[...]
    valid = (ti * tt + row) < t_valid

    r = mix[:, 0:A_WIDTH]
    k = mix[:, A_WIDTH:2 * A_WIDTH]
    v = mix[:, 2 * A_WIDTH:3 * A_WIDTH]
    wl = mix[:, A_OFF_W:A_OFF_A]
    al = mix[:, A_OFF_A:A_OFF_G]
    gl = mix[:, A_OFF_G:A_PAD]
    xw = w0_ref[...] + _dot(jnp.tanh(wl).astype(BF16), w2_ref[...])
    lw = jnp.where(valid, -_sigmoid(xw) * math.exp(-0.5), 0.0)
    a = _sigmoid(a0_ref[...] + _dot(al.astype(BF16), a2_ref[...]))
    g = _dot(_sigmoid(gl).astype(BF16), g2_ref[...])

    seg = seg_ref[...]
    segt = segt_ref[...]

    def head_sum(x):
        return _dot_exact_rhs(_dot_exact_rhs(x, seg), segt)

    kk = k * kk_ref[...]
    kk = kk * lax.rsqrt(jnp.maximum(head_sum(kk * kk), 1e-24))
    kk = jnp.where(valid, kk, 0.0)
    kt = jnp.where(valid, k * (1.0 + (a - 1.0) * ka_ref[...]), 0.0)
    b = kk * a
    v = jnp.where(valid, v, 0.0)

    ri = lax.broadcasted_iota(jnp.int32, (tt, tt), 0)
    ci = lax.broadcasted_iota(jnp.int32, (tt, tt), 1)
    same = (ri // c) == (ci // c)
    cum = _dot_exact_lhs(jnp.where(same & (ci <= ri), 1.0, 0.0).astype(BF16), lw)
    tot = _dot_exact_lhs(jnp.where(same, 1.0, 0.0).astype(BF16), lw)
    p_inv = jnp.exp(-cum)
    p_end = jnp.exp(tot - cum)
    rh_ref[...] = r * jnp.exp(cum)
    kap_ref[...] = kk * jnp.exp(cum - lw)
    kh_ref[...] = kt * p_inv
    bh_ref[...] = b * p_inv
    v_ref[...] = v
    khp_ref[...] = kt * p_end
    bhp_ref[...] = b * p_end
    pt_ref[...] = jnp.exp(tot)

    lane = lax.broadcasted_iota(jnp.int32, (1, LANES), 1)
    m0 = (lane < A_HEAD).astype(F32)
    m1 = 1.0 - m0
    r2 = lax.broadcasted_iota(jnp.int32, (2 * c, 2 * c), 0)
    c2 = lax.broadcasted_iota(jnp.int32, (2 * c, 2 * c), 1)
    strict = (r2 % c) > (c2 % c)
    incl = (r2 % c) >= (c2 % c)
    c2x = 2 * c
    bf = lambda x: x.astype(BF16)

    def stack(x):
        return jnp.concatenate([x * m0, x * m1], axis=0)

    def chunk(ci_, carry):
        rows = pl.ds(pl.multiple_of(ci_ * c, c), c)
        pairs = range(A_HEADS // 2)
        cols = [slice(p * LANES, (p + 1) * LANES) for p in pairs]
        vst = [stack(v_ref[rows, cols[p]]) for p in pairs]
        s0 = [s_ref[p] for p in pairs]
        lhs = [bf(jnp.concatenate([stack(kap_ref[rows, cols[p]]), stack(rh_ref[rows, cols[p]])], axis=0))
               for p in pairs]
        gram = [_dot_nt(lhs[p], bf(jnp.concatenate([stack(kh_ref[rows, cols[p]]), stack(bh_ref[rows, cols[p]])],
                                                   axis=0))) for p in pairs]
        from_s0 = [_dot_nt(lhs[p], bf(s0[p])) for p in pairs]
        x = [from_s0[p][:c2x] + _dot(bf(jnp.where(strict, gram[p][:c2x, :c2x], 0.0)), bf(vst[p])) for p in pairs]
        lp = [jnp.where(strict, gram[p][:c2x, c2x:], 0.0) for p in pairs]
        sq = [_dot(bf(lp[p]), bf(jnp.concatenate([lp[p], x[p]], axis=1))) for p in pairs]
        lp = [sq[p][:, :c2x] for p in pairs]
        x = [x[p] - sq[p][:, c2x:] for p in pairs]
        n = 2
        while 2 * n < c:
            sq = [_dot(bf(lp[p]), bf(jnp.concatenate([lp[p], x[p]], axis=1))) for p in pairs]
            lp = [sq[p][:, :c2x] for p in pairs]
            x = [x[p] + sq[p][:, c2x:] for p in pairs]
            n *= 2
        x = [x[p] + _dot(bf(lp[p]), bf(x[p])) for p in pairs]
        for p in pairs:
            tri = jnp.concatenate([jnp.where(incl, gram[p][c2x:, :c2x], 0.0),
                                   jnp.where(incl, -gram[p][c2x:, c2x:], 0.0)], axis=1)
            y = from_s0[p][c2x:] + _dot(bf(tri), bf(jnp.concatenate([vst[p], x[p]], axis=0)))
            y_ref[rows, cols[p]] = y[:c] + y[c:]
            decayed = jnp.concatenate([stack(khp_ref[rows, cols[p]]), stack(bhp_ref[rows, cols[p]])], axis=0)
            s_ref[p] = (s0[p] * pt_ref[pl.ds(pl.multiple_of(ci_ * c, c), 1), cols[p]]
                        + _dot_tn(bf(jnp.concatenate([vst[p], -x[p]], axis=0)), bf(decayed)))
        return carry

    lax.fori_loop(0, tt // c, chunk, 0)

    y = y_ref[...]
    mean = head_sum(y) * (1.0 / A_HEAD)
    yc = y - mean
    var = head_sum(yc * yc) * (1.0 / A_HEAD)
    yn = yc * lax.rsqrt(var + GN_EPS) * lng_ref[...] + lnb_ref[...]
    yn = yn + head_sum(r * kt * rk_ref[...]) * v
    ya_ref[...] = (yn * g).astype(ya_ref.dtype)


def _wkv_mix(za, shift_prev, s0, prm, *, nb, t_pad, t_valid, tt):
    nt = t_pad // tt
    vec = lambda n: pl.BlockSpec((1, n), lambda i, j: (0, 0))
    full = lambda a: pl.BlockSpec(a.shape, lambda i, j: (0,) * a.ndim)
    state_spec = pl.BlockSpec((None, A_HEADS // 2, LANES, LANES), lambda i, j: (i, 0, 0, 0))
    shift_spec = pl.BlockSpec((None, 1, A_PAD), lambda i, j: (i, 0, 0))
    vecs = [prm[n] for n in ("mu", "w0", "a0", "k_k", "k_a", "r_k", "lnx_g", "lnx_b")]
    mats = [prm[n] for n in ("w2", "a2", "g2", "seg", "segt")]
    return pl.pallas_call(
        functools.partial(_wkv_tile_kernel, tt=tt, t_valid=t_valid),
        out_shape=(jax.ShapeDtypeStruct((nb * t_pad, A_WIDTH), BF16),
                   jax.ShapeDtypeStruct((nb, 1, A_PAD), F32),
                   jax.ShapeDtypeStruct((nb, A_HEADS // 2, LANES, LANES), F32)),
        grid=(nb, nt),
        in_specs=[pl.BlockSpec((tt, A_PAD), lambda i, j: (i * nt + j, 0)), shift_spec, state_spec]
                 + [vec(x.shape[1]) for x in vecs] + [full(x) for x in mats],
        out_specs=(pl.BlockSpec((tt, A_WIDTH), lambda i, j: (i * nt + j, 0)), shift_spec, state_spec),
        scratch_shapes=[pltpu.VMEM((tt, A_WIDTH), F32)] * 9,
        compiler_params=_cparams(("parallel", "arbitrary")),
        name="wkv_mix",
    )(za, shift_prev, s0, *vecs, *mats)


CONV_HALO = 32


def _conv_kernel(zb_ref, cb_ref, w_ref, b_ref, g_ref, be_ref, o_ref, st_ref, ext_ref, conv_ref,
                 *, tt, t_last):
    ti = pl.program_id(1)
    pad = CONV_HALO - (CONV_W - 1)

    @pl.when(ti == 0)
    def _():
        ext_ref[0:CONV_HALO, :] = cb_ref[...]

    zb = zb_ref[...]
    ext_ref[CONV_HALO:CONV_HALO + tt, :] = zb[:, :B_WIDTH] * _sigmoid(zb[:, B_WIDTH:])
    rb = min(tt, 64)
    for r0 in range(0, tt, rb):
        for c0 in range(0, B_WIDTH, LANES):
            cols = slice(c0, c0 + LANES)
            acc = jnp.broadcast_to(b_ref[:, cols], (rb, LANES))
            for j in range(CONV_W):
                acc = acc + ext_ref[r0 + pad + j:r0 + pad + j + rb, cols] * w_ref[j:j + 1, cols]
            conv_ref[r0:r0 + rb, cols] = acc
    conv = conv_ref[...]
    mean = jnp.mean(conv, axis=-1, keepdims=True)
    xc = conv - mean
    var = jnp.mean(xc * xc, axis=-1, keepdims=True)
    y = xc * lax.rsqrt(var + LN_EPS) * g_ref[...] + be_ref[...]
    o_ref[...] = (y * _sigmoid(y)).astype(o_ref.dtype)
    st_ref[...] = ext_ref[t_last + pad:t_last + pad + CONV_W - 1, :]
    if tt >= CONV_HALO:
        ext_ref[0:CONV_HALO, :] = ext_ref[tt:tt + CONV_HALO, :]


def _conv_module(zb, conv_buf, prm, *, nb, t_pad, t_valid, tt):
    nt = t_pad // tt
    assert nt == 1 or tt >= CONV_HALO
    vec = pl.BlockSpec((1, B_WIDTH), lambda i, j: (0, 0))
    return pl.pallas_call(
        functools.partial(_conv_kernel, tt=tt, t_last=t_valid - (nt - 1) * tt),
        out_shape=(jax.ShapeDtypeStruct((nb * t_pad, B_WIDTH), BF16),
                   jax.ShapeDtypeStruct((nb, CONV_W - 1, B_WIDTH), F32)),
        grid=(nb, nt),
        in_specs=[pl.BlockSpec((tt, 2 * B_WIDTH), lambda i, j: (i * nt + j, 0)),
                  pl.BlockSpec((None, CONV_HALO, B_WIDTH), lambda i, j: (i, 0, 0)),
                  pl.BlockSpec((CONV_W, B_WIDTH), lambda i, j: (0, 0)), vec, vec, vec],
        out_specs=(pl.BlockSpec((tt, B_WIDTH), lambda i, j: (i * nt + j, 0)),
                   pl.BlockSpec((None, CONV_W - 1, B_WIDTH), lambda i, j: (i, 0, 0))),
        scratch_shapes=[pltpu.VMEM((CONV_HALO + tt, B_WIDTH), F32), pltpu.VMEM((tt, B_WIDTH), F32)],
        compiler_params=_cparams(("parallel", "arbitrary")),
        name="conv_module",
    )(zb, conv_buf, prm["dw_w"], prm["dw_b"], prm["cn_g"], prm["cn_b"])


def _top_blocks(gate, blk, n_sel):
    sel = []
    for _ in range(n_sel):
        m = jnp.max(gate, axis=0, keepdims=True)
        idx = jnp.min(jnp.where(gate == m, blk, gate.shape[0]), axis=0, keepdims=True)
        sel.append(idx)
        gate = jnp.where(blk == idx, -jnp.inf, gate)
    return sel


def _moba_prompt_kernel(q_ref, k_ref, v_ref, o_ref, kb_ref, vt_ref, km_ref, *, t):
    kvh = pl.program_id(1)
    qi = pl.program_id(2)
    nb = t // MOBA_BLOCK
    nq = C_GROUP * Q_BLOCK

    @pl.when(qi == 0)
    def _():
        km_ref[...] = jnp.zeros_like(km_ref)
        for n in range(nb):
            kblk = k_ref[n * MOBA_BLOCK:(n + 1) * MOBA_BLOCK, :]
            kb_ref[n] = kblk.astype(BF16)
            vt_ref[n] = v_ref[n * MOBA_BLOCK:(n + 1) * MOBA_BLOCK, :].T.astype(BF16)
            km_ref[n:n + 1, :] = jnp.mean(kblk, axis=0, keepdims=True)

    q = q_ref[...]
    qs = jnp.concatenate([q[:, g * C_HEAD_DIM:(g + 1) * C_HEAD_DIM] for g in range(C_GROUP)], axis=0)
    own = (qi * Q_BLOCK) // MOBA_BLOCK
    blk = lax.broadcasted_iota(jnp.int32, (km_ref.shape[0], 1), 0)
    gate = _dot_x3(_dot_nt, km_ref[...], qs)
    gate = jnp.where(blk < own, gate, jnp.where(blk < nb, NEG, -jnp.inf))
    sel = _top_blocks(gate, blk, MOBA_TOPK)
    sel = [jnp.where(own > r, sel[r], -1) for r in range(MOBA_TOPK)]

    col = lax.broadcasted_iota(jnp.int32, (1, nq), 1)
    head = (kvh * C_GROUP + col // Q_BLOCK + 1).astype(F32)
    slope = jnp.exp(head * (-(8.0 / C_HEADS) * math.log(2.0)))
    pos = qi * Q_BLOCK + col % Q_BLOCK
    trow = lax.broadcasted_iota(jnp.int32, (MOBA_BLOCK, 1), 0)
    tab = trow.astype(F32) * slope
    qb = (qs * (1.0 / math.sqrt(C_HEAD_DIM))).astype(BF16)

    def attend(blocks, carry):
        m, l, acc = carry
        m_new = m
        for _, s in blocks:
            m_new = jnp.maximum(m_new, jnp.max(s, axis=0, keepdims=True))
        alpha = jnp.exp(m - m_new)
        l = alpha * l
        acc = alpha * acc
        for n, s in blocks:
            p = jnp.exp(s - m_new)
            l = l + jnp.sum(p, axis=0, keepdims=True)
            acc = acc + _dot(vt_ref[n], p.astype(BF16))
        return m_new, l, acc

    def past_scores(n):
        picked = (sel[0] == n) | (sel[1] == n) | (sel[2] == n)
        cvec = jnp.where(picked, slope * (n * MOBA_BLOCK - pos).astype(F32), NEG)
        return _dot_nt(kb_ref[n], qb) + tab + cvec

    def past_pair(i, carry):
        return attend([(2 * i, past_scores(2 * i)), (2 * i + 1, past_scores(2 * i + 1))], carry)

    init = (jnp.full((1, nq), NEG, F32), jnp.zeros((1, nq), F32), jnp.zeros((C_HEAD_DIM, nq), F32))
    carry = lax.fori_loop(0, (own + 1) // 2, past_pair, init)
    s = _dot_nt(kb_ref[own], qb) + tab + slope * (own * MOBA_BLOCK - pos).astype(F32)
    s = jnp.where(trow <= pos - own * MOBA_BLOCK, s, NEG)
    _, l, acc = attend([(own, s)], carry)
    o = acc / l
    for g in range(C_GROUP):
        cols = slice(g * Q_BLOCK, (g + 1) * Q_BLOCK)
        o_ref[:, g * C_HEAD_DIM:(g + 1) * C_HEAD_DIM] = o[:, cols].T.astype(o_ref.dtype)


def _moba_prompt(z, *, nb, t):
    nq = t // Q_BLOCK
    kcol = C_Q_W // C_HEAD_DIM
    vcol = (C_Q_W + C_KV_W) // C_HEAD_DIM
    return pl.pallas_call(
        functools.partial(_moba_prompt_kernel, t=t),
        out_shape=jax.ShapeDtypeStruct((nb * t, C_Q_W), BF16),
        grid=(nb, C_KV_HEADS, nq),
        in_specs=[pl.BlockSpec((Q_BLOCK, C_GROUP * C_HEAD_DIM), lambda b, h, i: (b * nq + i, h)),
                  pl.BlockSpec((t, C_HEAD_DIM), lambda b, h, i: (b, kcol + h)),
                  pl.BlockSpec((t, C_HEAD_DIM), lambda b, h, i: (b, vcol + h))],
        out_specs=pl.BlockSpec((Q_BLOCK, C_GROUP * C_HEAD_DIM), lambda b, h, i: (b * nq + i, h)),
        scratch_shapes=[pltpu.VMEM((t // MOBA_BLOCK, MOBA_BLOCK, C_HEAD_DIM), BF16),
                        pltpu.VMEM((t // MOBA_BLOCK, C_HEAD_DIM, MOBA_BLOCK), BF16),
                        pltpu.VMEM((-(-t // MOBA_BLOCK // 8) * 8, C_HEAD_DIM), F32)],
        compiler_params=_cparams(("parallel", "parallel", "arbitrary")),
        name="moba_prompt",
    )(z, z, z)


PAGES_PER_STEP = 2 * MOBA_BLOCK // PAGE_SIZE


def _moba_decode_kernel(pt_ref, z_ref, *refs, n_blocks):
    del pt_ref
    npg = PAGES_PER_STEP
    k_refs, v_refs = refs[:npg], refs[npg:2 * npg]
    o_ref = refs[2 * npg]
    m_ref, l_ref, acc_ref, gate_ref = refs[2 * npg + 1:]
    j = pl.program_id(1)
    ppb = MOBA_BLOCK // PAGE_SIZE
    past = n_blocks * MOBA_BLOCK
    scale = 1.0 / math.sqrt(C_HEAD_DIM)
    hrow = lax.broadcasted_iota(jnp.int32, (C_HEADS, 1), 0)
    slope = jnp.exp((hrow + 1).astype(F32) * (-(8.0 / C_HEADS) * math.log(2.0)))

    z = z_ref[...]
    qrows = jnp.concatenate([z[:, h * C_HEAD_DIM:(h + 1) * C_HEAD_DIM] for h in range(C_HEADS)], axis=0)
    qb = qrows.astype(BF16)
    rows_blk = MOBA_BLOCK * C_KV_HEADS
    col = lax.broadcasted_iota(jnp.int32, (1, rows_blk), 1)
    mine = (col % C_KV_HEADS) == (hrow // C_GROUP)
    tab = jnp.where(mine, slope * (col // C_KV_HEADS).astype(F32), NEG)

    for blk in range(npg // ppb):
        n = j * (npg // ppb) + blk
        kblk = jnp.concatenate([k_refs[blk * ppb + i][...] for i in range(ppb)], axis=0)
        vblk = jnp.concatenate([v_refs[blk * ppb + i][...] for i in range(ppb)], axis=0)
        s = _dot_nt(qb, kblk.astype(BF16)) * scale + tab - slope * (past - n * MOBA_BLOCK).astype(F32)
        m = jnp.max(s, axis=-1, keepdims=True)
        p = jnp.exp(s - m)
        m_ref[pl.ds(n, 1)] = m[None]
        l_ref[pl.ds(n, 1)] = jnp.sum(p, axis=-1, keepdims=True)[None]
        acc_ref[pl.ds(n, 1)] = _dot(p.astype(BF16), vblk.astype(BF16))[None]
        ksum = jnp.sum(kblk.reshape(rows_blk // SUBLANES, SUBLANES, C_HEAD_DIM), axis=0)
        kmean = (ksum[:C_KV_HEADS] + ksum[C_KV_HEADS:]) * (1.0 / MOBA_BLOCK)
        kmrows = jnp.concatenate([jnp.broadcast_to(kmean[kvh:kvh + 1], (C_GROUP, C_HEAD_DIM))
                                  for kvh in range(C_KV_HEADS)], axis=0)
        gate_ref[pl.ds(n, 1)] = jnp.sum(qrows * kmrows, axis=-1, keepdims=True)[None]

    @pl.when(j == pl.num_programs(1) - 1)
    def _():
        gates = gate_ref[...]
        ms = m_ref[...]
        bidx = lax.broadcasted_iota(jnp.int32, gates.shape, 0)
        picked = jnp.zeros(gates.shape, jnp.bool_)
        for _ in range(min(MOBA_TOPK, n_blocks)):
            gmax = jnp.max(gates, axis=0, keepdims=True)
            first = jnp.min(jnp.where(gates == gmax, bidx, n_blocks), axis=0, keepdims=True)
            hit = bidx == first
            picked = picked | hit
            gates = jnp.where(hit, -jnp.inf, gates)
        knew = jnp.concatenate(
            [z[:, C_Q_W + kvh * C_HEAD_DIM:C_Q_W + (kvh + 1) * C_HEAD_DIM] for kvh in range(C_KV_HEADS)
             for _ in range(C_GROUP)], axis=0)
        vnew = jnp.concatenate(
            [z[:, C_Q_W + C_KV_W + kvh * C_HEAD_DIM:C_Q_W + C_KV_W + (kvh + 1) * C_HEAD_DIM]
             for kvh in range(C_KV_HEADS) for _ in range(C_GROUP)], axis=0)
        s_own = jnp.sum(qb.astype(F32) * knew.astype(BF16).astype(F32), axis=-1, keepdims=True) * scale
        mtot = jnp.maximum(jnp.max(jnp.where(picked, ms, NEG), axis=0), s_own)
        wgt = jnp.where(picked, jnp.exp(ms - mtot[None]), 0.0)
        p_own = jnp.exp(s_own - mtot)
        l = jnp.sum(wgt * l_ref[...], axis=0) + p_own
        acc = jnp.sum(wgt * acc_ref[...], axis=0) + p_own.astype(BF16).astype(F32) * vnew.astype(BF16).astype(F32)
        o_ref[...] = (acc / l).astype(o_ref.dtype)


def _moba_decode(z, cache_k, cache_v, page_table):
    nb, n_pages = page_table.shape
    assert n_pages % PAGES_PER_STEP == 0 and 2 * C_KV_HEADS == SUBLANES
    n_blocks = n_pages * PAGE_SIZE // MOBA_BLOCK
    steps = n_pages // PAGES_PER_STEP

    def page_spec(i):
        return pl.BlockSpec((None, PAGE_SIZE * C_KV_HEADS, C_HEAD_DIM),
                            lambda b, j, pt: (pt[b, j * PAGES_PER_STEP + i], 0, 0))

    pages = [page_spec(i) for i in range(PAGES_PER_STEP)]
    zw = z.shape[-1]
    return pl.pallas_call(
        functools.partial(_moba_decode_kernel, n_blocks=n_blocks),
        out_shape=jax.ShapeDtypeStruct((nb, C_HEADS, C_HEAD_DIM), BF16),
        grid_spec=pltpu.PrefetchScalarGridSpec(
            num_scalar_prefetch=1,
            grid=(nb, steps),
            in_specs=[pl.BlockSpec((None, 1, zw), lambda b, j, pt: (b, 0, 0))] + pages + pages,
            out_specs=pl.BlockSpec((None, C_HEADS, C_HEAD_DIM), lambda b, j, pt: (b, 0, 0)),
            scratch_shapes=[pltpu.VMEM((n_blocks, C_HEADS, 1), F32), pltpu.VMEM((n_blocks, C_HEADS, 1), F32),
                            pltpu.VMEM((n_blocks, C_HEADS, C_HEAD_DIM), F32),
                            pltpu.VMEM((n_blocks, C_HEADS, 1), F32)]),
        compiler_params=_cparams(("parallel", "arbitrary")),
        name="moba_decode",
    )(page_table, z, *([cache_k] * PAGES_PER_STEP), *([cache_v] * PAGES_PER_STEP))


def _pad_a_cols(x):
    i3 = 3 * A_WIDTH
    z = lambda n: jnp.zeros(x.shape[:-1] + (n,), x.dtype)
    return jnp.concatenate([
        x[..., :i3],
        x[..., i3:i3 + DECAY_RANK], z(LANES - DECAY_RANK),
        x[..., i3 + DECAY_RANK:i3 + DECAY_RANK + AAA_RANK], z(LANES - AAA_RANK),
        x[..., i3 + DECAY_RANK + AAA_RANK:], z(2 * LANES - GATE_RANK)], axis=-1)


def _unpad_a_cols(x):
    return jnp.concatenate([x[..., :A_OFF_W + DECAY_RANK], x[..., A_OFF_A:A_OFF_A + AAA_RANK],
                            x[..., A_OFF_G:A_OFF_G + GATE_RANK]], axis=-1)


def _pad_rows(w, n):
    return jnp.concatenate([w, jnp.zeros((n - w.shape[0],) + w.shape[1:], w.dtype)], axis=0)


def _ab_params(i, w_in_ab, mu_ab, w0, w2, a0, a2, g2, k_k, k_a, r_k, lnx_g, lnx_b, dw_w, dw_b, cn_g, cn_b):
    row = lambda x: x.reshape(1, -1)
    head = jnp.arange(A_WIDTH)[:, None] // A_HEAD == jnp.arange(LANES)[None, :]
    return {
        "w_a": _pad_a_cols(w_in_ab[i][:, :A_COLS]).astype(BF16),
        "w_b": w_in_ab[i][:, A_COLS:].astype(BF16),
        "mu": row(_pad_a_cols(mu_ab[i])),
        "w0": row(w0[i]), "a0": row(a0[i]), "k_k": row(k_k[i]), "k_a": row(k_a[i]), "r_k": row(r_k[i]),
        "lnx_g": row(lnx_g[i]), "lnx_b": row(lnx_b[i]),
        "w2": _pad_rows(w2[i], LANES).astype(BF16), "a2": _pad_rows(a2[i], LANES).astype(BF16),
        "g2": _pad_rows(g2[i], 2 * LANES).astype(BF16),
        "seg": head.astype(BF16), "segt": head.T.astype(BF16),
        "dw_w": dw_w[i], "dw_b": row(dw_b[i]), "cn_g": row(cn_g[i]), "cn_b": row(cn_b[i]),
    }


def _state_to_pairs(s):
    nb = s.shape[0]
    s = s.reshape(nb, A_HEADS // 2, 2, A_HEAD, A_HEAD)
    z = jnp.zeros_like(s[:, :, 0])
    top = jnp.concatenate([s[:, :, 0], z], axis=-1)
    bot = jnp.concatenate([z, s[:, :, 1]], axis=-1)
    return jnp.concatenate([top, bot], axis=-2)


def _pairs_to_state(s):
    nb = s.shape[0]
    return jnp.stack([s[:, :, :A_HEAD, :A_HEAD], s[:, :, A_HEAD:, A_HEAD:]], axis=2).reshape(
        nb, A_HEADS, A_HEAD, A_HEAD)


def _mixer_ab(x, g_pre, shift_prev, wkv0, conv_buf, prm, *, nb, t):
    if t >= WKV_CHUNK:
        t_pad, tt_a, tt_b, xp = t, min(t, 256), min(t, 128), x
    else:
        t_pad, tt_a, tt_b = WKV_CHUNK, WKV_CHUNK, WKV_CHUNK
        xp = jnp.pad(x.reshape(nb, t, -1), ((0, 0), (0, t_pad - t), (0, 0))).reshape(nb * t_pad, -1)
    za = _norm_matmul(xp, g_pre, prm["w_a"], F32, tn=512)
    zb = _norm_matmul(xp, g_pre, prm["w_b"], F32, tn=512)
    ya, shift, s_new = _wkv_mix(za, _pad_a_cols(shift_prev)[:, None, :], _state_to_pairs(wkv0), prm,
                                nb=nb, t_pad=t_pad, t_valid=t, tt=tt_a)
    hist = jnp.pad(conv_buf, ((0, 0), (CONV_HALO - (CONV_W - 1), 0), (0, 0)))
    yb, conv_new = _conv_module(zb, hist, prm, nb=nb, t_pad=t_pad, t_valid=t, tt=tt_b)
    y = jnp.concatenate([ya, yb], axis=-1)
    if t_pad != t:
        y = y.reshape(nb, t_pad, -1)[:, :t].reshape(nb * t, -1)
    return y, _unpad_a_cols(shift[:, 0]), _pairs_to_state(s_new), conv_new


def _rest_of_layer(x, l, k_arr, v_arr, k_col, v_col, wts, *, nb, t):
    (g_x_pre, g_x_post, w_xq, w_xo, g_ffn_pre, g_ffn_post, w_gu, w_down) = wts
    q = _norm_matmul(x, g_x_pre[l], w_xq, BF16, tn=X_W)
    rep = 1 if t >= 16 else 16
    q3 = q.reshape(nb, t, X_W)
    if rep > 1:
        q3 = jnp.repeat(q3, rep, axis=1)
    o = _cross_attention(q3, k_arr, v_arr, k_col, v_col, tq=min(t * rep, 512))
    o = o[:, ::rep].reshape(nb * t, X_W)
    x = _matmul_norm_res(o, w_xo, g_x_post[l], x)
    h = _norm_matmul(x, g_ffn_pre[l], w_gu, BF16, tn=512, swiglu_half=D_FF)
    return _matmul_norm_res(h, w_down, g_ffn_post[l], x)


def kernel(x_prompt, x_sample, cache_k, cache_v, state_wkv, state_shift, state_conv, cache_mem_k, cache_mem_v, page_table, mem_prompt, w_in_ab, w_out_ab, mu_ab, w0, w2, a0, a2, g2, k_k, k_a, r_k, lnx_g, lnx_b, dw_w, dw_b, cn_g, cn_b, w_in_c, w_out_c, g_mix_pre, g_mix_post, g_x_pre, g_x_post, g_mem, w_xq, w_xkv, w_xo, g_ffn_pre, g_ffn_post, w_ffn_gu, w_ffn_down):
    bp, tp, d = x_prompt.shape
    bs, ts, _ = x_sample.shape
    depth = g_mix_pre.shape[0]
    xp = x_prompt.reshape(bp * tp, d)
    xs = x_sample.reshape(bs * ts, d)
    mem = mem_prompt.reshape(bp * MEM_LEN, d)
    outs = {n: [] for n in ("kp", "vp", "wkvp", "shp", "cvp", "mkp", "mvp", "ks", "vs", "wkvs", "shs", "cvs")}
    for l in range(depth):
        i = l // 2
        if l % 2 == 0:
            prm = _ab_params(i, w_in_ab, mu_ab, w0, w2, a0, a2, g2, k_k, k_a, r_k, lnx_g, lnx_b,
                             dw_w, dw_b, cn_g, cn_b)
            w_out = w_out_ab[i].astype(BF16)
            fp, shp, wkvp, cvp = _mixer_ab(
                xp, g_mix_pre[l], jnp.zeros((bp, A_COLS), F32), jnp.zeros((bp, A_HEADS, A_HEAD, A_HEAD), F32),
                jnp.zeros((bp, CONV_W - 1, B_WIDTH), F32), prm, nb=bp, t=tp)
            fs, shs, wkvs, cvs = _mixer_ab(xs, g_mix_pre[l], state_shift[i], state_wkv[i], state_conv[i], prm,
                                           nb=bs, t=ts)
            for n, val in (("shp", shp), ("wkvp", wkvp), ("cvp", cvp), ("shs", shs), ("wkvs", wkvs), ("cvs", cvs)):
                outs[n].append(val)
        else:
            w_in = w_in_c[i].astype(BF16)
            w_out = w_out_c[i].astype(BF16)
            zp = _norm_matmul(xp, g_mix_pre[l], w_in, F32, tn=512)
            fp = _moba_prompt(zp, nb=bp, t=tp)
            zs = _norm_matmul(xs, g_mix_pre[l], w_in, F32, tn=512)
            n_phys = cache_k.shape[1]
            page_rows = (-1, PAGE_SIZE * C_KV_HEADS, C_HEAD_DIM)
            fs = _moba_decode(zs.reshape(bs, ts, -1), cache_k.reshape(page_rows), cache_v.reshape(page_rows),
                              page_table + i * n_phys).reshape(bs * ts, C_Q_W)
            kv = lambda z, nb, t, off: z[:, off:off + C_KV_W].reshape(nb, t, C_KV_HEADS, C_HEAD_DIM)
            outs["kp"].append(kv(zp, bp, tp, C_Q_W))
            outs["vp"].append(kv(zp, bp, tp, C_Q_W + C_KV_W))
            outs["ks"].append(kv(zs, bs, ts, C_Q_W))
            outs["vs"].append(kv(zs, bs, ts, C_Q_W + C_KV_W))
        xp = _matmul_norm_res(fp, w_out, g_mix_post[l], xp)
        xs = _matmul_norm_res(fs, w_out, g_mix_post[l], xs)
        mkv = _norm_matmul(mem, g_mem[l], w_xkv[l].astype(BF16), F32, tn=512)
        outs["mkp"].append(mkv[:, :X_W].reshape(bp, MEM_LEN, X_HEADS, X_HEAD_DIM))
        outs["mvp"].append(mkv[:, X_W:].reshape(bp, MEM_LEN, X_HEADS, X_HEAD_DIM))
        wts = (g_x_pre, g_x_post, w_xq[l].astype(BF16), w_xo[l].astype(BF16), g_ffn_pre, g_ffn_post,
               w_ffn_gu[l].astype(BF16), w_ffn_down[l].astype(BF16))
        xp = _rest_of_layer(xp, l, mkv, mkv, 0, 1, wts, nb=bp, t=tp)
        xs = _rest_of_layer(xs, l, cache_mem_k[l].reshape(bs * MEM_LEN, X_W),
                            cache_mem_v[l].reshape(bs * MEM_LEN, X_W), 0, 0, wts, nb=bs, t=ts)
    st = lambda n: jnp.stack(outs[n])
    return (xp.reshape(bp, tp, d), xs.reshape(bs, ts, d), st("kp"), st("vp"), st("wkvp"), st("shp"), st("cvp"),
            st("mkp"), st("mvp"), st("ks"), st("vs"), st("wkvs"), st("shs"), st("cvs"))
```

```python
import functools
import math

import jax
import jax.numpy as jnp
from jax import lax
from jax.experimental import pallas as pl
from jax.experimental.pallas import tpu as pltpu

F32 = jnp.float32
BF16 = jnp.bfloat16

D_MODEL = 2048
PAGE_SIZE = 128
A_WIDTH = 1024
A_HEAD = 64
A_HEADS = 16
DECAY_RANK = 64
AAA_RANK = 64
GATE_RANK = 160
A_COLS = 3 * A_WIDTH + DECAY_RANK + AAA_RANK + GATE_RANK
GN_EPS = 64e-5
B_WIDTH = 1024
CONV_W = 31
C_HEADS = 16
C_HEAD_DIM = 128
C_KV_HEADS = 4
C_GROUP = 4
C_Q_W = 2048
C_KV_W = 512
MOBA_BLOCK = 256
MOBA_TOPK = 3
Q_BLOCK = MOBA_BLOCK
MEM_LEN = 256
X_HEADS = 4
X_HEAD_DIM = 128
X_W = 512
D_FF = 5632
RMS_EPS = 1e-6
LN_EPS = 1e-5
NEG = -1e30

SUBLANES = 8
LANES = 128
A_OFF_W = 3 * A_WIDTH
A_OFF_A = A_OFF_W + LANES
A_OFF_G = A_OFF_A + LANES
A_PAD = A_OFF_G + 2 * LANES
WKV_CHUNK = 64
VMEM_LIMIT = 56 * 1024 * 1024


def _cparams(sem):
    return pltpu.CompilerParams(dimension_semantics=sem, vmem_limit_bytes=VMEM_LIMIT)


def _dot(a, b):
    return jnp.dot(a, b, preferred_element_type=F32)


def _dot_nt(a, b):
    return lax.dot_general(a, b, (((1,), (1,)), ((), ())), preferred_element_type=F32)


def _dot_tn(a, b):
    return lax.dot_general(a, b, (((0,), (0,)), ((), ())), preferred_element_type=F32)


def _split2(x):
    hi = x.astype(BF16)
    lo = (x - hi.astype(F32)).astype(BF16)
    return hi, lo


def _dot_x3(dot, a, b):
    ah, al = _split2(a)
    bh, bl = _split2(b)
    return dot(ah, bh) + (dot(ah, bl) + dot(al, bh))


def _dot_exact_rhs(a, b_bf16):
    hi, lo = _split2(a)
    return _dot(hi, b_bf16) + _dot(lo, b_bf16)


def _dot_exact_lhs(a_bf16, b):
    hi, lo = _split2(b)
    return _dot(a_bf16, hi) + _dot(a_bf16, lo)


def _sigmoid(x):
    return 1.0 / (1.0 + jnp.exp(-x))


def _norm_mm_kernel(x_ref, g_ref, *refs, n_w):
    w_refs, o_ref, xn_ref = refs[:n_w], refs[n_w], refs[n_w + 1]

    @pl.when(pl.program_id(1) == 0)
    def _():
        x = x_ref[...]
        ms = jnp.mean(x * x, axis=-1, keepdims=True)
        xn_ref[...] = (x * lax.rsqrt(ms + RMS_EPS) * g_ref[...]).astype(BF16)

    xn = xn_ref[...]
    if n_w == 1:
        y = _dot(xn, w_refs[0][...])
    else:
        gate = _dot(xn, w_refs[0][...])
        y = gate * _sigmoid(gate) * _dot(xn, w_refs[1][...])
    o_ref[...] = y.astype(o_ref.dtype)


def _norm_matmul(x, g, w, out_dtype, *, tn, swiglu_half=None):
    m, d = x.shape
    n = swiglu_half if swiglu_half else w.shape[1]
    tm = min(m, 1024)
    assert m % tm == 0 and n % tn == 0
    w_specs = [pl.BlockSpec((d, tn), lambda i, j: (0, j))]
    operands = [x, g.reshape(1, d), w]
    if swiglu_half:
        off = swiglu_half // tn
        w_specs.append(pl.BlockSpec((d, tn), lambda i, j: (0, j + off)))
        operands.append(w)
    return pl.pallas_call(
        functools.partial(_norm_mm_kernel, n_w=len(w_specs)),
        out_shape=jax.ShapeDtypeStruct((m, n), out_dtype),
        grid=(m // tm, n // tn),
        in_specs=[pl.BlockSpec((tm, d), lambda i, j: (i, 0)),
                  pl.BlockSpec((1, d), lambda i, j: (0, 0))] + w_specs,
        out_specs=pl.BlockSpec((tm, tn), lambda i, j: (i, j)),
        scratch_shapes=[pltpu.VMEM((tm, d), BF16)],
        compiler_params=_cparams(("parallel", "arbitrary")),
        name="norm_matmul",
    )(*operands)


def _mm_norm_res_kernel(x_ref, w_ref, g_ref, r_ref, o_ref):
    y = _dot(x_ref[...], w_ref[...])
    ms = jnp.mean(y * y, axis=-1, keepdims=True)
    o_ref[...] = r_ref[...] + y * lax.rsqrt(ms + RMS_EPS) * g_ref[...]


def _matmul_norm_res(x, w, g, res):
    m, kd = x.shape
    n = w.shape[1]
    tm = min(m, 512 if kd * n * 2 <= 8 * 1024 * 1024 else 256)
    assert m % tm == 0
    return pl.pallas_call(
        _mm_norm_res_kernel,
        out_shape=jax.ShapeDtypeStruct((m, n), F32),
        grid=(m // tm,),
        in_specs=[pl.BlockSpec((tm, kd), lambda i: (i, 0)),
                  pl.BlockSpec((kd, n), lambda i: (0, 0), pipeline_mode=pl.Buffered(1)),
                  pl.BlockSpec((1, n), lambda i: (0, 0)),
                  pl.BlockSpec((tm, n), lambda i: (i, 0))],
        out_specs=pl.BlockSpec((tm, n), lambda i: (i, 0)),
        compiler_params=_cparams(("parallel",)),
        name="matmul_norm_res",
    )(x, w, g.reshape(1, n), res)


def _xattn_kernel(q_ref, k_ref, v_ref, o_ref):
    scale = 1.0 / math.sqrt(X_HEAD_DIM)
    for h in range(X_HEADS):
        sl = slice(h * X_HEAD_DIM, (h + 1) * X_HEAD_DIM)
        s = _dot_nt(q_ref[:, sl], k_ref[:, sl].astype(BF16)) * scale
        p = jnp.exp(s - jnp.max(s, axis=-1, keepdims=True))
        l = jnp.sum(p, axis=-1, keepdims=True)
        o = _dot(p.astype(BF16), v_ref[:, sl].astype(BF16))
        o_ref[:, sl] = (o / l).astype(o_ref.dtype)


def _cross_attention(q, k_arr, v_arr, k_col, v_col, *, tq):
    b, t, _ = q.shape
    assert t % tq == 0
    return pl.pallas_call(
        _xattn_kernel,
        out_shape=jax.ShapeDtypeStruct((b, t, X_W), BF16),
        grid=(b, t // tq),
        in_specs=[pl.BlockSpec((None, tq, X_W), lambda i, j: (i, j, 0)),
                  pl.BlockSpec((MEM_LEN, X_W), lambda i, j: (i, k_col)),
                  pl.BlockSpec((MEM_LEN, X_W), lambda i, j: (i, v_col))],
        out_specs=pl.BlockSpec((None, tq, X_W), lambda i, j: (i, j, 0)),
        compiler_params=_cparams(("parallel", "arbitrary")),
        name="cross_attention",
    )(q, k_arr, v_arr)


def _wkv_tile_kernel(za_ref, sp_ref, s0_ref, mu_ref, w0_ref, a0_ref, kk_ref, ka_ref, rk_ref,
                     lng_ref, lnb_ref, w2_ref, a2_ref, g2_ref, seg_ref, segt_ref,
                     ya_ref, shift_ref, s_ref,
                     rh_ref, kap_ref, kh_ref, bh_ref, v_ref, khp_ref, bhp_ref, pt_ref, y_ref,
                     *, tt, t_valid):
    c = WKV_CHUNK
    ti = pl.program_id(1)

    @pl.when(ti == 0)
    def _():
        s_ref[...] = s0_ref[...]
        shift_ref[...] = sp_ref[...]

    za = za_ref[...]
    row = lax.broadcasted_iota(jnp.int32, (tt, 1), 0)
    prev = jnp.where(row == 0, shift_ref[...], pltpu.roll(za, 1, axis=0))
    last = (t_valid - 1) % tt
    shift_ref[...] = za[last:last + 1, :]
    mix = za + (prev - za) * mu_ref[...]
    valid = (ti * tt + row) < t_valid

    r = mix[:, 0:A_WIDTH]
    k = mix[:, A_WIDTH:2 * A_WIDTH]
    v = mix[:, 2 * A_WIDTH:3 * A_WIDTH]
    wl = mix[:, A_OFF_W:A_OFF_A]
    al = mix[:, A_OFF_A:A_OFF_G]
    gl = mix[:, A_OFF_G:A_PAD]
    xw = w0_ref[...] + _dot(jnp.tanh(wl).astype(BF16), w2_ref[...])
    lw = jnp.where(valid, -_sigmoid(xw) * math.exp(-0.5), 0.0)
    a = _sigmoid(a0_ref[...] + _dot(al.astype(BF16), a2_ref[...]))
    g = _dot(_sigmoid(gl).astype(BF16), g2_ref[...])

    seg = seg_ref[...]
    segt = segt_ref[...]

    def head_sum(x):
        return _dot_exact_rhs(_dot_exact_rhs(x, seg), segt)

    kk = k * kk_ref[...]
    kk = kk * lax.rsqrt(jnp.maximum(head_sum(kk * kk), 1e-24))
    kk = jnp.where(valid, kk, 0.0)
    kt = jnp.where(valid, k * (1.0 + (a - 1.0) * ka_ref[...]), 0.0)
    b = kk * a
    v = jnp.where(valid, v, 0.0)

    ri = lax.broadcasted_iota(jnp.int32, (tt, tt), 0)
    ci = lax.broadcasted_iota(jnp.int32, (tt, tt), 1)
    same = (ri // c) == (ci // c)
    cum = _dot_exact_lhs(jnp.where(same & (ci <= ri), 1.0, 0.0).astype(BF16), lw)
    tot = _dot_exact_lhs(jnp.where(same, 1.0, 0.0).astype(BF16), lw)
    p_inv = jnp.exp(-cum)
    p_end = jnp.exp(tot - cum)
    rh_ref[...] = r * jnp.exp(cum)
    kap_ref[...] = kk * jnp.exp(cum - lw)
    kh_ref[...] = kt * p_inv
    bh_ref[...] = b * p_inv
    v_ref[...] = v
    khp_ref[...] = kt * p_end
    bhp_ref[...] = b * p_end
    pt_ref[...] = jnp.exp(tot)

    lane = lax.broadcasted_iota(jnp.int32, (1, LANES), 1)
    m0 = (lane < A_HEAD).astype(F32)
    m1 = 1.0 - m0
    r2 = lax.broadcasted_iota(jnp.int32, (2 * c, 2 * c), 0)
    c2 = lax.broadcasted_iota(jnp.int32, (2 * c, 2 * c), 1)
    strict = (r2 % c) > (c2 % c)
    incl = (r2 % c) >= (c2 % c)
    c2x = 2 * c
    bf = lambda x: x.astype(BF16)

    def stack(x):
        return jnp.concatenate([x * m0, x * m1], axis=0)

    def chunk(ci_, carry):
        rows = pl.ds(pl.multiple_of(ci_ * c, c), c)
        pairs = range(A_HEADS // 2)
        cols = [slice(p * LANES, (p + 1) * LANES) for p in pairs]
        vst = [stack(v_ref[rows, cols[p]]) for p in pairs]
        s0 = [s_ref[p] for p in pairs]
        lhs = [bf(jnp.concatenate([stack(kap_ref[rows, cols[p]]), stack(rh_ref[rows, cols[p]])], axis=0))
               for p in pairs]
        gram = [_dot_nt(lhs[p], bf(jnp.concatenate([stack(kh_ref[rows, cols[p]]), stack(bh_ref[rows, cols[p]])],
                                                   axis=0))) for p in pairs]
        from_s0 = [_dot_nt(lhs[p], bf(s0[p])) for p in pairs]
        x = [from_s0[p][:c2x] + _dot(bf(jnp.where(strict, gram[p][:c2x, :c2x], 0.0)), bf(vst[p])) for p in pairs]
        lp = [jnp.where(strict, gram[p][:c2x, c2x:], 0.0) for p in pairs]
        sq = [_dot(bf(lp[p]), bf(jnp.concatenate([lp[p], x[p]], axis=1))) for p in pairs]
        lp = [sq[p][:, :c2x] for p in pairs]
        x = [x[p] - sq[p][:, c2x:] for p in pairs]
        n = 2
        while 2 * n < c:
            sq = [_dot(bf(lp[p]), bf(jnp.concatenate([lp[p], x[p]], axis=1))) for p in pairs]
            lp = [sq[p][:, :c2x] for p in pairs]
            x = [x[p] + sq[p][:, c2x:] for p in pairs]
            n *= 2
        x = [x[p] + _dot(bf(lp[p]), bf(x[p])) for p in pairs]
        for p in pairs:
            tri = jnp.concatenate([jnp.where(incl, gram[p][c2x:, :c2x], 0.0),
                                   jnp.where(incl, -gram[p][c2x:, c2x:], 0.0)], axis=1)
            y = from_s0[p][c2x:] + _dot(bf(tri), bf(jnp.concatenate([vst[p], x[p]], axis=0)))
            y_ref[rows, cols[p]] = y[:c] + y[c:]
            decayed = jnp.concatenate([stack(khp_ref[rows, cols[p]]), stack(bhp_ref[rows, cols[p]])], axis=0)
            s_ref[p] = (s0[p] * pt_ref[pl.ds(pl.multiple_of(ci_ * c, c), 1), cols[p]]
                        + _dot_tn(bf(jnp.concatenate([vst[p], -x[p]], axis=0)), bf(decayed)))
        return carry

    lax.fori_loop(0, tt // c, chunk, 0)

    y = y_ref[...]
    mean = head_sum(y) * (1.0 / A_HEAD)
    yc = y - mean
    var = head_sum(yc * yc) * (1.0 / A_HEAD)
    yn = yc * lax.rsqrt(var + GN_EPS) * lng_ref[...] + lnb_ref[...]
    yn = yn + head_sum(r * kt * rk_ref[...]) * v
    ya_ref[...] = (yn * g).astype(ya_ref.dtype)


def _wkv_mix(za, shift_prev, s0, prm, *, nb, t_pad, t_valid, tt):
    nt = t_pad // tt
    vec = lambda n: pl.BlockSpec((1, n), lambda i, j: (0, 0))
    full = lambda a: pl.BlockSpec(a.shape, lambda i, j: (0,) * a.ndim)
    state_spec = pl.BlockSpec((None, A_HEADS // 2, LANES, LANES), lambda i, j: (i, 0, 0, 0))
    shift_spec = pl.BlockSpec((None, 1, A_PAD), lambda i, j: (i, 0, 0))
    vecs = [prm[n] for n in ("mu", "w0", "a0", "k_k", "k_a", "r_k", "lnx_g", "lnx_b")]
    mats = [prm[n] for n in ("w2", "a2", "g2", "seg", "segt")]
    return pl.pallas_call(
        functools.partial(_wkv_tile_kernel, tt=tt, t_valid=t_valid),
        out_shape=(jax.ShapeDtypeStruct((nb * t_pad, A_WIDTH), BF16),
                   jax.ShapeDtypeStruct((nb, 1, A_PAD), F32),
                   jax.ShapeDtypeStruct((nb, A_HEADS // 2, LANES, LANES), F32)),
        grid=(nb, nt),
        in_specs=[pl.BlockSpec((tt, A_PAD), lambda i, j: (i * nt + j, 0)), shift_spec, state_spec]
                 + [vec(x.shape[1]) for x in vecs] + [full(x) for x in mats],
        out_specs=(pl.BlockSpec((tt, A_WIDTH), lambda i, j: (i * nt + j, 0)), shift_spec, state_spec),
        scratch_shapes=[pltpu.VMEM((tt, A_WIDTH), F32)] * 9,
        compiler_params=_cparams(("parallel", "arbitrary")),
        name="wkv_mix",
    )(za, shift_prev, s0, *vecs, *mats)


CONV_HALO = 32


def _conv_kernel(zb_ref, cb_ref, w_ref, b_ref, g_ref, be_ref, o_ref, st_ref, ext_ref, conv_ref,
                 *, tt, t_last):
    ti = pl.program_id(1)
    pad = CONV_HALO - (CONV_W - 1)

    @pl.when(ti == 0)
    def _():
        ext_ref[0:CONV_HALO, :] = cb_ref[...]

    zb = zb_ref[...]
    ext_ref[CONV_HALO:CONV_HALO + tt, :] = zb[:, :B_WIDTH] * _sigmoid(zb[:, B_WIDTH:])
    rb = min(tt, 64)
    for r0 in range(0, tt, rb):
        for c0 in range(0, B_WIDTH, LANES):
            cols = slice(c0, c0 + LANES)
            x = ext_ref[r0:r0 + rb + CONV_HALO, cols]
            n = rb + CONV_HALO
            shifted = [x] + [pltpu.roll(x, n - s, axis=0) for s in range(1, SUBLANES)]
            acc = jnp.broadcast_to(b_ref[:, cols], (rb, LANES))
            for j in range(CONV_W):
                s = (pad + j) % SUBLANES
                base = pad + j - s
                acc = acc + shifted[s][base:base + rb] * w_ref[j:j + 1, cols]
            conv_ref[r0:r0 + rb, cols] = acc
    conv = conv_ref[...]
    mean = jnp.mean(conv, axis=-1, keepdims=True)
    xc = conv - mean
    var = jnp.mean(xc * xc, axis=-1, keepdims=True)
    y = xc * lax.rsqrt(var + LN_EPS) * g_ref[...] + be_ref[...]
    o_ref[...] = (y * _sigmoid(y)).astype(o_ref.dtype)
    st_ref[...] = ext_ref[t_last + pad:t_last + pad + CONV_W - 1, :]
    if tt >= CONV_HALO:
        ext_ref[0:CONV_HALO, :] = ext_ref[tt:tt + CONV_HALO, :]


def _conv_module(zb, conv_buf, prm, *, nb, t_pad, t_valid, tt):
    nt = t_pad // tt
    assert nt == 1 or tt >= CONV_HALO
    vec = pl.BlockSpec((1, B_WIDTH), lambda i, j: (0, 0))
    return pl.pallas_call(
        functools.partial(_conv_kernel, tt=tt, t_last=t_valid - (nt - 1) * tt),
        out_shape=(jax.ShapeDtypeStruct((nb * t_pad, B_WIDTH), BF16),
                   jax.ShapeDtypeStruct((nb, CONV_W - 1, B_WIDTH), F32)),
        grid=(nb, nt),
        in_specs=[pl.BlockSpec((tt, 2 * B_WIDTH), lambda i, j: (i * nt + j, 0)),
                  pl.BlockSpec((None, CONV_HALO, B_WIDTH), lambda i, j: (i, 0, 0)),
                  pl.BlockSpec((CONV_W, B_WIDTH), lambda i, j: (0, 0)), vec, vec, vec],
        out_specs=(pl.BlockSpec((tt, B_WIDTH), lambda i, j: (i * nt + j, 0)),
                   pl.BlockSpec((None, CONV_W - 1, B_WIDTH), lambda i, j: (i, 0, 0))),
        scratch_shapes=[pltpu.VMEM((CONV_HALO + tt, B_WIDTH), F32), pltpu.VMEM((tt, B_WIDTH), F32)],
        compiler_params=_cparams(("parallel", "arbitrary")),
        name="conv_module",
    )(zb, conv_buf, prm["dw_w"], prm["dw_b"], prm["cn_g"], prm["cn_b"])


def _top_blocks(gate, blk, n_sel):
    sel = []
    for _ in range(n_sel):
        m = jnp.max(gate, axis=0, keepdims=True)
        idx = jnp.min(jnp.where(gate == m, blk, gate.shape[0]), axis=0, keepdims=True)
        sel.append(idx)
        gate = jnp.where(blk == idx, -jnp.inf, gate)
    return sel


def _moba_prompt_kernel(q_ref, k_ref, v_ref, o_ref, kb_ref, vt_ref, km_ref, qf_ref, *, t):
    kvh = pl.program_id(1)
    qi = pl.program_id(2)
    nb = t // MOBA_BLOCK
    nq = C_GROUP * Q_BLOCK

    log2e = 1.0 / math.log(2.0)
    n_feat = 3

    def slope2_of(group):
        head = (kvh * C_GROUP + group + 1).astype(F32)
        return jnp.exp(head * (-(8.0 / C_HEADS) * math.log(2.0))) * log2e

    @pl.when(qi == 0)
    def _():
        srow = slope2_of(lax.broadcasted_iota(jnp.int32, (nq, 1), 0) // Q_BLOCK)
        s_hi = srow.astype(BF16).astype(F32)
        s_mid = (srow - s_hi).astype(BF16).astype(F32)
        s_lo = (srow - s_hi - s_mid).astype(BF16).astype(F32)
        part = lax.broadcasted_iota(jnp.int32, (1, LANES), 1)
        qfeat = jnp.where(part % n_feat == 0, s_hi, jnp.where(part % n_feat == 1, s_mid, s_lo))
        qf_ref[...] = jnp.where(part < 2 * n_feat, qfeat, 0.0).astype(BF16)
        km_ref[...] = jnp.zeros_like(km_ref)
        lane = lax.broadcasted_iota(jnp.int32, (MOBA_BLOCK, LANES), 1)
        offset = lax.broadcasted_iota(jnp.int32, (MOBA_BLOCK, LANES), 0).astype(F32)
        for n in range(nb):
            kblk = k_ref[n * MOBA_BLOCK:(n + 1) * MOBA_BLOCK, :]
            feat = jnp.where(lane < n_feat, float(n * MOBA_BLOCK), jnp.where(lane < 2 * n_feat, offset, 0.0))
            kb_ref[n] = jnp.concatenate([kblk, feat], axis=1).astype(BF16)
            vt_ref[n] = v_ref[n * MOBA_BLOCK:(n + 1) * MOBA_BLOCK, :].T.astype(BF16)
            km_ref[n:n + 1, :] = jnp.mean(kblk, axis=0, keepdims=True)

    q = q_ref[...]
    qs = jnp.concatenate([q[:, g * C_HEAD_DIM:(g + 1) * C_HEAD_DIM] for g in range(C_GROUP)], axis=0)
    own = (qi * Q_BLOCK) // MOBA_BLOCK
    blk = lax.broadcasted_iota(jnp.int32, (km_ref.shape[0], 1), 0)
    gate = _dot_x3(_dot_nt, km_ref[...], qs)
    gate = jnp.where(blk < own, gate, jnp.where(blk < nb, NEG, -jnp.inf))
    sel = _top_blocks(gate, blk, MOBA_TOPK)
    sel = [jnp.where(own > r, sel[r], -1) for r in range(MOBA_TOPK)]

    col = lax.broadcasted_iota(jnp.int32, (1, nq), 1)
    pos = qi * Q_BLOCK + col % Q_BLOCK
    base = -slope2_of(col // Q_BLOCK) * pos.astype(F32)
    qb = jnp.concatenate([(qs * (log2e / math.sqrt(C_HEAD_DIM))).astype(BF16), qf_ref[...]], axis=1)
    trow = lax.broadcasted_iota(jnp.int32, (MOBA_BLOCK, 1), 0)

    def attend_upto(own_static):
        scores = []
        for n in range(own_static):
            picked = (sel[0] == n) | (sel[1] == n) | (sel[2] == n)
            scores.append(_dot_nt(kb_ref[n], qb) + jnp.where(picked, base, NEG))
        causal = trow <= col % Q_BLOCK
        scores.append(jnp.where(causal, _dot_nt(kb_ref[own_static], qb) + base, NEG))
        m = scores[-1].max(axis=0, keepdims=True)
        for s in scores[:-1]:
            m = jnp.maximum(m, jnp.max(s, axis=0, keepdims=True))
        l = jnp.zeros((1, nq), F32)
        acc = jnp.zeros((C_HEAD_DIM, nq), F32)
        for n, s in enumerate(scores):
            p = jnp.exp2(s - m)
            l = l + jnp.sum(p, axis=0, keepdims=True)
            acc = acc + _dot(vt_ref[n], p.astype(BF16))
        o = acc / l
        for g in range(C_GROUP):
            cols = slice(g * Q_BLOCK, (g + 1) * Q_BLOCK)
            o_ref[:, g * C_HEAD_DIM:(g + 1) * C_HEAD_DIM] = o[:, cols].T.astype(o_ref.dtype)

    for own_static in range(nb):
        pl.when(own == own_static)(functools.partial(attend_upto, own_static))


def _moba_prompt(z, *, nb, t):
    nq = t // Q_BLOCK
    assert t % Q_BLOCK == 0 and t // MOBA_BLOCK <= 256 and MOBA_BLOCK <= 256
    kcol = C_Q_W // C_HEAD_DIM
    vcol = (C_Q_W + C_KV_W) // C_HEAD_DIM
    return pl.pallas_call(
        functools.partial(_moba_prompt_kernel, t=t),
        out_shape=jax.ShapeDtypeStruct((nb * t, C_Q_W), BF16),
        grid=(nb, C_KV_HEADS, nq),
        in_specs=[pl.BlockSpec((Q_BLOCK, C_GROUP * C_HEAD_DIM), lambda b, h, i: (b * nq + i, h)),
                  pl.BlockSpec((t, C_HEAD_DIM), lambda b, h, i: (b, kcol + h)),
                  pl.BlockSpec((t, C_HEAD_DIM), lambda b, h, i: (b, vcol + h))],
        out_specs=pl.BlockSpec((Q_BLOCK, C_GROUP * C_HEAD_DIM), lambda b, h, i: (b * nq + i, h)),
        scratch_shapes=[pltpu.VMEM((t // MOBA_BLOCK, MOBA_BLOCK, C_HEAD_DIM + LANES), BF16),
                        pltpu.VMEM((t // MOBA_BLOCK, C_HEAD_DIM, MOBA_BLOCK), BF16),
                        pltpu.VMEM((-(-t // MOBA_BLOCK // SUBLANES) * SUBLANES, C_HEAD_DIM), F32),
                        pltpu.VMEM((C_GROUP * Q_BLOCK, LANES), BF16)],
        compiler_params=_cparams(("parallel", "parallel", "arbitrary")),
        name="moba_prompt",
    )(z, z, z)


PAGES_PER_STEP = 4 * MOBA_BLOCK // PAGE_SIZE


def _moba_decode_kernel(pt_ref, z_ref, *refs, n_blocks):
    del pt_ref
    npg = PAGES_PER_STEP
    k_refs, v_refs = refs[:npg], refs[npg:2 * npg]
    o_ref = refs[2 * npg]
    m_ref, l_ref, acc_ref, gate_ref = refs[2 * npg + 1:]
    j = pl.program_id(1)
    ppb = MOBA_BLOCK // PAGE_SIZE
    past = n_blocks * MOBA_BLOCK
    scale = 1.0 / math.sqrt(C_HEAD_DIM)
    hrow = lax.broadcasted_iota(jnp.int32, (C_HEADS, 1), 0)
    slope = jnp.exp((hrow + 1).astype(F32) * (-(8.0 / C_HEADS) * math.log(2.0)))

    z = z_ref[...]
    qrows = jnp.concatenate([z[:, h * C_HEAD_DIM:(h + 1) * C_HEAD_DIM] for h in range(C_HEADS)], axis=0)
    qb = qrows.astype(BF16)
    rows_blk = MOBA_BLOCK * C_KV_HEADS
    col = lax.broadcasted_iota(jnp.int32, (1, rows_blk), 1)
    mine = (col % C_KV_HEADS) == (hrow // C_GROUP)
    tab = jnp.where(mine, slope * (col // C_KV_HEADS).astype(F32), NEG)

    for blk in range(npg // ppb):
        n = j * (npg // ppb) + blk
        kblk = jnp.concatenate([k_refs[blk * ppb + i][...] for i in range(ppb)], axis=0)
        vblk = jnp.concatenate([v_refs[blk * ppb + i][...] for i in range(ppb)], axis=0)
        s = _dot_nt(qb, kblk.astype(BF16)) * scale + tab - slope * (past - n * MOBA_BLOCK).astype(F32)
        m = jnp.max(s, axis=-1, keepdims=True)
        p = jnp.exp(s - m)
        m_ref[pl.ds(n, 1)] = m[None]
        l_ref[pl.ds(n, 1)] = jnp.sum(p, axis=-1, keepdims=True)[None]
        acc_ref[pl.ds(n, 1)] = _dot(p.astype(BF16), vblk.astype(BF16))[None]
        ksum = jnp.sum(kblk.reshape(rows_blk // SUBLANES, SUBLANES, C_HEAD_DIM), axis=0)
        kmean = (ksum[:C_KV_HEADS] + ksum[C_KV_HEADS:]) * (1.0 / MOBA_BLOCK)
        kmrows = jnp.concatenate([jnp.broadcast_to(kmean[kvh:kvh + 1], (C_GROUP, C_HEAD_DIM))
                                  for kvh in range(C_KV_HEADS)], axis=0)
        gate_ref[pl.ds(n, 1)] = jnp.sum(qrows * kmrows, axis=-1, keepdims=True)[None]

    @pl.when(j == pl.num_programs(1) - 1)
    def _():
        gates = gate_ref[...]
        ms = m_ref[...]
        bidx = lax.broadcasted_iota(jnp.int32, gates.shape, 0)
        picked = jnp.zeros(gates.shape, jnp.bool_)
        for _ in range(min(MOBA_TOPK, n_blocks)):
            gmax = jnp.max(gates, axis=0, keepdims=True)
            first = jnp.min(jnp.where(gates == gmax, bidx, n_blocks), axis=0, keepdims=True)
            hit = bidx == first
            picked = picked | hit
            gates = jnp.where(hit, -jnp.inf, gates)
        knew = jnp.concatenate(
            [z[:, C_Q_W + kvh * C_HEAD_DIM:C_Q_W + (kvh + 1) * C_HEAD_DIM] for kvh in range(C_KV_HEADS)
             for _ in range(C_GROUP)], axis=0)
        vnew = jnp.concatenate(
            [z[:, C_Q_W + C_KV_W + kvh * C_HEAD_DIM:C_Q_W + C_KV_W + (kvh + 1) * C_HEAD_DIM]
             for kvh in range(C_KV_HEADS) for _ in range(C_GROUP)], axis=0)
        s_own = jnp.sum(qb.astype(F32) * knew.astype(BF16).astype(F32), axis=-1, keepdims=True) * scale
        mtot = jnp.maximum(jnp.max(jnp.where(picked, ms, NEG), axis=0), s_own)
        wgt = jnp.where(picked, jnp.exp(ms - mtot[None]), 0.0)
        p_own = jnp.exp(s_own - mtot)
        l = jnp.sum(wgt * l_ref[...], axis=0) + p_own
        acc = jnp.sum(wgt * acc_ref[...], axis=0) + p_own.astype(BF16).astype(F32) * vnew.astype(BF16).astype(F32)
        o_ref[...] = (acc / l).astype(o_ref.dtype)


def _moba_decode(z, cache_k, cache_v, page_table):
    nb, n_pages = page_table.shape
    assert n_pages % PAGES_PER_STEP == 0 and 2 * C_KV_HEADS == SUBLANES
    n_blocks = n_pages * PAGE_SIZE // MOBA_BLOCK
    steps = n_pages // PAGES_PER_STEP

    def page_spec(i):
        return pl.BlockSpec((None, PAGE_SIZE * C_KV_HEADS, C_HEAD_DIM),
                            lambda b, j, pt: (pt[b, j * PAGES_PER_STEP + i], 0, 0))

    pages = [page_spec(i) for i in range(PAGES_PER_STEP)]
    zw = z.shape[-1]
    return pl.pallas_call(
        functools.partial(_moba_decode_kernel, n_blocks=n_blocks),
        out_shape=jax.ShapeDtypeStruct((nb, C_HEADS, C_HEAD_DIM), BF16),
        grid_spec=pltpu.PrefetchScalarGridSpec(
            num_scalar_prefetch=1,
            grid=(nb, steps),
            in_specs=[pl.BlockSpec((None, 1, zw), lambda b, j, pt: (b, 0, 0))] + pages + pages,
            out_specs=pl.BlockSpec((None, C_HEADS, C_HEAD_DIM), lambda b, j, pt: (b, 0, 0)),
            scratch_shapes=[pltpu.VMEM((n_blocks, C_HEADS, 1), F32), pltpu.VMEM((n_blocks, C_HEADS, 1), F32),
                            pltpu.VMEM((n_blocks, C_HEADS, C_HEAD_DIM), F32),
                            pltpu.VMEM((n_blocks, C_HEADS, 1), F32)]),
        compiler_params=_cparams(("parallel", "arbitrary")),
        name="moba_decode",
    )(page_table, z, *([cache_k] * PAGES_PER_STEP), *([cache_v] * PAGES_PER_STEP))


def _pad_a_cols(x):
    i3 = 3 * A_WIDTH
    z = lambda n: jnp.zeros(x.shape[:-1] + (n,), x.dtype)
    return jnp.concatenate([
        x[..., :i3],
        x[..., i3:i3 + DECAY_RANK], z(LANES - DECAY_RANK),
        x[..., i3 + DECAY_RANK:i3 + DECAY_RANK + AAA_RANK], z(LANES - AAA_RANK),
        x[..., i3 + DECAY_RANK + AAA_RANK:], z(2 * LANES - GATE_RANK)], axis=-1)


def _unpad_a_cols(x):
    return jnp.concatenate([x[..., :A_OFF_W + DECAY_RANK], x[..., A_OFF_A:A_OFF_A + AAA_RANK],
                            x[..., A_OFF_G:A_OFF_G + GATE_RANK]], axis=-1)


def _pad_rows(w, n):
    return jnp.concatenate([w, jnp.zeros((n - w.shape[0],) + w.shape[1:], w.dtype)], axis=0)


def _ab_params(i, w_in_ab, mu_ab, w0, w2, a0, a2, g2, k_k, k_a, r_k, lnx_g, lnx_b, dw_w, dw_b, cn_g, cn_b):
    row = lambda x: x.reshape(1, -1)
    head = jnp.arange(A_WIDTH)[:, None] // A_HEAD == jnp.arange(LANES)[None, :]
    return {
        "w_a": _pad_a_cols(w_in_ab[i][:, :A_COLS]).astype(BF16),
        "w_b": w_in_ab[i][:, A_COLS:].astype(BF16),
        "mu": row(_pad_a_cols(mu_ab[i])),
        "w0": row(w0[i]), "a0": row(a0[i]), "k_k": row(k_k[i]), "k_a": row(k_a[i]), "r_k": row(r_k[i]),
        "lnx_g": row(lnx_g[i]), "lnx_b": row(lnx_b[i]),
        "w2": _pad_rows(w2[i], LANES).astype(BF16), "a2": _pad_rows(a2[i], LANES).astype(BF16),
        "g2": _pad_rows(g2[i], 2 * LANES).astype(BF16),
        "seg": head.astype(BF16), "segt": head.T.astype(BF16),
        "dw_w": dw_w[i], "dw_b": row(dw_b[i]), "cn_g": row(cn_g[i]), "cn_b": row(cn_b[i]),
    }


def _state_to_pairs(s):
    nb = s.shape[0]
    s = s.reshape(nb, A_HEADS // 2, 2, A_HEAD, A_HEAD)
    z = jnp.zeros_like(s[:, :, 0])
    top = jnp.concatenate([s[:, :, 0], z], axis=-1)
    bot = jnp.concatenate([z, s[:, :, 1]], axis=-1)
    return jnp.concatenate([top, bot], axis=-2)


def _pairs_to_state(s):
    nb = s.shape[0]
    return jnp.stack([s[:, :, :A_HEAD, :A_HEAD], s[:, :, A_HEAD:, A_HEAD:]], axis=2).reshape(
        nb, A_HEADS, A_HEAD, A_HEAD)


def _mixer_ab(x, g_pre, shift_prev, wkv0, conv_buf, prm, *, nb, t):
    if t >= WKV_CHUNK:
        t_pad, tt_a, tt_b, xp = t, min(t, 256), min(t, 128), x
    else:
        t_pad, tt_a, tt_b = WKV_CHUNK, WKV_CHUNK, WKV_CHUNK
        xp = jnp.pad(x.reshape(nb, t, -1), ((0, 0), (0, t_pad - t), (0, 0))).reshape(nb * t_pad, -1)
    za = _norm_matmul(xp, g_pre, prm["w_a"], F32, tn=512)
    zb = _norm_matmul(xp, g_pre, prm["w_b"], F32, tn=512)
    ya, shift, s_new = _wkv_mix(za, _pad_a_cols(shift_prev)[:, None, :], _state_to_pairs(wkv0), prm,
                                nb=nb, t_pad=t_pad, t_valid=t, tt=tt_a)
    hist = jnp.pad(conv_buf, ((0, 0), (CONV_HALO - (CONV_W - 1), 0), (0, 0)))
    yb, conv_new = _conv_module(zb, hist, prm, nb=nb, t_pad=t_pad, t_valid=t, tt=tt_b)
    y = jnp.concatenate([ya, yb], axis=-1)
    if t_pad != t:
        y = y.reshape(nb, t_pad, -1)[:, :t].reshape(nb * t, -1)
    return y, _unpad_a_cols(shift[:, 0]), _pairs_to_state(s_new), conv_new


def _rest_of_layer(x, l, k_arr, v_arr, k_col, v_col, wts, *, nb, t):
    (g_x_pre, g_x_post, w_xq, w_xo, g_ffn_pre, g_ffn_post, w_gu, w_down) = wts
    q = _norm_matmul(x, g_x_pre[l], w_xq, BF16, tn=X_W)
    rep = 1 if t >= 16 else 16
    q3 = q.reshape(nb, t, X_W)
    if rep > 1:
        q3 = jnp.repeat(q3, rep, axis=1)
    o = _cross_attention(q3, k_arr, v_arr, k_col, v_col, tq=min(t * rep, 512))
    o = o[:, ::rep].reshape(nb * t, X_W)
    x = _matmul_norm_res(o, w_xo, g_x_post[l], x)
    h = _norm_matmul(x, g_ffn_pre[l], w_gu, BF16, tn=512, swiglu_half=D_FF)
    return _matmul_norm_res(h, w_down, g_ffn_post[l], x)


def kernel(x_prompt, x_sample, cache_k, cache_v, state_wkv, state_shift, state_conv, cache_mem_k, cache_mem_v, page_table, mem_prompt, w_in_ab, w_out_ab, mu_ab, w0, w2, a0, a2, g2, k_k, k_a, r_k, lnx_g, lnx_b, dw_w, dw_b, cn_g, cn_b, w_in_c, w_out_c, g_mix_pre, g_mix_post, g_x_pre, g_x_post, g_mem, w_xq, w_xkv, w_xo, g_ffn_pre, g_ffn_post, w_ffn_gu, w_ffn_down):
    bp, tp, d = x_prompt.shape
    bs, ts, _ = x_sample.shape
    depth = g_mix_pre.shape[0]
    xp = x_prompt.reshape(bp * tp, d)
    xs = x_sample.reshape(bs * ts, d)
    mem = mem_prompt.reshape(bp * MEM_LEN, d)
    outs = {n: [] for n in ("kp", "vp", "wkvp", "shp", "cvp", "mkp", "mvp", "ks", "vs", "wkvs", "shs", "cvs")}
    for l in range(depth):
        i = l // 2
        if l % 2 == 0:
            prm = _ab_params(i, w_in_ab, mu_ab, w0, w2, a0, a2, g2, k_k, k_a, r_k, lnx_g, lnx_b,
                             dw_w, dw_b, cn_g, cn_b)
            w_out = w_out_ab[i].astype(BF16)
            fp, shp, wkvp, cvp = _mixer_ab(
                xp, g_mix_pre[l], jnp.zeros((bp, A_COLS), F32), jnp.zeros((bp, A_HEADS, A_HEAD, A_HEAD), F32),
                jnp.zeros((bp, CONV_W - 1, B_WIDTH), F32), prm, nb=bp, t=tp)
            fs, shs, wkvs, cvs = _mixer_ab(xs, g_mix_pre[l], state_shift[i], state_wkv[i], state_conv[i], prm,
                                           nb=bs, t=ts)
            for n, val in (("shp", shp), ("wkvp", wkvp), ("cvp", cvp), ("shs", shs), ("wkvs", wkvs), ("cvs", cvs)):
                outs[n].append(val)
        else:
            w_in = w_in_c[i].astype(BF16)
            w_out = w_out_c[i].astype(BF16)
            zp = _norm_matmul(xp, g_mix_pre[l], w_in, F32, tn=512)
            fp = _moba_prompt(zp, nb=bp, t=tp)
            zs = _norm_matmul(xs, g_mix_pre[l], w_in, F32, tn=512)
            n_phys = cache_k.shape[1]
            page_rows = (-1, PAGE_SIZE * C_KV_HEADS, C_HEAD_DIM)
            fs = _moba_decode(zs.reshape(bs, ts, -1), cache_k.reshape(page_rows), cache_v.reshape(page_rows),
                              page_table + i * n_phys).reshape(bs * ts, C_Q_W)
            kv = lambda z, nb, t, off: z[:, off:off + C_KV_W].reshape(nb, t, C_KV_HEADS, C_HEAD_DIM)
            outs["kp"].append(kv(zp, bp, tp, C_Q_W))
            outs["vp"].append(kv(zp, bp, tp, C_Q_W + C_KV_W))
            outs["ks"].append(kv(zs, bs, ts, C_Q_W))
            outs["vs"].append(kv(zs, bs, ts, C_Q_W + C_KV_W))
        xp = _matmul_norm_res(fp, w_out, g_mix_post[l], xp)
        xs = _matmul_norm_res(fs, w_out, g_mix_post[l], xs)
        mkv = _norm_matmul(mem, g_mem[l], w_xkv[l].astype(BF16), F32, tn=512)
        outs["mkp"].append(mkv[:, :X_W].reshape(bp, MEM_LEN, X_HEADS, X_HEAD_DIM))
        outs["mvp"].append(mkv[:, X_W:].reshape(bp, MEM_LEN, X_HEADS, X_HEAD_DIM))
        wts = (g_x_pre, g_x_post, w_xq[l].astype(BF16), w_xo[l].astype(BF16), g_ffn_pre, g_ffn_post,
               w_ffn_gu[l].astype(BF16), w_ffn_down[l].astype(BF16))
        xp = _rest_of_layer(xp, l, mkv, mkv, 0, 1, wts, nb=bp, t=tp)
        xs = _rest_of_layer(xs, l, cache_mem_k[l].reshape(bs * MEM_LEN, X_W),
                            cache_mem_v[l].reshape(bs * MEM_LEN, X_W), 0, 0, wts, nb=bs, t=ts)
    st = lambda n: jnp.stack(outs[n])
    return (xp.reshape(bp, tp, d), xs.reshape(bs, ts, d), st("kp"), st("vp"), st("wkvp"), st("shp"), st("cvp"),
            st("mkp"), st("mvp"), st("ks"), st("vs"), st("wkvs"), st("shs"), st("cvs"))
```

```python
import functools
import math

import jax
import jax.numpy as jnp
from jax import lax
from jax.experimental import pallas as pl
from jax.experimental.pallas import tpu as pltpu

F32 = jnp.float32
BF16 = jnp.bfloat16

D_MODEL = 2048
PAGE_SIZE = 128
A_WIDTH = 1024
A_HEAD = 64
A_HEADS = 16
DECAY_RANK = 64
AAA_RANK = 64
GATE_RANK = 160
A_COLS = 3 * A_WIDTH + DECAY_RANK + AAA_RANK + GATE_RANK
GN_EPS = 64e-5
B_WIDTH = 1024
CONV_W = 31
C_HEADS = 16
C_HEAD_DIM = 128
C_KV_HEADS = 4
C_GROUP = 4
C_Q_W = 2048
C_KV_W = 512
MOBA_BLOCK = 256
MOBA_TOPK = 3
Q_BLOCK = MOBA_BLOCK
MEM_LEN = 256
X_HEADS = 4
X_HEAD_DIM = 128
X_W = 512
D_FF = 5632
RMS_EPS = 1e-6
LN_EPS = 1e-5
NEG = -1e30

SUBLANES = 8
MXU_WIDTH = 256
LANES = 128
A_OFF_W = 3 * A_WIDTH
A_OFF_A = A_OFF_W + LANES
A_OFF_G = A_OFF_A + LANES
A_PAD = A_OFF_G + 2 * LANES
WKV_CHUNK = 64
WKV_SHORT_CHUNK = 16
VMEM_LIMIT = 56 * 1024 * 1024


def _cparams(sem):
    return pltpu.CompilerParams(dimension_semantics=sem, vmem_limit_bytes=VMEM_LIMIT)


def _dot(a, b):
    return jnp.dot(a, b, preferred_element_type=F32)


def _dot_nt(a, b):
    return lax.dot_general(a, b, (((1,), (1,)), ((), ())), preferred_element_type=F32)


def _dot_tn(a, b):
    return lax.dot_general(a, b, (((0,), (0,)), ((), ())), preferred_element_type=F32)


def _split2(x):
    hi = x.astype(BF16)
    lo = (x - hi.astype(F32)).astype(BF16)
    return hi, lo


def _dot_x3(dot, a, b):
    ah, al = _split2(a)
    bh, bl = _split2(b)
    return dot(ah, bh) + (dot(ah, bl) + dot(al, bh))


def _dot_exact_rhs(a, b_bf16):
    hi, lo = _split2(a)
    return _dot(hi, b_bf16) + _dot(lo, b_bf16)


def _dot_exact_lhs(a_bf16, b):
    hi, lo = _split2(b)
    return _dot(a_bf16, hi) + _dot(a_bf16, lo)


def _sigmoid(x):
    return 1.0 / (1.0 + jnp.exp(-x))


def _norm_mm_kernel(x_ref, g_ref, *refs, n_w):
    w_refs, o_ref, xn_ref = refs[:n_w], refs[n_w], refs[n_w + 1]

    @pl.when(pl.program_id(1) == 0)
    def _():
        x = x_ref[...]
        ms = jnp.mean(x * x, axis=-1, keepdims=True)
        xn_ref[...] = (x * lax.rsqrt(ms + RMS_EPS) * g_ref[...]).astype(BF16)

    xn = xn_ref[...]
    tn = o_ref.shape[1]
    sub = MXU_WIDTH if tn % MXU_WIDTH == 0 else tn
    for c0 in range(0, tn, sub):
        cs = slice(c0, c0 + sub)
        if n_w == 1:
            y = _dot(xn, w_refs[0][:, cs])
        else:
            gate = _dot(xn, w_refs[0][:, cs])
            y = gate * _sigmoid(gate) * _dot(xn, w_refs[1][:, cs])
        o_ref[:, cs] = y.astype(o_ref.dtype)


def _norm_matmul(x, g, w, out_dtype, *, tn, swiglu_half=None):
    m, d = x.shape
    n = swiglu_half if swiglu_half else w.shape[1]
    tm = min(m, 1024)
    assert m % tm == 0 and n % tn == 0
    w_specs = [pl.BlockSpec((d, tn), lambda i, j: (0, j))]
    operands = [x, g.reshape(1, d), w]
    if swiglu_half:
        off = swiglu_half // tn
        w_specs.append(pl.BlockSpec((d, tn), lambda i, j: (0, j + off)))
        operands.append(w)
    return pl.pallas_call(
        functools.partial(_norm_mm_kernel, n_w=len(w_specs)),
        out_shape=jax.ShapeDtypeStruct((m, n), out_dtype),
        grid=(m // tm, n // tn),
        in_specs=[pl.BlockSpec((tm, d), lambda i, j: (i, 0)),
                  pl.BlockSpec((1, d), lambda i, j: (0, 0))] + w_specs,
        out_specs=pl.BlockSpec((tm, tn), lambda i, j: (i, j)),
        scratch_shapes=[pltpu.VMEM((tm, d), BF16)],
        compiler_params=_cparams(("parallel", "arbitrary")),
        name="norm_matmul",
    )(*operands)


def _mm_norm_res_kernel(x_ref, w_ref, g_ref, r_ref, o_ref):
    y = _dot(x_ref[...], w_ref[...])
    ms = jnp.mean(y * y, axis=-1, keepdims=True)
    o_ref[...] = r_ref[...] + y * lax.rsqrt(ms + RMS_EPS) * g_ref[...]


def _matmul_norm_res(x, w, g, res):
    m, kd = x.shape
    n = w.shape[1]
    tm = min(m, 512 if kd * n * 2 <= 8 * 1024 * 1024 else 256)
    assert m % tm == 0
    return pl.pallas_call(
        _mm_norm_res_kernel,
        out_shape=jax.ShapeDtypeStruct((m, n), F32),
        grid=(m // tm,),
        in_specs=[pl.BlockSpec((tm, kd), lambda i: (i, 0)),
                  pl.BlockSpec((kd, n), lambda i: (0, 0), pipeline_mode=pl.Buffered(1)),
                  pl.BlockSpec((1, n), lambda i: (0, 0)),
                  pl.BlockSpec((tm, n), lambda i: (i, 0))],
        out_specs=pl.BlockSpec((tm, n), lambda i: (i, 0)),
        compiler_params=_cparams(("parallel",)),
        name="matmul_norm_res",
    )(x, w, g.reshape(1, n), res)


def _xattn_kernel(q_ref, k_ref, v_ref, o_ref):
    scale = 1.0 / math.sqrt(X_HEAD_DIM)
    for h in range(X_HEADS):
        sl = slice(h * X_HEAD_DIM, (h + 1) * X_HEAD_DIM)
        s = _dot_nt(q_ref[:, sl], k_ref[:, sl].astype(BF16)) * scale
        p = jnp.exp(s - jnp.max(s, axis=-1, keepdims=True))
        l = jnp.sum(p, axis=-1, keepdims=True)
        o = _dot(p.astype(BF16), v_ref[:, sl].astype(BF16))
        o_ref[:, sl] = (o / l).astype(o_ref.dtype)


def _cross_attention(q, k_arr, v_arr, k_col, v_col, *, tq):
    b, t, _ = q.shape
    assert t % tq == 0
    return pl.pallas_call(
        _xattn_kernel,
        out_shape=jax.ShapeDtypeStruct((b, t, X_W), BF16),
        grid=(b, t // tq),
        in_specs=[pl.BlockSpec((None, tq, X_W), lambda i, j: (i, j, 0)),
                  pl.BlockSpec((MEM_LEN, X_W), lambda i, j: (i, k_col)),
                  pl.BlockSpec((MEM_LEN, X_W), lambda i, j: (i, v_col))],
        out_specs=pl.BlockSpec((None, tq, X_W), lambda i, j: (i, j, 0)),
        compiler_params=_cparams(("parallel", "arbitrary")),
        name="cross_attention",
    )(q, k_arr, v_arr)


def _wkv_chunk_kernel(za_ref, sp_ref, s0_ref, mu_ref, w0_ref, a0_ref, kk_ref, ka_ref, rk_ref,
                      lng_ref, lnb_ref, w2_ref, a2_ref, g2_ref, seg_ref, segt_ref,
                      ya_ref, shift_ref, s_ref,
                      r_ref, v_ref, kt_ref, kn_ref, b_ref, lw_ref, cum_ref, tot_ref, y_ref, *, tt, c, t_valid):
    c2x = 2 * c
    ti = pl.program_id(1)
    pairs = range(A_HEADS // 2)
    cols = [slice(p * LANES, (p + 1) * LANES) for p in pairs]
    bf = lambda x: x.astype(BF16)

    @pl.when(ti == 0)
    def _():
        s_ref[...] = s0_ref[...]
        shift_ref[...] = sp_ref[...]

    seg = seg_ref[...]
    segt = segt_ref[...]

    def head_sum(x):
        return _dot_exact_rhs(_dot_exact_rhs(x, seg), segt)

    za = za_ref[...].astype(F32)
    row = lax.broadcasted_iota(jnp.int32, (tt, 1), 0)
    prev = jnp.where(row == 0, shift_ref[...], pltpu.roll(za, 1, axis=0))
    last = (t_valid - 1) % tt
    shift_ref[...] = za[last:last + 1, :]
    mix = za + (prev - za) * mu_ref[...]
    valid = (ti * tt + row) < t_valid
    r = mix[:, 0:A_WIDTH]
    k = mix[:, A_WIDTH:2 * A_WIDTH]
    v = jnp.where(valid, mix[:, 2 * A_WIDTH:3 * A_WIDTH], 0.0)
    xw = w0_ref[...] + _dot(bf(jnp.tanh(mix[:, A_OFF_W:A_OFF_A])), w2_ref[...])
    lw = jnp.where(valid, -_sigmoid(xw) * math.exp(-0.5), 0.0)
    a = _sigmoid(a0_ref[...] + _dot(bf(mix[:, A_OFF_A:A_OFF_G]), a2_ref[...]))
    g = _dot(bf(_sigmoid(mix[:, A_OFF_G:A_PAD])), g2_ref[...])
    kk = k * kk_ref[...]
    kn = jnp.where(valid, kk * lax.rsqrt(jnp.maximum(head_sum(kk * kk), 1e-24)), 0.0)
    kt = jnp.where(valid, k * (1.0 + (a - 1.0) * ka_ref[...]), 0.0)
    bonus = head_sum(r * kt * rk_ref[...]) * v
    ri = lax.broadcasted_iota(jnp.int32, (tt, tt), 0)
    ci = lax.broadcasted_iota(jnp.int32, (tt, tt), 1)
    same = (ri // c) == (ci // c)
    cum_ref[...] = _dot_exact_lhs(jnp.where(same & (ci <= ri), 1.0, 0.0).astype(BF16), lw)
    tot_ref[...] = _dot_exact_lhs(jnp.where(same, 1.0, 0.0).astype(BF16), lw)
    r_ref[...] = r
    v_ref[...] = v
    kt_ref[...] = kt
    kn_ref[...] = kn
    b_ref[...] = kn * a
    lw_ref[...] = lw

    lane = lax.broadcasted_iota(jnp.int32, (1, LANES), 1)
    m0 = (lane < A_HEAD).astype(F32)
    m1 = 1.0 - m0
    r2 = lax.broadcasted_iota(jnp.int32, (c2x, c2x), 0)
    c2 = lax.broadcasted_iota(jnp.int32, (c2x, c2x), 1)
    strict = (r2 % c) > (c2 % c)
    incl = (r2 % c) >= (c2 % c)

    def stack(x):
        return jnp.concatenate([x * m0, x * m1], axis=0)

    def prepare(ch):
        rows = slice(ch * c, (ch + 1) * c)
        cum, tot = cum_ref[rows, :], tot_ref[rows, :]
        kt_, b_ = kt_ref[rows, :], b_ref[rows, :]
        p_inv = jnp.exp(-cum)
        p_end = jnp.exp(tot - cum)
        return dict(v=v_ref[rows, :], rh=r_ref[rows, :] * jnp.exp(cum), kap=kn_ref[rows, :] * jnp.exp(cum - lw_ref[rows, :]),
                    kh=kt_ * p_inv, bh=b_ * p_inv, khp=kt_ * p_end, bhp=b_ * p_end, pt=jnp.exp(tot[0:1]))

    def solve(d):
        vst = [stack(d["v"][:, cols[p]]) for p in pairs]
        s0 = [s_ref[p] for p in pairs]
        lhs = [bf(jnp.concatenate([stack(d["kap"][:, cols[p]]), stack(d["rh"][:, cols[p]])], axis=0)) for p in pairs]
        gram = [_dot_nt(lhs[p], bf(jnp.concatenate([stack(d["kh"][:, cols[p]]), stack(d["bh"][:, cols[p]])], axis=0)))
                for p in pairs]
        from_s0 = [_dot_nt(lhs[p], bf(s0[p])) for p in pairs]
        x = [from_s0[p][:c2x] + _dot(bf(jnp.where(strict, gram[p][:c2x, :c2x], 0.0)), bf(vst[p])) for p in pairs]
        lp = [jnp.where(strict, gram[p][:c2x, c2x:], 0.0) for p in pairs]
        sq = [_dot(bf(lp[p]), bf(jnp.concatenate([lp[p], x[p]], axis=1))) for p in pairs]
        lp = [sq[p][:, :c2x] for p in pairs]
        x = [x[p] - sq[p][:, c2x:] for p in pairs]
        n = 2
        while 2 * n < c:
            sq = [_dot(bf(lp[p]), bf(jnp.concatenate([lp[p], x[p]], axis=1))) for p in pairs]
            lp = [sq[p][:, :c2x] for p in pairs]
            x = [x[p] + sq[p][:, c2x:] for p in pairs]
            n *= 2
        x = [x[p] + _dot(bf(lp[p]), bf(x[p])) for p in pairs]
        ys = []
        for p in pairs:
            tri = jnp.concatenate([jnp.where(incl, gram[p][c2x:, :c2x], 0.0),
                                   jnp.where(incl, -gram[p][c2x:, c2x:], 0.0)], axis=1)
            y = from_s0[p][c2x:] + _dot(bf(tri), bf(jnp.concatenate([vst[p], x[p]], axis=0)))
            ys.append(y[:c] + y[c:])
            decayed = jnp.concatenate([stack(d["khp"][:, cols[p]]), stack(d["bhp"][:, cols[p]])], axis=0)
            s_ref[p] = s0[p] * d["pt"][:, cols[p]] + _dot_tn(bf(jnp.concatenate([vst[p], -x[p]], axis=0)), bf(decayed))
        return jnp.concatenate(ys, axis=1)

    n_chunks = tt // c
    d = prepare(0)
    for ch in range(n_chunks):
        d_next = prepare(ch + 1) if ch + 1 < n_chunks else None
        y_ref[ch * c:(ch + 1) * c, :] = solve(d)
        d = d_next

    y = y_ref[...]
    mean = head_sum(y) * (1.0 / A_HEAD)
    yc = y - mean
    var = head_sum(yc * yc) * (1.0 / A_HEAD)
    yn = yc * lax.rsqrt(var + GN_EPS) * lng_ref[...] + lnb_ref[...]
    ya_ref[...] = ((yn + bonus) * g).astype(ya_ref.dtype)


def _wkv_mix(za, shift_prev, s0, prm, *, nb, t_pad, t_valid, tt, chunk):
    nt = t_pad // tt
    assert t_pad % tt == 0 and tt % chunk == 0 and chunk % SUBLANES == 0
    vec = lambda n: pl.BlockSpec((1, n), lambda i, j: (0, 0))
    full = lambda a: pl.BlockSpec(a.shape, lambda i, j: (0,) * a.ndim)
    state_spec = pl.BlockSpec((None, A_HEADS // 2, LANES, LANES), lambda i, j: (i, 0, 0, 0))
    shift_spec = pl.BlockSpec((None, 1, A_PAD), lambda i, j: (i, 0, 0))
    vecs = [prm[n] for n in ("mu", "w0", "a0", "k_k", "k_a", "r_k", "lnx_g", "lnx_b")]
    mats = [prm[n] for n in ("w2", "a2", "g2", "seg", "segt")]
    return pl.pallas_call(
        functools.partial(_wkv_chunk_kernel, tt=tt, c=chunk, t_valid=t_valid),
        out_shape=(jax.ShapeDtypeStruct((nb * t_pad, A_WIDTH), BF16),
                   jax.ShapeDtypeStruct((nb, 1, A_PAD), F32),
                   jax.ShapeDtypeStruct((nb, A_HEADS // 2, LANES, LANES), F32)),
        grid=(nb, nt),
        in_specs=[pl.BlockSpec((tt, A_PAD), lambda i, j: (i * nt + j, 0)), shift_spec, state_spec]
                 + [vec(x.shape[1]) for x in vecs] + [full(x) for x in mats],
        out_specs=(pl.BlockSpec((tt, A_WIDTH), lambda i, j: (i * nt + j, 0)), shift_spec, state_spec),
        scratch_shapes=[pltpu.VMEM((tt, A_WIDTH), F32)] * 9,
        compiler_params=_cparams(("parallel", "arbitrary")),
        name="wkv_mix",
    )(za, shift_prev, s0, *vecs, *mats)


CONV_HALO = 32


def _conv_kernel(zb_ref, cb_ref, w_ref, b_ref, g_ref, be_ref, o_ref, st_ref, ext_ref, conv_ref,
                 *, tt, t_last):
    ti = pl.program_id(1)
    pad = CONV_HALO - (CONV_W - 1)

    @pl.when(ti == 0)
    def _():
        ext_ref[0:CONV_HALO, :] = cb_ref[...]

    zb = zb_ref[...].astype(F32)
    ext_ref[CONV_HALO:CONV_HALO + tt, :] = zb[:, :B_WIDTH] * _sigmoid(zb[:, B_WIDTH:])
    rb = min(tt, 64)
    for r0 in range(0, tt, rb):
        for c0 in range(0, B_WIDTH, LANES):
            cols = slice(c0, c0 + LANES)
            x = ext_ref[r0:r0 + rb + CONV_HALO, cols]
            n = rb + CONV_HALO
            shifted = [x] + [pltpu.roll(x, n - s, axis=0) for s in range(1, SUBLANES)]
            acc = jnp.broadcast_to(b_ref[:, cols], (rb, LANES))
            for j in range(CONV_W):
                s = (pad + j) % SUBLANES
                base = pad + j - s
                acc = acc + shifted[s][base:base + rb] * w_ref[j:j + 1, cols]
            conv_ref[r0:r0 + rb, cols] = acc
    conv = conv_ref[...]
    mean = jnp.mean(conv, axis=-1, keepdims=True)
    xc = conv - mean
    var = jnp.mean(xc * xc, axis=-1, keepdims=True)
    y = xc * lax.rsqrt(var + LN_EPS) * g_ref[...] + be_ref[...]
    o_ref[...] = (y * _sigmoid(y)).astype(o_ref.dtype)
    st_ref[...] = ext_ref[t_last + pad:t_last + pad + CONV_W - 1, :]
    if tt >= CONV_HALO:
        ext_ref[0:CONV_HALO, :] = ext_ref[tt:tt + CONV_HALO, :]


def _conv_module(zb, conv_buf, prm, *, nb, t_pad, t_valid, tt):
    nt = t_pad // tt
    assert nt == 1 or tt >= CONV_HALO
    vec = pl.BlockSpec((1, B_WIDTH), lambda i, j: (0, 0))
    return pl.pallas_call(
        functools.partial(_conv_kernel, tt=tt, t_last=t_valid - (nt - 1) * tt),
        out_shape=(jax.ShapeDtypeStruct((nb * t_pad, B_WIDTH), BF16),
                   jax.ShapeDtypeStruct((nb, CONV_W - 1, B_WIDTH), F32)),
        grid=(nb, nt),
        in_specs=[pl.BlockSpec((tt, 2 * B_WIDTH), lambda i, j: (i * nt + j, 0)),
                  pl.BlockSpec((None, CONV_HALO, B_WIDTH), lambda i, j: (i, 0, 0)),
                  pl.BlockSpec((CONV_W, B_WIDTH), lambda i, j: (0, 0)), vec, vec, vec],
        out_specs=(pl.BlockSpec((tt, B_WIDTH), lambda i, j: (i * nt + j, 0)),
                   pl.BlockSpec((None, CONV_W - 1, B_WIDTH), lambda i, j: (i, 0, 0))),
        scratch_shapes=[pltpu.VMEM((CONV_HALO + tt, B_WIDTH), F32), pltpu.VMEM((tt, B_WIDTH), F32)],
        compiler_params=_cparams(("parallel", "arbitrary")),
        name="conv_module",
    )(zb, conv_buf, prm["dw_w"], prm["dw_b"], prm["cn_g"], prm["cn_b"])


def _top_blocks(gate, blk, n_sel):
    sel = []
    for _ in range(n_sel):
        m = jnp.max(gate, axis=0, keepdims=True)
        idx = jnp.min(jnp.where(gate == m, blk, gate.shape[0]), axis=0, keepdims=True)
        sel.append(idx)
        gate = jnp.where(blk == idx, -jnp.inf, gate)
    return sel


def _moba_prompt_kernel(q_ref, k_ref, v_ref, o_ref, kb_ref, vt_ref, km_ref, qf_ref, *, t):
    kvh = pl.program_id(1)
    qi = pl.program_id(2)
    nb = t // MOBA_BLOCK
    nq = C_GROUP * Q_BLOCK

    log2e = 1.0 / math.log(2.0)
    n_feat = 3

    def slope2_of(group):
        head = (kvh * C_GROUP + group + 1).astype(F32)
        return jnp.exp(head * (-(8.0 / C_HEADS) * math.log(2.0))) * log2e

    @pl.when(qi == 0)
    def _():
        srow = slope2_of(lax.broadcasted_iota(jnp.int32, (nq, 1), 0) // Q_BLOCK)
        s_hi = srow.astype(BF16).astype(F32)
        s_mid = (srow - s_hi).astype(BF16).astype(F32)
        s_lo = (srow - s_hi - s_mid).astype(BF16).astype(F32)
        part = lax.broadcasted_iota(jnp.int32, (1, LANES), 1)
        qfeat = jnp.where(part % n_feat == 0, s_hi, jnp.where(part % n_feat == 1, s_mid, s_lo))
        qf_ref[...] = jnp.where(part < 2 * n_feat, qfeat, 0.0).astype(BF16)
        km_ref[...] = jnp.zeros_like(km_ref)
        lane = lax.broadcasted_iota(jnp.int32, (MOBA_BLOCK, LANES), 1)
        offset = lax.broadcasted_iota(jnp.int32, (MOBA_BLOCK, LANES), 0).astype(F32)
        for n in range(nb):
            kblk = k_ref[n * MOBA_BLOCK:(n + 1) * MOBA_BLOCK, :]
            feat = jnp.where(lane < n_feat, float(n * MOBA_BLOCK), jnp.where(lane < 2 * n_feat, offset, 0.0))
            kb_ref[n] = jnp.concatenate([kblk, feat], axis=1).astype(BF16)
            vt_ref[n] = v_ref[n * MOBA_BLOCK:(n + 1) * MOBA_BLOCK, :].T.astype(BF16)
            km_ref[n:n + 1, :] = jnp.mean(kblk, axis=0, keepdims=True)

    q = q_ref[...]
    qs = jnp.concatenate([q[:, g * C_HEAD_DIM:(g + 1) * C_HEAD_DIM] for g in range(C_GROUP)], axis=0)
    own = (qi * Q_BLOCK) // MOBA_BLOCK
    blk = lax.broadcasted_iota(jnp.int32, (km_ref.shape[0], 1), 0)
    gate = _dot_x3(_dot_nt, km_ref[...], qs)
    gate = jnp.where(blk < own, gate, jnp.where(blk < nb, NEG, -jnp.inf))
    sel = _top_blocks(gate, blk, MOBA_TOPK)
    sel = [jnp.where(own > r, sel[r], -1) for r in range(MOBA_TOPK)]

    col = lax.broadcasted_iota(jnp.int32, (1, nq), 1)
    pos = qi * Q_BLOCK + col % Q_BLOCK
    base = -slope2_of(col // Q_BLOCK) * pos.astype(F32)
    qb = jnp.concatenate([(qs * (log2e / math.sqrt(C_HEAD_DIM))).astype(BF16), qf_ref[...]], axis=1)
    trow = lax.broadcasted_iota(jnp.int32, (MOBA_BLOCK, 1), 0)

    def attend_upto(own_static):
        scores = []
        for n in range(own_static):
            picked = (sel[0] == n) | (sel[1] == n) | (sel[2] == n)
            scores.append(_dot_nt(kb_ref[n], qb) + jnp.where(picked, base, NEG))
        causal = trow <= col % Q_BLOCK
        scores.append(jnp.where(causal, _dot_nt(kb_ref[own_static], qb) + base, NEG))
        m = scores[-1].max(axis=0, keepdims=True)
        for s in scores[:-1]:
            m = jnp.maximum(m, jnp.max(s, axis=0, keepdims=True))
        l = jnp.zeros((1, nq), F32)
        acc = jnp.zeros((C_HEAD_DIM, nq), F32)
        for n, s in enumerate(scores):
            p = jnp.exp2(s - m)
            l = l + jnp.sum(p, axis=0, keepdims=True)
            acc = acc + _dot(vt_ref[n], p.astype(BF16))
        o = acc / l
        for g in range(C_GROUP):
            cols = slice(g * Q_BLOCK, (g + 1) * Q_BLOCK)
            o_ref[:, g * C_HEAD_DIM:(g + 1) * C_HEAD_DIM] = o[:, cols].T.astype(o_ref.dtype)

    for own_static in range(nb):
        pl.when(own == own_static)(functools.partial(attend_upto, own_static))


def _moba_prompt(z, *, nb, t):
    nq = t // Q_BLOCK
    assert t % Q_BLOCK == 0 and t // MOBA_BLOCK <= 256 and MOBA_BLOCK <= 256
    kcol = C_Q_W // C_HEAD_DIM
    vcol = (C_Q_W + C_KV_W) // C_HEAD_DIM
    return pl.pallas_call(
        functools.partial(_moba_prompt_kernel, t=t),
        out_shape=jax.ShapeDtypeStruct((nb * t, C_Q_W), BF16),
        grid=(nb, C_KV_HEADS, nq),
        in_specs=[pl.BlockSpec((Q_BLOCK, C_GROUP * C_HEAD_DIM), lambda b, h, i: (b * nq + i, h)),
                  pl.BlockSpec((t, C_HEAD_DIM), lambda b, h, i: (b, kcol + h)),
                  pl.BlockSpec((t, C_HEAD_DIM), lambda b, h, i: (b, vcol + h))],
        out_specs=pl.BlockSpec((Q_BLOCK, C_GROUP * C_HEAD_DIM), lambda b, h, i: (b * nq + i, h)),
        scratch_shapes=[pltpu.VMEM((t // MOBA_BLOCK, MOBA_BLOCK, C_HEAD_DIM + LANES), BF16),
                        pltpu.VMEM((t // MOBA_BLOCK, C_HEAD_DIM, MOBA_BLOCK), BF16),
                        pltpu.VMEM((-(-t // MOBA_BLOCK // SUBLANES) * SUBLANES, C_HEAD_DIM), F32),
                        pltpu.VMEM((C_GROUP * Q_BLOCK, LANES), BF16)],
        compiler_params=_cparams(("parallel", "parallel", "arbitrary")),
        name="moba_prompt",
    )(z, z, z)


PAGES_PER_STEP = 4 * MOBA_BLOCK // PAGE_SIZE


def _moba_decode_kernel(pt_ref, z_ref, *refs, n_blocks):
    del pt_ref
    npg = PAGES_PER_STEP
    k_refs, v_refs = refs[:npg], refs[npg:2 * npg]
    o_ref = refs[2 * npg]
    m_ref, l_ref, acc_ref, gate_ref = refs[2 * npg + 1:]
    j = pl.program_id(1)
    ppb = MOBA_BLOCK // PAGE_SIZE
    past = n_blocks * MOBA_BLOCK
    scale = 1.0 / math.sqrt(C_HEAD_DIM)
    hrow = lax.broadcasted_iota(jnp.int32, (C_HEADS, 1), 0)
    slope = jnp.exp((hrow + 1).astype(F32) * (-(8.0 / C_HEADS) * math.log(2.0)))

    z = z_ref[...]
    qrows = jnp.concatenate([z[:, h * C_HEAD_DIM:(h + 1) * C_HEAD_DIM] for h in range(C_HEADS)], axis=0)
    qb = qrows.astype(BF16)
    rows_blk = MOBA_BLOCK * C_KV_HEADS
    col = lax.broadcasted_iota(jnp.int32, (1, rows_blk), 1)
    mine = (col % C_KV_HEADS) == (hrow // C_GROUP)
    tab = jnp.where(mine, slope * (col // C_KV_HEADS).astype(F32), NEG)

    for blk in range(npg // ppb):
        n = j * (npg // ppb) + blk
        kblk = jnp.concatenate([k_refs[blk * ppb + i][...] for i in range(ppb)], axis=0)
        vblk = jnp.concatenate([v_refs[blk * ppb + i][...] for i in range(ppb)], axis=0)
        s = _dot_nt(qb, kblk.astype(BF16)) * scale + tab - slope * (past - n * MOBA_BLOCK).astype(F32)
        m = jnp.max(s, axis=-1, keepdims=True)
        p = jnp.exp(s - m)
        m_ref[pl.ds(n, 1)] = m[None]
        l_ref[pl.ds(n, 1)] = jnp.sum(p, axis=-1, keepdims=True)[None]
        acc_ref[pl.ds(n, 1)] = _dot(p.astype(BF16), vblk.astype(BF16))[None]
        ksum = jnp.sum(kblk.reshape(rows_blk // SUBLANES, SUBLANES, C_HEAD_DIM), axis=0)
        kmean = (ksum[:C_KV_HEADS] + ksum[C_KV_HEADS:]) * (1.0 / MOBA_BLOCK)
        kmrows = jnp.concatenate([jnp.broadcast_to(kmean[kvh:kvh + 1], (C_GROUP, C_HEAD_DIM))
                                  for kvh in range(C_KV_HEADS)], axis=0)
        gate_ref[pl.ds(n, 1)] = jnp.sum(qrows * kmrows, axis=-1, keepdims=True)[None]

    @pl.when(j == pl.num_programs(1) - 1)
    def _():
        gates = gate_ref[...]
        ms = m_ref[...]
        bidx = lax.broadcasted_iota(jnp.int32, gates.shape, 0)
        picked = jnp.zeros(gates.shape, jnp.bool_)
        for _ in range(min(MOBA_TOPK, n_blocks)):
            gmax = jnp.max(gates, axis=0, keepdims=True)
            first = jnp.min(jnp.where(gates == gmax, bidx, n_blocks), axis=0, keepdims=True)
            hit = bidx == first
            picked = picked | hit
            gates = jnp.where(hit, -jnp.inf, gates)
        knew = jnp.concatenate(
            [z[:, C_Q_W + kvh * C_HEAD_DIM:C_Q_W + (kvh + 1) * C_HEAD_DIM] for kvh in range(C_KV_HEADS)
             for _ in range(C_GROUP)], axis=0)
        vnew = jnp.concatenate(
            [z[:, C_Q_W + C_KV_W + kvh * C_HEAD_DIM:C_Q_W + C_KV_W + (kvh + 1) * C_HEAD_DIM]
             for kvh in range(C_KV_HEADS) for _ in range(C_GROUP)], axis=0)
        s_own = jnp.sum(qb.astype(F32) * knew.astype(BF16).astype(F32), axis=-1, keepdims=True) * scale
        mtot = jnp.maximum(jnp.max(jnp.where(picked, ms, NEG), axis=0), s_own)
        wgt = jnp.where(picked, jnp.exp(ms - mtot[None]), 0.0)
        p_own = jnp.exp(s_own - mtot)
        l = jnp.sum(wgt * l_ref[...], axis=0) + p_own
        acc = jnp.sum(wgt * acc_ref[...], axis=0) + p_own.astype(BF16).astype(F32) * vnew.astype(BF16).astype(F32)
        o_ref[...] = (acc / l).astype(o_ref.dtype)


def _moba_decode(z, cache_k, cache_v, page_table):
    nb, n_pages = page_table.shape
    assert n_pages % PAGES_PER_STEP == 0 and 2 * C_KV_HEADS == SUBLANES
    n_blocks = n_pages * PAGE_SIZE // MOBA_BLOCK
    steps = n_pages // PAGES_PER_STEP

    def page_spec(i):
        return pl.BlockSpec((None, PAGE_SIZE * C_KV_HEADS, C_HEAD_DIM),
                            lambda b, j, pt: (pt[b, j * PAGES_PER_STEP + i], 0, 0))

    pages = [page_spec(i) for i in range(PAGES_PER_STEP)]
    zw = z.shape[-1]
    return pl.pallas_call(
        functools.partial(_moba_decode_kernel, n_blocks=n_blocks),
        out_shape=jax.ShapeDtypeStruct((nb, C_HEADS, C_HEAD_DIM), BF16),
        grid_spec=pltpu.PrefetchScalarGridSpec(
            num_scalar_prefetch=1,
            grid=(nb, steps),
            in_specs=[pl.BlockSpec((None, 1, zw), lambda b, j, pt: (b, 0, 0))] + pages + pages,
            out_specs=pl.BlockSpec((None, C_HEADS, C_HEAD_DIM), lambda b, j, pt: (b, 0, 0)),
            scratch_shapes=[pltpu.VMEM((n_blocks, C_HEADS, 1), F32), pltpu.VMEM((n_blocks, C_HEADS, 1), F32),
                            pltpu.VMEM((n_blocks, C_HEADS, C_HEAD_DIM), F32),
                            pltpu.VMEM((n_blocks, C_HEADS, 1), F32)]),
        compiler_params=_cparams(("parallel", "arbitrary")),
        name="moba_decode",
    )(page_table, z, *([cache_k] * PAGES_PER_STEP), *([cache_v] * PAGES_PER_STEP))


def _pad_a_cols(x):
    i3 = 3 * A_WIDTH
    z = lambda n: jnp.zeros(x.shape[:-1] + (n,), x.dtype)
    return jnp.concatenate([
        x[..., :i3],
        x[..., i3:i3 + DECAY_RANK], z(LANES - DECAY_RANK),
        x[..., i3 + DECAY_RANK:i3 + DECAY_RANK + AAA_RANK], z(LANES - AAA_RANK),
        x[..., i3 + DECAY_RANK + AAA_RANK:], z(2 * LANES - GATE_RANK)], axis=-1)


def _unpad_a_cols(x):
    return jnp.concatenate([x[..., :A_OFF_W + DECAY_RANK], x[..., A_OFF_A:A_OFF_A + AAA_RANK],
                            x[..., A_OFF_G:A_OFF_G + GATE_RANK]], axis=-1)


def _pad_rows(w, n):
    return jnp.concatenate([w, jnp.zeros((n - w.shape[0],) + w.shape[1:], w.dtype)], axis=0)


def _ab_params(i, w_in_ab, mu_ab, w0, w2, a0, a2, g2, k_k, k_a, r_k, lnx_g, lnx_b, dw_w, dw_b, cn_g, cn_b):
    row = lambda x: x.reshape(1, -1)
    head = jnp.arange(A_WIDTH)[:, None] // A_HEAD == jnp.arange(LANES)[None, :]
    return {
        "w_a": _pad_a_cols(w_in_ab[i][:, :A_COLS]).astype(BF16),
        "w_b": w_in_ab[i][:, A_COLS:].astype(BF16),
        "mu": row(_pad_a_cols(mu_ab[i])),
        "w0": row(w0[i]), "a0": row(a0[i]), "k_k": row(k_k[i]), "k_a": row(k_a[i]), "r_k": row(r_k[i]),
        "lnx_g": row(lnx_g[i]), "lnx_b": row(lnx_b[i]),
        "w2": _pad_rows(w2[i], LANES).astype(BF16), "a2": _pad_rows(a2[i], LANES).astype(BF16),
        "g2": _pad_rows(g2[i], 2 * LANES).astype(BF16),
        "seg": head.astype(BF16), "segt": head.T.astype(BF16),
        "dw_w": dw_w[i], "dw_b": row(dw_b[i]), "cn_g": row(cn_g[i]), "cn_b": row(cn_b[i]),
    }


def _state_to_pairs(s):
    nb = s.shape[0]
    s = s.reshape(nb, A_HEADS // 2, 2, A_HEAD, A_HEAD)
    z = jnp.zeros_like(s[:, :, 0])
    top = jnp.concatenate([s[:, :, 0], z], axis=-1)
    bot = jnp.concatenate([z, s[:, :, 1]], axis=-1)
    return jnp.concatenate([top, bot], axis=-2)


def _pairs_to_state(s):
    nb = s.shape[0]
    return jnp.stack([s[:, :, :A_HEAD, :A_HEAD], s[:, :, A_HEAD:, A_HEAD:]], axis=2).reshape(
        nb, A_HEADS, A_HEAD, A_HEAD)


def _mixer_ab(x, g_pre, shift_prev, wkv0, conv_buf, prm, *, nb, t):
    if t >= WKV_CHUNK:
        assert t % WKV_CHUNK == 0
        t_pad, chunk, tt_a, tt_b, xp = t, WKV_CHUNK, min(t, 256), min(t, 128), x
    else:
        t_pad = chunk = tt_a = tt_b = -(-t // WKV_SHORT_CHUNK) * WKV_SHORT_CHUNK
        xp = jnp.pad(x.reshape(nb, t, -1), ((0, 0), (0, t_pad - t), (0, 0))).reshape(nb * t_pad, -1)
    za = _norm_matmul(xp, g_pre, prm["w_a"], BF16, tn=512)
    zb = _norm_matmul(xp, g_pre, prm["w_b"], BF16, tn=512)
    ya, shift, s_new = _wkv_mix(za, _pad_a_cols(shift_prev)[:, None, :], _state_to_pairs(wkv0), prm,
                                nb=nb, t_pad=t_pad, t_valid=t, tt=tt_a, chunk=chunk)
    hist = jnp.pad(conv_buf, ((0, 0), (CONV_HALO - (CONV_W - 1), 0), (0, 0)))
    yb, conv_new = _conv_module(zb, hist, prm, nb=nb, t_pad=t_pad, t_valid=t, tt=tt_b)
    y = jnp.concatenate([ya, yb], axis=-1)
    if t_pad != t:
        y = y.reshape(nb, t_pad, -1)[:, :t].reshape(nb * t, -1)
    return y, _unpad_a_cols(shift[:, 0]), _pairs_to_state(s_new), conv_new


def _rest_of_layer(x, l, k_arr, v_arr, k_col, v_col, wts, *, nb, t):
    (g_x_pre, g_x_post, w_xq, w_xo, g_ffn_pre, g_ffn_post, w_gu, w_down) = wts
    q = _norm_matmul(x, g_x_pre[l], w_xq, BF16, tn=X_W)
    rep = 1 if t >= 16 else 16
    q3 = q.reshape(nb, t, X_W)
    if rep > 1:
        q3 = jnp.repeat(q3, rep, axis=1)
    o = _cross_attention(q3, k_arr, v_arr, k_col, v_col, tq=min(t * rep, 512))
    o = o[:, ::rep].reshape(nb * t, X_W)
    x = _matmul_norm_res(o, w_xo, g_x_post[l], x)
    h = _norm_matmul(x, g_ffn_pre[l], w_gu, BF16, tn=512, swiglu_half=D_FF)
    return _matmul_norm_res(h, w_down, g_ffn_post[l], x)


def kernel(x_prompt, x_sample, cache_k, cache_v, state_wkv, state_shift, state_conv, cache_mem_k, cache_mem_v, page_table, mem_prompt, w_in_ab, w_out_ab, mu_ab, w0, w2, a0, a2, g2, k_k, k_a, r_k, lnx_g, lnx_b, dw_w, dw_b, cn_g, cn_b, w_in_c, w_out_c, g_mix_pre, g_mix_post, g_x_pre, g_x_post, g_mem, w_xq, w_xkv, w_xo, g_ffn_pre, g_ffn_post, w_ffn_gu, w_ffn_down):
    bp, tp, d = x_prompt.shape
    bs, ts, _ = x_sample.shape
    depth = g_mix_pre.shape[0]
    xp = x_prompt.reshape(bp * tp, d)
    xs = x_sample.reshape(bs * ts, d)
    mem = mem_prompt.reshape(bp * MEM_LEN, d)
    outs = {n: [] for n in ("kp", "vp", "wkvp", "shp", "cvp", "mkp", "mvp", "ks", "vs", "wkvs", "shs", "cvs")}
    for l in range(depth):
        i = l // 2
        if l % 2 == 0:
            prm = _ab_params(i, w_in_ab, mu_ab, w0, w2, a0, a2, g2, k_k, k_a, r_k, lnx_g, lnx_b,
                             dw_w, dw_b, cn_g, cn_b)
            w_out = w_out_ab[i].astype(BF16)
            fp, shp, wkvp, cvp = _mixer_ab(
                xp, g_mix_pre[l], jnp.zeros((bp, A_COLS), F32), jnp.zeros((bp, A_HEADS, A_HEAD, A_HEAD), F32),
                jnp.zeros((bp, CONV_W - 1, B_WIDTH), F32), prm, nb=bp, t=tp)
            fs, shs, wkvs, cvs = _mixer_ab(xs, g_mix_pre[l], state_shift[i], state_wkv[i], state_conv[i], prm,
                                           nb=bs, t=ts)
            for n, val in (("shp", shp), ("wkvp", wkvp), ("cvp", cvp), ("shs", shs), ("wkvs", wkvs), ("cvs", cvs)):
                outs[n].append(val)
        else:
            w_in = w_in_c[i].astype(BF16)
            w_out = w_out_c[i].astype(BF16)
            zp = _norm_matmul(xp, g_mix_pre[l], w_in, F32, tn=512)
            fp = _moba_prompt(zp, nb=bp, t=tp)
            zs = _norm_matmul(xs, g_mix_pre[l], w_in, F32, tn=512)
            n_phys = cache_k.shape[1]
            page_rows = (-1, PAGE_SIZE * C_KV_HEADS, C_HEAD_DIM)
            fs = _moba_decode(zs.reshape(bs, ts, -1), cache_k.reshape(page_rows), cache_v.reshape(page_rows),
                              page_table + i * n_phys).reshape(bs * ts, C_Q_W)
            kv = lambda z, nb, t, off: z[:, off:off + C_KV_W].reshape(nb, t, C_KV_HEADS, C_HEAD_DIM)
            outs["kp"].append(kv(zp, bp, tp, C_Q_W))
            outs["vp"].append(kv(zp, bp, tp, C_Q_W + C_KV_W))
            outs["ks"].append(kv(zs, bs, ts, C_Q_W))
            outs["vs"].append(kv(zs, bs, ts, C_Q_W + C_KV_W))
        xp = _matmul_norm_res(fp, w_out, g_mix_post[l], xp)
        xs = _matmul_norm_res(fs, w_out, g_mix_post[l], xs)
        mkv = _norm_matmul(mem, g_mem[l], w_xkv[l].astype(BF16), F32, tn=512)
        outs["mkp"].append(mkv[:, :X_W].reshape(bp, MEM_LEN, X_HEADS, X_HEAD_DIM))
        outs["mvp"].append(mkv[:, X_W:].reshape(bp, MEM_LEN, X_HEADS, X_HEAD_DIM))
        wts = (g_x_pre, g_x_post, w_xq[l].astype(BF16), w_xo[l].astype(BF16), g_ffn_pre, g_ffn_post,
               w_ffn_gu[l].astype(BF16), w_ffn_down[l].astype(BF16))
        xp = _rest_of_layer(xp, l, mkv, mkv, 0, 1, wts, nb=bp, t=tp)
        xs = _rest_of_layer(xs, l, cache_mem_k[l].reshape(bs * MEM_LEN, X_W),
                            cache_mem_v[l].reshape(bs * MEM_LEN, X_W), 0, 0, wts, nb=bs, t=ts)
    st = lambda n: jnp.stack(outs[n])
    return (xp.reshape(bp, tp, d), xs.reshape(bs, ts, d), st("kp"), st("vp"), st("wkvp"), st("shp"), st("cvp"),
            st("mkp"), st("mvp"), st("ks"), st("vs"), st("wkvs"), st("shs"), st("cvs"))
```

```python
import functools
import math

import jax
import jax.numpy as jnp
from jax import lax
from jax.experimental import pallas as pl
from jax.experimental.pallas import tpu as pltpu

F32 = jnp.float32
BF16 = jnp.bfloat16

D_MODEL = 2048
PAGE_SIZE = 128
A_WIDTH = 1024
A_HEAD = 64
A_HEADS = 16
DECAY_RANK = 64
AAA_RANK = 64
GATE_RANK = 160
A_COLS = 3 * A_WIDTH + DECAY_RANK + AAA_RANK + GATE_RANK
GN_EPS = 64e-5
B_WIDTH = 1024
CONV_W = 31
C_HEADS = 16
C_HEAD_DIM = 128
C_KV_HEADS = 4
C_GROUP = 4
C_Q_W = 2048
C_KV_W = 512
MOBA_BLOCK = 256
MOBA_TOPK = 3
Q_BLOCK = MOBA_BLOCK
MEM_LEN = 256
X_HEADS = 4
X_HEAD_DIM = 128
X_W = 512
D_FF = 5632
RMS_EPS = 1e-6
LN_EPS = 1e-5
NEG = -1e30

SUBLANES = 8
MXU_WIDTH = 256
BF16_ROWS = 16
LANES = 128
A_OFF_W = 3 * A_WIDTH
A_OFF_A = A_OFF_W + LANES
A_OFF_G = A_OFF_A + LANES
A_PAD = A_OFF_G + 2 * LANES
WKV_CHUNK = 64
WKV_SHORT_CHUNK = BF16_ROWS
VMEM_LIMIT = 56 * 1024 * 1024


def _cparams(sem):
    return pltpu.CompilerParams(dimension_semantics=sem, vmem_limit_bytes=VMEM_LIMIT)


def _dot(a, b):
    return jnp.dot(a, b, preferred_element_type=F32)


def _dot_nt(a, b):
    return lax.dot_general(a, b, (((1,), (1,)), ((), ())), preferred_element_type=F32)


def _dot_tn(a, b):
    return lax.dot_general(a, b, (((0,), (0,)), ((), ())), preferred_element_type=F32)


def _split2(x):
    hi = x.astype(BF16)
    lo = (x - hi.astype(F32)).astype(BF16)
    return hi, lo


def _dot_x3(dot, a, b):
    ah, al = _split2(a)
    bh, bl = _split2(b)
    return dot(ah, bh) + (dot(ah, bl) + dot(al, bh))


def _dot_exact_rhs(a, b_bf16):
    hi, lo = _split2(a)
    return _dot(hi, b_bf16) + _dot(lo, b_bf16)


def _dot_exact_lhs(a_bf16, b):
    hi, lo = _split2(b)
    return _dot(a_bf16, hi) + _dot(a_bf16, lo)


def _sigmoid(x):
    return 1.0 / (1.0 + jnp.exp(-x))


def _norm_mm_kernel(x_ref, g_ref, *refs, n_w):
    w_refs, o_ref, xn_ref = refs[:n_w], refs[n_w], refs[n_w + 1]

    @pl.when(pl.program_id(1) == 0)
    def _():
        x = x_ref[...]
        ms = jnp.mean(x * x, axis=-1, keepdims=True)
        xn_ref[...] = (x * lax.rsqrt(ms + RMS_EPS) * g_ref[...]).astype(BF16)

    xn = xn_ref[...]
    tn = o_ref.shape[1]
    sub = MXU_WIDTH if tn % MXU_WIDTH == 0 else tn
    for c0 in range(0, tn, sub):
        cs = slice(c0, c0 + sub)
        if n_w == 1:
            y = _dot(xn, w_refs[0][:, cs])
        else:
            gate = _dot(xn, w_refs[0][:, cs])
            y = gate * _sigmoid(gate) * _dot(xn, w_refs[1][:, cs])
        o_ref[:, cs] = y.astype(o_ref.dtype)


def _norm_matmul(x, g, w, out_dtype, *, tn, swiglu_half=None):
    m, d = x.shape
    n = swiglu_half if swiglu_half else w.shape[1]
    tm = min(m, 1024)
    assert m % tm == 0 and n % tn == 0
    w_specs = [pl.BlockSpec((d, tn), lambda i, j: (0, j))]
    operands = [x, g.reshape(1, d), w]
    if swiglu_half:
        off = swiglu_half // tn
        w_specs.append(pl.BlockSpec((d, tn), lambda i, j: (0, j + off)))
        operands.append(w)
    return pl.pallas_call(
        functools.partial(_norm_mm_kernel, n_w=len(w_specs)),
        out_shape=jax.ShapeDtypeStruct((m, n), out_dtype),
        grid=(m // tm, n // tn),
        in_specs=[pl.BlockSpec((tm, d), lambda i, j: (i, 0)),
                  pl.BlockSpec((1, d), lambda i, j: (0, 0))] + w_specs,
        out_specs=pl.BlockSpec((tm, tn), lambda i, j: (i, j)),
        scratch_shapes=[pltpu.VMEM((tm, d), BF16)],
        compiler_params=_cparams(("parallel", "arbitrary")),
        name="norm_matmul",
    )(*operands)


def _mm_norm_res_kernel(x_ref, w_ref, g_ref, r_ref, o_ref):
    y = _dot(x_ref[...], w_ref[...])
    ms = jnp.mean(y * y, axis=-1, keepdims=True)
    o_ref[...] = r_ref[...] + y * lax.rsqrt(ms + RMS_EPS) * g_ref[...]


def _matmul_norm_res(x, w, g, res):
    m, kd = x.shape
    n = w.shape[1]
    tm = min(m, 512 if kd * n * 2 <= 8 * 1024 * 1024 else 256)
    assert m % tm == 0
    return pl.pallas_call(
        _mm_norm_res_kernel,
        out_shape=jax.ShapeDtypeStruct((m, n), F32),
        grid=(m // tm,),
        in_specs=[pl.BlockSpec((tm, kd), lambda i: (i, 0)),
                  pl.BlockSpec((kd, n), lambda i: (0, 0), pipeline_mode=pl.Buffered(1)),
                  pl.BlockSpec((1, n), lambda i: (0, 0)),
                  pl.BlockSpec((tm, n), lambda i: (i, 0))],
        out_specs=pl.BlockSpec((tm, n), lambda i: (i, 0)),
        compiler_params=_cparams(("parallel",)),
        name="matmul_norm_res",
    )(x, w, g.reshape(1, n), res)


def _xattn_kernel(q_ref, k_ref, v_ref, o_ref):
    scale = 1.0 / math.sqrt(X_HEAD_DIM)
    for h in range(X_HEADS):
        sl = slice(h * X_HEAD_DIM, (h + 1) * X_HEAD_DIM)
        s = _dot_nt(q_ref[:, sl], k_ref[:, sl].astype(BF16)) * scale
        p = jnp.exp(s - jnp.max(s, axis=-1, keepdims=True))
        l = jnp.sum(p, axis=-1, keepdims=True)
        o = _dot(p.astype(BF16), v_ref[:, sl].astype(BF16))
        o_ref[:, sl] = (o / l).astype(o_ref.dtype)


def _cross_attention(q, k_arr, v_arr, k_col, v_col, *, tq):
    b, t, _ = q.shape
    assert t % tq == 0
    return pl.pallas_call(
        _xattn_kernel,
        out_shape=jax.ShapeDtypeStruct((b, t, X_W), BF16),
        grid=(b, t // tq),
        in_specs=[pl.BlockSpec((None, tq, X_W), lambda i, j: (i, j, 0)),
                  pl.BlockSpec((MEM_LEN, X_W), lambda i, j: (i, k_col)),
                  pl.BlockSpec((MEM_LEN, X_W), lambda i, j: (i, v_col))],
        out_specs=pl.BlockSpec((None, tq, X_W), lambda i, j: (i, j, 0)),
        compiler_params=_cparams(("parallel", "arbitrary")),
        name="cross_attention",
    )(q, k_arr, v_arr)


XATTN_SEQS_PER_STEP = 4
XATTN_HEAD_ROWS = BF16_ROWS


def _xattn_token_kernel(q_ref, k_ref, v_ref, o_ref):
    rows = MEM_LEN * X_HEADS
    scale = 1.0 / math.sqrt(X_HEAD_DIM)
    col = lax.broadcasted_iota(jnp.int32, (1, rows), 1)
    head = lax.broadcasted_iota(jnp.int32, (XATTN_HEAD_ROWS, 1), 0)
    mine = (col % X_HEADS) == head
    for s in range(XATTN_SEQS_PER_STEP):
        span = slice(s * rows, (s + 1) * rows)
        sc = jnp.where(mine, _dot_nt(q_ref[s], k_ref[span, :].astype(BF16)) * scale, NEG)
        p = jnp.exp(sc - jnp.max(sc, axis=-1, keepdims=True))
        l = jnp.sum(p, axis=-1, keepdims=True)
        o_ref[s] = (_dot(p.astype(BF16), v_ref[span, :].astype(BF16)) / l).astype(o_ref.dtype)


def _cross_attention_token(q, mem_k, mem_v):
    b = q.shape[0]
    assert b % XATTN_SEQS_PER_STEP == 0
    rows = MEM_LEN * X_HEADS
    qh = jnp.pad(q.reshape(b, X_HEADS, X_HEAD_DIM), ((0, 0), (0, XATTN_HEAD_ROWS - X_HEADS), (0, 0)))
    q_spec = pl.BlockSpec((XATTN_SEQS_PER_STEP, XATTN_HEAD_ROWS, X_HEAD_DIM), lambda i: (i, 0, 0))
    kv_spec = pl.BlockSpec((XATTN_SEQS_PER_STEP * rows, X_HEAD_DIM), lambda i: (i, 0))
    o = pl.pallas_call(
        _xattn_token_kernel,
        out_shape=jax.ShapeDtypeStruct((b, XATTN_HEAD_ROWS, X_HEAD_DIM), BF16),
        grid=(b // XATTN_SEQS_PER_STEP,),
        in_specs=[q_spec, kv_spec, kv_spec],
        out_specs=q_spec,
        compiler_params=_cparams(("parallel",)),
        name="cross_attention_token",
    )(qh, mem_k.reshape(b * rows, X_HEAD_DIM), mem_v.reshape(b * rows, X_HEAD_DIM))
    return o[:, :X_HEADS].reshape(b, X_W)


def _wkv_chunk_kernel(za_ref, sp_ref, s0_ref, mu_ref, w0_ref, a0_ref, kk_ref, ka_ref, rk_ref,
                      lng_ref, lnb_ref, w2_ref, a2_ref, g2_ref, seg_ref, segt_ref,
                      ya_ref, shift_ref, s_ref,
                      r_ref, v_ref, kt_ref, kn_ref, b_ref, lw_ref, cum_ref, y_ref, *, tt, c, t_valid):
    c2x = 2 * c
    ti = pl.program_id(1)
    pairs = range(A_HEADS // 2)
    cols = [slice(p * LANES, (p + 1) * LANES) for p in pairs]
    bf = lambda x: x.astype(BF16)

    @pl.when(ti == 0)
    def _():
        s_ref[...] = s0_ref[...]
        shift_ref[...] = sp_ref[...]

    seg = seg_ref[...]
    segt = segt_ref[...]

    def head_sum(x):
        return _dot_exact_rhs(_dot_exact_rhs(x, seg), segt)

    za = za_ref[...].astype(F32)
    row = lax.broadcasted_iota(jnp.int32, (tt, 1), 0)
    prev = jnp.where(row == 0, shift_ref[...], pltpu.roll(za, 1, axis=0))
    last = (t_valid - 1) % tt
    shift_ref[...] = za[last:last + 1, :]
    mix = za + (prev - za) * mu_ref[...]
    valid = (ti * tt + row) < t_valid
    r = mix[:, 0:A_WIDTH]
    k = mix[:, A_WIDTH:2 * A_WIDTH]
    v = jnp.where(valid, mix[:, 2 * A_WIDTH:3 * A_WIDTH], 0.0)
    xw = w0_ref[...] + _dot(bf(jnp.tanh(mix[:, A_OFF_W:A_OFF_A])), w2_ref[...])
    lw = jnp.where(valid, -_sigmoid(xw) * math.exp(-0.5), 0.0)
    a = _sigmoid(a0_ref[...] + _dot(bf(mix[:, A_OFF_A:A_OFF_G]), a2_ref[...]))
    g = _dot(bf(_sigmoid(mix[:, A_OFF_G:A_PAD])), g2_ref[...])
    kk = k * kk_ref[...]
    kn = jnp.where(valid, kk * lax.rsqrt(jnp.maximum(head_sum(kk * kk), 1e-24)), 0.0)
    kt = jnp.where(valid, k * (1.0 + (a - 1.0) * ka_ref[...]), 0.0)
    bonus = head_sum(r * kt * rk_ref[...]) * v
    ri = lax.broadcasted_iota(jnp.int32, (tt, tt), 0)
    ci = lax.broadcasted_iota(jnp.int32, (tt, tt), 1)
    same = (ri // c) == (ci // c)
    cum_ref[...] = _dot_exact_lhs(jnp.where(same & (ci <= ri), 1.0, 0.0).astype(BF16), lw)
    r_ref[...] = r
    v_ref[...] = v
    kt_ref[...] = kt
    kn_ref[...] = kn
    b_ref[...] = kn * a
    lw_ref[...] = lw

    lane = lax.broadcasted_iota(jnp.int32, (1, LANES), 1)
    m0 = (lane < A_HEAD).astype(F32)
    m1 = 1.0 - m0
    r2 = lax.broadcasted_iota(jnp.int32, (c2x, c2x), 0)
    c2 = lax.broadcasted_iota(jnp.int32, (c2x, c2x), 1)
    strict = (r2 % c) > (c2 % c)
    incl = (r2 % c) >= (c2 % c)

    def stack(x):
        return jnp.concatenate([x * m0, x * m1], axis=0)

    def prepare(ch):
        rows = slice(ch * c, (ch + 1) * c)
        cum = cum_ref[rows, :]
        tot = cum_ref[(ch + 1) * c - 1:(ch + 1) * c, :]
        kt_, b_ = kt_ref[rows, :], b_ref[rows, :]
        p_inv = jnp.exp(-cum)
        p_end = jnp.exp(tot - cum)
        return dict(v=v_ref[rows, :], rh=r_ref[rows, :] * jnp.exp(cum), kap=kn_ref[rows, :] * jnp.exp(cum - lw_ref[rows, :]),
                    kh=kt_ * p_inv, bh=b_ * p_inv, khp=kt_ * p_end, bhp=b_ * p_end, pt=jnp.exp(tot))

    def solve(d):
        vst = [stack(d["v"][:, cols[p]]) for p in pairs]
        s0 = [s_ref[p] for p in pairs]
        lhs = [bf(jnp.concatenate([stack(d["kap"][:, cols[p]]), stack(d["rh"][:, cols[p]])], axis=0)) for p in pairs]
        gram = [_dot_nt(lhs[p], bf(jnp.concatenate([stack(d["kh"][:, cols[p]]), stack(d["bh"][:, cols[p]])], axis=0)))
                for p in pairs]
        from_s0 = [_dot_nt(lhs[p], bf(s0[p])) for p in pairs]
        x = [from_s0[p][:c2x] + _dot(bf(jnp.where(strict, gram[p][:c2x, :c2x], 0.0)), bf(vst[p])) for p in pairs]
        lp = [jnp.where(strict, gram[p][:c2x, c2x:], 0.0) for p in pairs]
        sq = [_dot(bf(lp[p]), bf(jnp.concatenate([lp[p], x[p]], axis=1))) for p in pairs]
        lp = [sq[p][:, :c2x] for p in pairs]
        x = [x[p] - sq[p][:, c2x:] for p in pairs]
        n = 2
        while 2 * n < c:
            sq = [_dot(bf(lp[p]), bf(jnp.concatenate([lp[p], x[p]], axis=1))) for p in pairs]
            lp = [sq[p][:, :c2x] for p in pairs]
            x = [x[p] + sq[p][:, c2x:] for p in pairs]
            n *= 2
        x = [x[p] + _dot(bf(lp[p]), bf(x[p])) for p in pairs]
        ys = []
        for p in pairs:
            tri = jnp.concatenate([jnp.where(incl, gram[p][c2x:, :c2x], 0.0),
                                   jnp.where(incl, -gram[p][c2x:, c2x:], 0.0)], axis=1)
            y = from_s0[p][c2x:] + _dot(bf(tri), bf(jnp.concatenate([vst[p], x[p]], axis=0)))
            ys.append(y[:c] + y[c:])
            decayed = jnp.concatenate([stack(d["khp"][:, cols[p]]), stack(d["bhp"][:, cols[p]])], axis=0)
            s_ref[p] = s0[p] * d["pt"][:, cols[p]] + _dot_tn(bf(jnp.concatenate([vst[p], -x[p]], axis=0)), bf(decayed))
        return jnp.concatenate(ys, axis=1)

    n_chunks = tt // c
    d = prepare(0)
    for ch in range(n_chunks):
        d_next = prepare(ch + 1) if ch + 1 < n_chunks else None
        y_ref[ch * c:(ch + 1) * c, :] = solve(d)
        d = d_next

    y = y_ref[...]
    mean = head_sum(y) * (1.0 / A_HEAD)
    yc = y - mean
    var = head_sum(yc * yc) * (1.0 / A_HEAD)
    yn = yc * lax.rsqrt(var + GN_EPS) * lng_ref[...] + lnb_ref[...]
    ya_ref[...] = ((yn + bonus) * g).astype(ya_ref.dtype)


def _wkv_mix(za, shift_prev, s0, prm, *, nb, t_pad, t_valid, tt, chunk):
    nt = t_pad // tt
    assert t_pad % tt == 0 and tt % chunk == 0 and chunk % SUBLANES == 0
    vec = lambda n: pl.BlockSpec((1, n), lambda i, j: (0, 0))
    full = lambda a: pl.BlockSpec(a.shape, lambda i, j: (0,) * a.ndim)
    state_spec = pl.BlockSpec((None, A_HEADS // 2, LANES, LANES), lambda i, j: (i, 0, 0, 0))
    shift_spec = pl.BlockSpec((None, 1, A_PAD), lambda i, j: (i, 0, 0))
    vecs = [prm[n] for n in ("mu", "w0", "a0", "k_k", "k_a", "r_k", "lnx_g", "lnx_b")]
    mats = [prm[n] for n in ("w2", "a2", "g2", "seg", "segt")]
    return pl.pallas_call(
        functools.partial(_wkv_chunk_kernel, tt=tt, c=chunk, t_valid=t_valid),
        out_shape=(jax.ShapeDtypeStruct((nb * t_pad, A_WIDTH), BF16),
                   jax.ShapeDtypeStruct((nb, 1, A_PAD), F32),
                   jax.ShapeDtypeStruct((nb, A_HEADS // 2, LANES, LANES), F32)),
        grid=(nb, nt),
        in_specs=[pl.BlockSpec((tt, A_PAD), lambda i, j: (i * nt + j, 0)), shift_spec, state_spec]
                 + [vec(x.shape[1]) for x in vecs] + [full(x) for x in mats],
        out_specs=(pl.BlockSpec((tt, A_WIDTH), lambda i, j: (i * nt + j, 0)), shift_spec, state_spec),
        scratch_shapes=[pltpu.VMEM((tt, A_WIDTH), F32)] * 8,
        compiler_params=_cparams(("parallel", "arbitrary")),
        name="wkv_mix",
    )(za, shift_prev, s0, *vecs, *mats)


CONV_HALO = 32


def _conv_kernel(zb_ref, cb_ref, w_ref, b_ref, g_ref, be_ref, o_ref, st_ref, ext_ref, conv_ref,
                 *, tt, t_last):
    ti = pl.program_id(1)
    pad = CONV_HALO - (CONV_W - 1)

    @pl.when(ti == 0)
    def _():
        ext_ref[0:CONV_HALO, :] = cb_ref[...]

    zb = zb_ref[...].astype(F32)
    ext_ref[CONV_HALO:CONV_HALO + tt, :] = zb[:, :B_WIDTH] * _sigmoid(zb[:, B_WIDTH:])
    rb = min(tt, 64)
    for r0 in range(0, tt, rb):
        for c0 in range(0, B_WIDTH, LANES):
            cols = slice(c0, c0 + LANES)
            x = ext_ref[r0:r0 + rb + CONV_HALO, cols]
            n = rb + CONV_HALO
            shifted = [x] + [pltpu.roll(x, n - s, axis=0) for s in range(1, SUBLANES)]
            acc = jnp.broadcast_to(b_ref[:, cols], (rb, LANES))
            for j in range(CONV_W):
                s = (pad + j) % SUBLANES
                base = pad + j - s
                acc = acc + shifted[s][base:base + rb] * w_ref[j:j + 1, cols]
            conv_ref[r0:r0 + rb, cols] = acc
    conv = conv_ref[...]
    mean = jnp.mean(conv, axis=-1, keepdims=True)
    xc = conv - mean
    var = jnp.mean(xc * xc, axis=-1, keepdims=True)
    y = xc * lax.rsqrt(var + LN_EPS) * g_ref[...] + be_ref[...]
    o_ref[...] = (y * _sigmoid(y)).astype(o_ref.dtype)
    st_ref[...] = ext_ref[t_last + pad:t_last + pad + CONV_W - 1, :]
    if tt >= CONV_HALO:
        ext_ref[0:CONV_HALO, :] = ext_ref[tt:tt + CONV_HALO, :]


def _conv_module(zb, conv_buf, prm, *, nb, t_pad, t_valid, tt):
    nt = t_pad // tt
    assert nt == 1 or tt >= CONV_HALO
    vec = pl.BlockSpec((1, B_WIDTH), lambda i, j: (0, 0))
    return pl.pallas_call(
        functools.partial(_conv_kernel, tt=tt, t_last=t_valid - (nt - 1) * tt),
        out_shape=(jax.ShapeDtypeStruct((nb * t_pad, B_WIDTH), BF16),
                   jax.ShapeDtypeStruct((nb, CONV_W - 1, B_WIDTH), F32)),
        grid=(nb, nt),
        in_specs=[pl.BlockSpec((tt, 2 * B_WIDTH), lambda i, j: (i * nt + j, 0)),
                  pl.BlockSpec((None, CONV_HALO, B_WIDTH), lambda i, j: (i, 0, 0)),
                  pl.BlockSpec((CONV_W, B_WIDTH), lambda i, j: (0, 0)), vec, vec, vec],
        out_specs=(pl.BlockSpec((tt, B_WIDTH), lambda i, j: (i * nt + j, 0)),
                   pl.BlockSpec((None, CONV_W - 1, B_WIDTH), lambda i, j: (i, 0, 0))),
        scratch_shapes=[pltpu.VMEM((CONV_HALO + tt, B_WIDTH), F32), pltpu.VMEM((tt, B_WIDTH), F32)],
        compiler_params=_cparams(("parallel", "arbitrary")),
        name="conv_module",
    )(zb, conv_buf, prm["dw_w"], prm["dw_b"], prm["cn_g"], prm["cn_b"])


def _top_blocks(gate, blk, n_sel):
    sel = []
    for _ in range(n_sel):
        m = jnp.max(gate, axis=0, keepdims=True)
        idx = jnp.min(jnp.where(gate == m, blk, gate.shape[0]), axis=0, keepdims=True)
        sel.append(idx)
        gate = jnp.where(blk == idx, -jnp.inf, gate)
    return sel


def _moba_prompt_kernel(q_ref, k_ref, v_ref, o_ref, kb_ref, vt_ref, km_ref, qf_ref, *, t):
    kvh = pl.program_id(1)
    qi = pl.program_id(2)
    nb = t // MOBA_BLOCK
    nq = C_GROUP * Q_BLOCK

    log2e = 1.0 / math.log(2.0)
    n_feat = 3

    def slope2_of(group):
        head = (kvh * C_GROUP + group + 1).astype(F32)
        return jnp.exp(head * (-(8.0 / C_HEADS) * math.log(2.0))) * log2e

    @pl.when(qi == 0)
    def _():
        srow = slope2_of(lax.broadcasted_iota(jnp.int32, (nq, 1), 0) // Q_BLOCK)
        s_hi = srow.astype(BF16).astype(F32)
        s_mid = (srow - s_hi).astype(BF16).astype(F32)
        s_lo = (srow - s_hi - s_mid).astype(BF16).astype(F32)
        part = lax.broadcasted_iota(jnp.int32, (1, LANES), 1)
        qfeat = jnp.where(part % n_feat == 0, s_hi, jnp.where(part % n_feat == 1, s_mid, s_lo))
        qf_ref[...] = jnp.where(part < 2 * n_feat, qfeat, 0.0).astype(BF16)
        km_ref[...] = jnp.zeros_like(km_ref)
        lane = lax.broadcasted_iota(jnp.int32, (MOBA_BLOCK, LANES), 1)
        offset = lax.broadcasted_iota(jnp.int32, (MOBA_BLOCK, LANES), 0).astype(F32)
        for n in range(nb):
            kblk = k_ref[n * MOBA_BLOCK:(n + 1) * MOBA_BLOCK, :]
            feat = jnp.where(lane < n_feat, float(n * MOBA_BLOCK), jnp.where(lane < 2 * n_feat, offset, 0.0))
            kb_ref[n] = jnp.concatenate([kblk, feat], axis=1).astype(BF16)
            vt_ref[n] = v_ref[n * MOBA_BLOCK:(n + 1) * MOBA_BLOCK, :].T.astype(BF16)
            km_ref[n:n + 1, :] = jnp.mean(kblk, axis=0, keepdims=True)

    q = q_ref[...]
    qs = jnp.concatenate([q[:, g * C_HEAD_DIM:(g + 1) * C_HEAD_DIM] for g in range(C_GROUP)], axis=0)
    own = (qi * Q_BLOCK) // MOBA_BLOCK
    blk = lax.broadcasted_iota(jnp.int32, (km_ref.shape[0], 1), 0)
    gate = _dot_x3(_dot_nt, km_ref[...], qs)
    gate = jnp.where(blk < own, gate, jnp.where(blk < nb, NEG, -jnp.inf))
    sel = _top_blocks(gate, blk, MOBA_TOPK)
    sel = [jnp.where(own > r, sel[r], -1) for r in range(MOBA_TOPK)]

    col = lax.broadcasted_iota(jnp.int32, (1, nq), 1)
    pos = qi * Q_BLOCK + col % Q_BLOCK
    base = -slope2_of(col // Q_BLOCK) * pos.astype(F32)
    qb = jnp.concatenate([(qs * (log2e / math.sqrt(C_HEAD_DIM))).astype(BF16), qf_ref[...]], axis=1)
    trow = lax.broadcasted_iota(jnp.int32, (MOBA_BLOCK, 1), 0)

    def attend_upto(own_static):
        scores = []
        for n in range(own_static):
            picked = (sel[0] == n) | (sel[1] == n) | (sel[2] == n)
            scores.append(_dot_nt(kb_ref[n], qb) + jnp.where(picked, base, NEG))
        causal = trow <= col % Q_BLOCK
        scores.append(jnp.where(causal, _dot_nt(kb_ref[own_static], qb) + base, NEG))
        m = scores[-1].max(axis=0, keepdims=True)
        for s in scores[:-1]:
            m = jnp.maximum(m, jnp.max(s, axis=0, keepdims=True))
        l = jnp.zeros((1, nq), F32)
        acc = jnp.zeros((C_HEAD_DIM, nq), F32)
        for n, s in enumerate(scores):
            p = jnp.exp2(s - m)
            l = l + jnp.sum(p, axis=0, keepdims=True)
            acc = acc + _dot(vt_ref[n], p.astype(BF16))
        o = acc / l
        for g in range(C_GROUP):
            cols = slice(g * Q_BLOCK, (g + 1) * Q_BLOCK)
            o_ref[:, g * C_HEAD_DIM:(g + 1) * C_HEAD_DIM] = o[:, cols].T.astype(o_ref.dtype)

    for own_static in range(nb):
        pl.when(own == own_static)(functools.partial(attend_upto, own_static))


def _moba_prompt(z, *, nb, t):
    nq = t // Q_BLOCK
    assert t % Q_BLOCK == 0 and t // MOBA_BLOCK <= 256 and MOBA_BLOCK <= 256
    kcol = C_Q_W // C_HEAD_DIM
    vcol = (C_Q_W + C_KV_W) // C_HEAD_DIM
    return pl.pallas_call(
        functools.partial(_moba_prompt_kernel, t=t),
        out_shape=jax.ShapeDtypeStruct((nb * t, C_Q_W), BF16),
        grid=(nb, C_KV_HEADS, nq),
        in_specs=[pl.BlockSpec((Q_BLOCK, C_GROUP * C_HEAD_DIM), lambda b, h, i: (b * nq + i, h)),
                  pl.BlockSpec((t, C_HEAD_DIM), lambda b, h, i: (b, kcol + h)),
                  pl.BlockSpec((t, C_HEAD_DIM), lambda b, h, i: (b, vcol + h))],
        out_specs=pl.BlockSpec((Q_BLOCK, C_GROUP * C_HEAD_DIM), lambda b, h, i: (b * nq + i, h)),
        scratch_shapes=[pltpu.VMEM((t // MOBA_BLOCK, MOBA_BLOCK, C_HEAD_DIM + LANES), BF16),
                        pltpu.VMEM((t // MOBA_BLOCK, C_HEAD_DIM, MOBA_BLOCK), BF16),
                        pltpu.VMEM((-(-t // MOBA_BLOCK // SUBLANES) * SUBLANES, C_HEAD_DIM), F32),
                        pltpu.VMEM((C_GROUP * Q_BLOCK, LANES), BF16)],
        compiler_params=_cparams(("parallel", "parallel", "arbitrary")),
        name="moba_prompt",
    )(z, z, z)


PAGES_PER_STEP = 8 * MOBA_BLOCK // PAGE_SIZE


def _moba_decode_kernel(pt_ref, z_ref, *refs, n_blocks):
    del pt_ref
    npg = PAGES_PER_STEP
    k_refs, v_refs = refs[:npg], refs[npg:2 * npg]
    o_ref = refs[2 * npg]
    m_ref, l_ref, acc_ref, gate_ref = refs[2 * npg + 1:]
    j = pl.program_id(1)
    ppb = MOBA_BLOCK // PAGE_SIZE
    past = n_blocks * MOBA_BLOCK
    scale = 1.0 / math.sqrt(C_HEAD_DIM)
    hrow = lax.broadcasted_iota(jnp.int32, (C_HEADS, 1), 0)
    slope = jnp.exp((hrow + 1).astype(F32) * (-(8.0 / C_HEADS) * math.log(2.0)))

    z = z_ref[...]
    qrows = jnp.concatenate([z[:, h * C_HEAD_DIM:(h + 1) * C_HEAD_DIM] for h in range(C_HEADS)], axis=0)
    qb = qrows.astype(BF16)
    rows_blk = MOBA_BLOCK * C_KV_HEADS
    col = lax.broadcasted_iota(jnp.int32, (1, rows_blk), 1)
    mine = (col % C_KV_HEADS) == (hrow // C_GROUP)
    tab = jnp.where(mine, slope * (col // C_KV_HEADS).astype(F32), NEG)

    for blk in range(npg // ppb):
        n = j * (npg // ppb) + blk
        kblk = jnp.concatenate([k_refs[blk * ppb + i][...] for i in range(ppb)], axis=0)
        vblk = jnp.concatenate([v_refs[blk * ppb + i][...] for i in range(ppb)], axis=0)
        s = _dot_nt(qb, kblk.astype(BF16)) * scale + tab - slope * (past - n * MOBA_BLOCK).astype(F32)
        m = jnp.max(s, axis=-1, keepdims=True)
        p = jnp.exp(s - m)
        m_ref[pl.ds(n, 1)] = m[None]
        l_ref[pl.ds(n, 1)] = jnp.sum(p, axis=-1, keepdims=True)[None]
        acc_ref[pl.ds(n, 1)] = _dot(p.astype(BF16), vblk.astype(BF16))[None]
        ksum = jnp.sum(kblk.reshape(rows_blk // SUBLANES, SUBLANES, C_HEAD_DIM), axis=0)
        kmean = (ksum[:C_KV_HEADS] + ksum[C_KV_HEADS:]) * (1.0 / MOBA_BLOCK)
        kmrows = jnp.concatenate([jnp.broadcast_to(kmean[kvh:kvh + 1], (C_GROUP, C_HEAD_DIM))
                                  for kvh in range(C_KV_HEADS)], axis=0)
        gate_ref[pl.ds(n, 1)] = jnp.sum(qrows * kmrows, axis=-1, keepdims=True)[None]

    @pl.when(j == pl.num_programs(1) - 1)
    def _():
        gates = gate_ref[...]
        ms = m_ref[...]
        bidx = lax.broadcasted_iota(jnp.int32, gates.shape, 0)
        picked = jnp.zeros(gates.shape, jnp.bool_)
        for _ in range(min(MOBA_TOPK, n_blocks)):
            gmax = jnp.max(gates, axis=0, keepdims=True)
            first = jnp.min(jnp.where(gates == gmax, bidx, n_blocks), axis=0, keepdims=True)
            hit = bidx == first
            picked = picked | hit
            gates = jnp.where(hit, -jnp.inf, gates)
        knew = jnp.concatenate(
            [z[:, C_Q_W + kvh * C_HEAD_DIM:C_Q_W + (kvh + 1) * C_HEAD_DIM] for kvh in range(C_KV_HEADS)
             for _ in range(C_GROUP)], axis=0)
        vnew = jnp.concatenate(
            [z[:, C_Q_W + C_KV_W + kvh * C_HEAD_DIM:C_Q_W + C_KV_W + (kvh + 1) * C_HEAD_DIM]
             for kvh in range(C_KV_HEADS) for _ in range(C_GROUP)], axis=0)
        s_own = jnp.sum(qb.astype(F32) * knew.astype(BF16).astype(F32), axis=-1, keepdims=True) * scale
        mtot = jnp.maximum(jnp.max(jnp.where(picked, ms, NEG), axis=0), s_own)
        wgt = jnp.where(picked, jnp.exp(ms - mtot[None]), 0.0)
        p_own = jnp.exp(s_own - mtot)
        l = jnp.sum(wgt * l_ref[...], axis=0) + p_own
        acc = jnp.sum(wgt * acc_ref[...], axis=0) + p_own.astype(BF16).astype(F32) * vnew.astype(BF16).astype(F32)
        o_ref[...] = (acc / l).astype(o_ref.dtype)


def _moba_decode(z, cache_k, cache_v, page_table):
    nb, n_pages = page_table.shape
    assert n_pages % PAGES_PER_STEP == 0 and 2 * C_KV_HEADS == SUBLANES
    n_blocks = n_pages * PAGE_SIZE // MOBA_BLOCK
    steps = n_pages // PAGES_PER_STEP

    def page_spec(i):
        return pl.BlockSpec((None, PAGE_SIZE * C_KV_HEADS, C_HEAD_DIM),
                            lambda b, j, pt: (pt[b, j * PAGES_PER_STEP + i], 0, 0))

    pages = [page_spec(i) for i in range(PAGES_PER_STEP)]
    zw = z.shape[-1]
    return pl.pallas_call(
        functools.partial(_moba_decode_kernel, n_blocks=n_blocks),
        out_shape=jax.ShapeDtypeStruct((nb, C_HEADS, C_HEAD_DIM), BF16),
        grid_spec=pltpu.PrefetchScalarGridSpec(
            num_scalar_prefetch=1,
            grid=(nb, steps),
            in_specs=[pl.BlockSpec((None, 1, zw), lambda b, j, pt: (b, 0, 0))] + pages + pages,
            out_specs=pl.BlockSpec((None, C_HEADS, C_HEAD_DIM), lambda b, j, pt: (b, 0, 0)),
            scratch_shapes=[pltpu.VMEM((n_blocks, C_HEADS, 1), F32), pltpu.VMEM((n_blocks, C_HEADS, 1), F32),
                            pltpu.VMEM((n_blocks, C_HEADS, C_HEAD_DIM), F32),
                            pltpu.VMEM((n_blocks, C_HEADS, 1), F32)]),
        compiler_params=_cparams(("parallel", "arbitrary")),
        name="moba_decode",
    )(page_table, z, *([cache_k] * PAGES_PER_STEP), *([cache_v] * PAGES_PER_STEP))


def _pad_a_cols(x):
    i3 = 3 * A_WIDTH
    z = lambda n: jnp.zeros(x.shape[:-1] + (n,), x.dtype)
    return jnp.concatenate([
        x[..., :i3],
        x[..., i3:i3 + DECAY_RANK], z(LANES - DECAY_RANK),
        x[..., i3 + DECAY_RANK:i3 + DECAY_RANK + AAA_RANK], z(LANES - AAA_RANK),
        x[..., i3 + DECAY_RANK + AAA_RANK:], z(2 * LANES - GATE_RANK)], axis=-1)


def _unpad_a_cols(x):
    return jnp.concatenate([x[..., :A_OFF_W + DECAY_RANK], x[..., A_OFF_A:A_OFF_A + AAA_RANK],
                            x[..., A_OFF_G:A_OFF_G + GATE_RANK]], axis=-1)


def _pad_rows(w, n):
    return jnp.concatenate([w, jnp.zeros((n - w.shape[0],) + w.shape[1:], w.dtype)], axis=0)


def _ab_params(i, w_in_ab, mu_ab, w0, w2, a0, a2, g2, k_k, k_a, r_k, lnx_g, lnx_b, dw_w, dw_b, cn_g, cn_b):
    row = lambda x: x.reshape(1, -1)
    head = jnp.arange(A_WIDTH)[:, None] // A_HEAD == jnp.arange(LANES)[None, :]
    return {
        "w_a": _pad_a_cols(w_in_ab[i][:, :A_COLS]).astype(BF16),
        "w_b": w_in_ab[i][:, A_COLS:].astype(BF16),
        "mu": row(_pad_a_cols(mu_ab[i])),
        "w0": row(w0[i]), "a0": row(a0[i]), "k_k": row(k_k[i]), "k_a": row(k_a[i]), "r_k": row(r_k[i]),
        "lnx_g": row(lnx_g[i]), "lnx_b": row(lnx_b[i]),
        "w2": _pad_rows(w2[i], LANES).astype(BF16), "a2": _pad_rows(a2[i], LANES).astype(BF16),
        "g2": _pad_rows(g2[i], 2 * LANES).astype(BF16),
        "seg": head.astype(BF16), "segt": head.T.astype(BF16),
        "dw_w": dw_w[i], "dw_b": row(dw_b[i]), "cn_g": row(cn_g[i]), "cn_b": row(cn_b[i]),
    }


def _state_to_pairs(s):
    nb = s.shape[0]
    s = s.reshape(nb, A_HEADS // 2, 2, A_HEAD, A_HEAD)
    z = jnp.zeros_like(s[:, :, 0])
    top = jnp.concatenate([s[:, :, 0], z], axis=-1)
    bot = jnp.concatenate([z, s[:, :, 1]], axis=-1)
    return jnp.concatenate([top, bot], axis=-2)


def _pairs_to_state(s):
    nb = s.shape[0]
    return jnp.stack([s[:, :, :A_HEAD, :A_HEAD], s[:, :, A_HEAD:, A_HEAD:]], axis=2).reshape(
        nb, A_HEADS, A_HEAD, A_HEAD)


def _mixer_ab(x, g_pre, shift_prev, wkv0, conv_buf, prm, *, nb, t):
    if t >= WKV_CHUNK:
        assert t % WKV_CHUNK == 0
        t_pad, chunk, tt_a, tt_b, xp = t, WKV_CHUNK, min(t, 256), min(t, 128), x
    else:
        t_pad = chunk = tt_a = tt_b = -(-t // WKV_SHORT_CHUNK) * WKV_SHORT_CHUNK
        xp = jnp.pad(x.reshape(nb, t, -1), ((0, 0), (0, t_pad - t), (0, 0))).reshape(nb * t_pad, -1)
    za = _norm_matmul(xp, g_pre, prm["w_a"], BF16, tn=512)
    zb = _norm_matmul(xp, g_pre, prm["w_b"], BF16, tn=512)
    ya, shift, s_new = _wkv_mix(za, _pad_a_cols(shift_prev)[:, None, :], _state_to_pairs(wkv0), prm,
                                nb=nb, t_pad=t_pad, t_valid=t, tt=tt_a, chunk=chunk)
    hist = jnp.pad(conv_buf, ((0, 0), (CONV_HALO - (CONV_W - 1), 0), (0, 0)))
    yb, conv_new = _conv_module(zb, hist, prm, nb=nb, t_pad=t_pad, t_valid=t, tt=tt_b)
    y = jnp.concatenate([ya, yb], axis=-1)
    if t_pad != t:
        y = y.reshape(nb, t_pad, -1)[:, :t].reshape(nb * t, -1)
    return y, _unpad_a_cols(shift[:, 0]), _pairs_to_state(s_new), conv_new


def _rest_of_layer(x, l, k_arr, v_arr, k_col, v_col, wts, *, nb, t):
    (g_x_pre, g_x_post, w_xq, w_xo, g_ffn_pre, g_ffn_post, w_gu, w_down) = wts
    q = _norm_matmul(x, g_x_pre[l], w_xq, BF16, tn=X_W)
    if k_arr.ndim == 4:
        assert t == 1
        o = _cross_attention_token(q, k_arr, v_arr)
    else:
        tq = min(t, 512)
        assert t % tq == 0 and tq % BF16_ROWS == 0
        o = _cross_attention(q.reshape(nb, t, X_W), k_arr, v_arr, k_col, v_col, tq=tq).reshape(nb * t, X_W)
    x = _matmul_norm_res(o, w_xo, g_x_post[l], x)
    h = _norm_matmul(x, g_ffn_pre[l], w_gu, BF16, tn=512, swiglu_half=D_FF)
    return _matmul_norm_res(h, w_down, g_ffn_post[l], x)


def kernel(x_prompt, x_sample, cache_k, cache_v, state_wkv, state_shift, state_conv, cache_mem_k, cache_mem_v, page_table, mem_prompt, w_in_ab, w_out_ab, mu_ab, w0, w2, a0, a2, g2, k_k, k_a, r_k, lnx_g, lnx_b, dw_w, dw_b, cn_g, cn_b, w_in_c, w_out_c, g_mix_pre, g_mix_post, g_x_pre, g_x_post, g_mem, w_xq, w_xkv, w_xo, g_ffn_pre, g_ffn_post, w_ffn_gu, w_ffn_down):
    bp, tp, d = x_prompt.shape
    bs, ts, _ = x_sample.shape
    depth = g_mix_pre.shape[0]
    xp = x_prompt.reshape(bp * tp, d)
    xs = x_sample.reshape(bs * ts, d)
    mem = mem_prompt.reshape(bp * MEM_LEN, d)
    outs = {n: [] for n in ("kp", "vp", "wkvp", "shp", "cvp", "mkp", "mvp", "ks", "vs", "wkvs", "shs", "cvs")}
    for l in range(depth):
        i = l // 2
        if l % 2 == 0:
            prm = _ab_params(i, w_in_ab, mu_ab, w0, w2, a0, a2, g2, k_k, k_a, r_k, lnx_g, lnx_b,
                             dw_w, dw_b, cn_g, cn_b)
            w_out = w_out_ab[i].astype(BF16)
            fp, shp, wkvp, cvp = _mixer_ab(
                xp, g_mix_pre[l], jnp.zeros((bp, A_COLS), F32), jnp.zeros((bp, A_HEADS, A_HEAD, A_HEAD), F32),
                jnp.zeros((bp, CONV_W - 1, B_WIDTH), F32), prm, nb=bp, t=tp)
            fs, shs, wkvs, cvs = _mixer_ab(xs, g_mix_pre[l], state_shift[i], state_wkv[i], state_conv[i], prm,
                                           nb=bs, t=ts)
            for n, val in (("shp", shp), ("wkvp", wkvp), ("cvp", cvp), ("shs", shs), ("wkvs", wkvs), ("cvs", cvs)):
                outs[n].append(val)
        else:
            w_in = w_in_c[i].astype(BF16)
            w_out = w_out_c[i].astype(BF16)
            zp = _norm_matmul(xp, g_mix_pre[l], w_in, F32, tn=512)
            fp = _moba_prompt(zp, nb=bp, t=tp)
            zs = _norm_matmul(xs, g_mix_pre[l], w_in, F32, tn=512)
            n_phys = cache_k.shape[1]
            page_rows = (-1, PAGE_SIZE * C_KV_HEADS, C_HEAD_DIM)
            fs = _moba_decode(zs.reshape(bs, ts, -1), cache_k.reshape(page_rows), cache_v.reshape(page_rows),
                              page_table + i * n_phys).reshape(bs * ts, C_Q_W)
            kv = lambda z, nb, t, off: z[:, off:off + C_KV_W].reshape(nb, t, C_KV_HEADS, C_HEAD_DIM)
            outs["kp"].append(kv(zp, bp, tp, C_Q_W))
            outs["vp"].append(kv(zp, bp, tp, C_Q_W + C_KV_W))
            outs["ks"].append(kv(zs, bs, ts, C_Q_W))
            outs["vs"].append(kv(zs, bs, ts, C_Q_W + C_KV_W))
        xp = _matmul_norm_res(fp, w_out, g_mix_post[l], xp)
        xs = _matmul_norm_res(fs, w_out, g_mix_post[l], xs)
        mkv = _norm_matmul(mem, g_mem[l], w_xkv[l].astype(BF16), F32, tn=512)
        outs["mkp"].append(mkv[:, :X_W].reshape(bp, MEM_LEN, X_HEADS, X_HEAD_DIM))
        outs["mvp"].append(mkv[:, X_W:].reshape(bp, MEM_LEN, X_HEADS, X_HEAD_DIM))
        wts = (g_x_pre, g_x_post, w_xq[l].astype(BF16), w_xo[l].astype(BF16), g_ffn_pre, g_ffn_post,
               w_ffn_gu[l].astype(BF16), w_ffn_down[l].astype(BF16))
        xp = _rest_of_layer(xp, l, mkv, mkv, 0, 1, wts, nb=bp, t=tp)
        if ts == 1:
            mem_s = (cache_mem_k[l], cache_mem_v[l])
        else:
            mem_s = (cache_mem_k[l].reshape(bs * MEM_LEN, X_W), cache_mem_v[l].reshape(bs * MEM_LEN, X_W))
        xs = _rest_of_layer(xs, l, *mem_s, 0, 0, wts, nb=bs, t=ts)
    st = lambda n: jnp.stack(outs[n])
    return (xp.reshape(bp, tp, d), xs.reshape(bs, ts, d), st("kp"), st("vp"), st("wkvp"), st("shp"), st("cvp"),
            st("mkp"), st("mvp"), st("ks"), st("vs"), st("wkvs"), st("shs"), st("cvs"))
```

```python
import functools
import math

import jax
import jax.numpy as jnp
from jax import lax
from jax.experimental import pallas as pl
from jax.experimental.pallas import tpu as pltpu

F32 = jnp.float32
BF16 = jnp.bfloat16

D_MODEL = 2048
PAGE_SIZE = 128
A_WIDTH = 1024
A_HEAD = 64
A_HEADS = 16
DECAY_RANK = 64
AAA_RANK = 64
GATE_RANK = 160
A_COLS = 3 * A_WIDTH + DECAY_RANK + AAA_RANK + GATE_RANK
GN_EPS = 64e-5
B_WIDTH = 1024
CONV_W = 31
C_HEADS = 16
C_HEAD_DIM = 128
C_KV_HEADS = 4
C_GROUP = 4
C_Q_W = 2048
C_KV_W = 512
MOBA_BLOCK = 256
MOBA_TOPK = 3
Q_BLOCK = MOBA_BLOCK
MEM_LEN = 256
X_HEADS = 4
X_HEAD_DIM = 128
X_W = 512
D_FF = 5632
RMS_EPS = 1e-6
LN_EPS = 1e-5
NEG = -1e30

SUBLANES = 8
MXU_WIDTH = 256
BF16_ROWS = 16
LANES = 128
A_OFF_W = 3 * A_WIDTH
A_OFF_A = A_OFF_W + LANES
A_OFF_G = A_OFF_A + LANES
A_PAD = A_OFF_G + 2 * LANES
WKV_CHUNK = 64
WKV_SHORT_CHUNK = BF16_ROWS
VMEM_LIMIT = 56 * 1024 * 1024


def _cparams(sem):
    return pltpu.CompilerParams(dimension_semantics=sem, vmem_limit_bytes=VMEM_LIMIT)


def _dot(a, b):
    return jnp.dot(a, b, preferred_element_type=F32)


def _dot_nt(a, b):
    return lax.dot_general(a, b, (((1,), (1,)), ((), ())), preferred_element_type=F32)


def _dot_tn(a, b):
    return lax.dot_general(a, b, (((0,), (0,)), ((), ())), preferred_element_type=F32)


def _split2(x):
    hi = x.astype(BF16)
    lo = (x - hi.astype(F32)).astype(BF16)
    return hi, lo


def _dot_x3(dot, a, b):
    ah, al = _split2(a)
    bh, bl = _split2(b)
    return dot(ah, bh) + (dot(ah, bl) + dot(al, bh))


def _dot_exact_rhs(a, b_bf16):
    hi, lo = _split2(a)
    return _dot(hi, b_bf16) + _dot(lo, b_bf16)


def _dot_exact_lhs(a_bf16, b):
    hi, lo = _split2(b)
    return _dot(a_bf16, hi) + _dot(a_bf16, lo)


def _sigmoid(x):
    return 1.0 / (1.0 + jnp.exp(-x))


def _norm_swiglu_kernel(x_ref, g_ref, wg_ref, wu_ref, o_ref, xn_ref):
    x = x_ref[...]
    ms = jnp.mean(x * x, axis=-1, keepdims=True)
    xn_ref[...] = (x * lax.rsqrt(ms + RMS_EPS) * g_ref[...]).astype(BF16)
    for c0 in range(0, o_ref.shape[1], MXU_WIDTH):
        cs = slice(c0, c0 + MXU_WIDTH)
        gate = _dot(xn_ref[...], wg_ref[:, cs])
        o_ref[:, cs] = (gate * _sigmoid(gate) * _dot(xn_ref[...], wu_ref[:, cs])).astype(o_ref.dtype)


def _norm_mm_resident_kernel(x_ref, g_ref, w_ref, o_ref, xn_ref, *, tn):
    x = x_ref[...]
    ms = jnp.mean(x * x, axis=-1, keepdims=True)
    xn_ref[...] = (x * lax.rsqrt(ms + RMS_EPS) * g_ref[...]).astype(BF16)
    for c0 in range(0, o_ref.shape[1], tn):
        o_ref[:, c0:c0 + tn] = _dot(xn_ref[...], w_ref[:, c0:c0 + tn]).astype(o_ref.dtype)


RESIDENT_WEIGHT_BYTES = 16 * 1024 * 1024


def _norm_matmul(x, g, w, out_dtype, *, tn, swiglu_half=None):
    m, d = x.shape
    n = swiglu_half if swiglu_half else w.shape[1]
    if not swiglu_half and w.size * w.dtype.itemsize <= RESIDENT_WEIGHT_BYTES:
        tm = min(m, 512)
        assert m % tm == 0 and n % tn == 0
        return pl.pallas_call(
            functools.partial(_norm_mm_resident_kernel, tn=tn),
            out_shape=jax.ShapeDtypeStruct((m, n), out_dtype),
            grid=(m // tm,),
            in_specs=[pl.BlockSpec((tm, d), lambda i: (i, 0)),
                      pl.BlockSpec((1, d), lambda i: (0, 0)),
                      pl.BlockSpec((d, n), lambda i: (0, 0), pipeline_mode=pl.Buffered(1))],
            out_specs=pl.BlockSpec((tm, n), lambda i: (i, 0)),
            scratch_shapes=[pltpu.VMEM((tm, d), BF16)],
            compiler_params=_cparams(("parallel",)),
            name="norm_matmul_resident",
        )(x, g.reshape(1, d), w)
    assert swiglu_half, "a plain weight larger than RESIDENT_WEIGHT_BYTES is not supported"
    groups = next(k for k in range(1, n + 1) if n % k == 0 and (n // k) % MXU_WIDTH == 0
                  and 2 * d * (n // k) * w.dtype.itemsize <= 3 * RESIDENT_WEIGHT_BYTES // 2)
    tg = n // groups
    tm = min(m, 512)
    assert m % tm == 0
    w_spec = lambda off: pl.BlockSpec((d, tg), lambda j, i: (0, j + off), pipeline_mode=pl.Buffered(1))
    return pl.pallas_call(
        _norm_swiglu_kernel,
        out_shape=jax.ShapeDtypeStruct((m, n), out_dtype),
        grid=(groups, m // tm),
        in_specs=[pl.BlockSpec((tm, d), lambda j, i: (i, 0)),
                  pl.BlockSpec((1, d), lambda j, i: (0, 0)), w_spec(0), w_spec(groups)],
        out_specs=pl.BlockSpec((tm, tg), lambda j, i: (i, j)),
        scratch_shapes=[pltpu.VMEM((tm, d), BF16)],
        compiler_params=_cparams(("parallel", "parallel")),
        name="norm_swiglu",
    )(x, g.reshape(1, d), w, w)


def _mm_norm_res_kernel(x_ref, w_ref, g_ref, r_ref, o_ref):
    y = _dot(x_ref[...], w_ref[...])
    ms = jnp.mean(y * y, axis=-1, keepdims=True)
    o_ref[...] = r_ref[...] + y * lax.rsqrt(ms + RMS_EPS) * g_ref[...]


def _matmul_norm_res(x, w, g, res):
    m, kd = x.shape
    n = w.shape[1]
    tm = min(m, 512 if kd * n * 2 <= 8 * 1024 * 1024 else 256)
    assert m % tm == 0
    return pl.pallas_call(
        _mm_norm_res_kernel,
        out_shape=jax.ShapeDtypeStruct((m, n), F32),
        grid=(m // tm,),
        in_specs=[pl.BlockSpec((tm, kd), lambda i: (i, 0)),
                  pl.BlockSpec((kd, n), lambda i: (0, 0), pipeline_mode=pl.Buffered(1)),
                  pl.BlockSpec((1, n), lambda i: (0, 0)),
                  pl.BlockSpec((tm, n), lambda i: (i, 0))],
        out_specs=pl.BlockSpec((tm, n), lambda i: (i, 0)),
        compiler_params=_cparams(("parallel",)),
        name="matmul_norm_res",
    )(x, w, g.reshape(1, n), res)


def _xattn_kernel(q_ref, k_ref, v_ref, o_ref):
    scale = 1.0 / math.sqrt(X_HEAD_DIM)
    for h in range(X_HEADS):
        sl = slice(h * X_HEAD_DIM, (h + 1) * X_HEAD_DIM)
        s = _dot_nt(q_ref[:, sl], k_ref[:, sl].astype(BF16)) * scale
        p = jnp.exp(s - jnp.max(s, axis=-1, keepdims=True))
        l = jnp.sum(p, axis=-1, keepdims=True)
        o = _dot(p.astype(BF16), v_ref[:, sl].astype(BF16))
        o_ref[:, sl] = (o / l).astype(o_ref.dtype)


def _cross_attention(q, k_arr, v_arr, k_col, v_col, *, tq):
    b, t, _ = q.shape
    assert t % tq == 0
    return pl.pallas_call(
        _xattn_kernel,
        out_shape=jax.ShapeDtypeStruct((b, t, X_W), BF16),
        grid=(b, t // tq),
        in_specs=[pl.BlockSpec((None, tq, X_W), lambda i, j: (i, j, 0)),
                  pl.BlockSpec((MEM_LEN, X_W), lambda i, j: (i, k_col)),
                  pl.BlockSpec((MEM_LEN, X_W), lambda i, j: (i, v_col))],
        out_specs=pl.BlockSpec((None, tq, X_W), lambda i, j: (i, j, 0)),
        compiler_params=_cparams(("parallel", "arbitrary")),
        name="cross_attention",
    )(q, k_arr, v_arr)


XATTN_SEQS_PER_STEP = 4
XATTN_HEAD_ROWS = BF16_ROWS


def _xattn_token_kernel(q_ref, k_ref, v_ref, o_ref):
    rows = MEM_LEN * X_HEADS
    scale = 1.0 / math.sqrt(X_HEAD_DIM)
    col = lax.broadcasted_iota(jnp.int32, (1, rows), 1)
    head = lax.broadcasted_iota(jnp.int32, (XATTN_HEAD_ROWS, 1), 0)
    mine = (col % X_HEADS) == head
    for s in range(XATTN_SEQS_PER_STEP):
        span = slice(s * rows, (s + 1) * rows)
        sc = jnp.where(mine, _dot_nt(q_ref[s], k_ref[span, :].astype(BF16)) * scale, NEG)
        p = jnp.exp(sc - jnp.max(sc, axis=-1, keepdims=True))
        l = jnp.sum(p, axis=-1, keepdims=True)
        o_ref[s] = (_dot(p.astype(BF16), v_ref[span, :].astype(BF16)) / l).astype(o_ref.dtype)


def _cross_attention_token(q, mem_k, mem_v):
    b = q.shape[0]
    assert b % XATTN_SEQS_PER_STEP == 0
    rows = MEM_LEN * X_HEADS
    qh = jnp.pad(q.reshape(b, X_HEADS, X_HEAD_DIM), ((0, 0), (0, XATTN_HEAD_ROWS - X_HEADS), (0, 0)))
    q_spec = pl.BlockSpec((XATTN_SEQS_PER_STEP, XATTN_HEAD_ROWS, X_HEAD_DIM), lambda i: (i, 0, 0))
    kv_spec = pl.BlockSpec((XATTN_SEQS_PER_STEP * rows, X_HEAD_DIM), lambda i: (i, 0))
    o = pl.pallas_call(
        _xattn_token_kernel,
        out_shape=jax.ShapeDtypeStruct((b, XATTN_HEAD_ROWS, X_HEAD_DIM), BF16),
        grid=(b // XATTN_SEQS_PER_STEP,),
        in_specs=[q_spec, kv_spec, kv_spec],
        out_specs=q_spec,
        compiler_params=_cparams(("parallel",)),
        name="cross_attention_token",
    )(qh, mem_k.reshape(b * rows, X_HEAD_DIM), mem_v.reshape(b * rows, X_HEAD_DIM))
    return o[:, :X_HEADS].reshape(b, X_W)


def _wkv_chunk_kernel(za_ref, sp_ref, s0_ref, mu_ref, w0_ref, a0_ref, kk_ref, ka_ref, rk_ref,
                      lng_ref, lnb_ref, w2_ref, a2_ref, g2_ref, seg_ref, segt_ref,
                      ya_ref, shift_ref, s_ref,
                      r_ref, v_ref, kt_ref, kn_ref, b_ref, lw_ref, cum_ref, y_ref, *, tt, c, t_valid):
    c2x = 2 * c
    ti = pl.program_id(1)
    pairs = range(A_HEADS // 2)
    cols = [slice(p * LANES, (p + 1) * LANES) for p in pairs]
    bf = lambda x: x.astype(BF16)

    @pl.when(ti == 0)
    def _():
        s_ref[...] = s0_ref[...]
        shift_ref[...] = sp_ref[...]

    seg = seg_ref[...]
    segt = segt_ref[...]

    def head_sum(x):
        return _dot_exact_rhs(_dot_exact_rhs(x, seg), segt)

    za = za_ref[...].astype(F32)
    row = lax.broadcasted_iota(jnp.int32, (tt, 1), 0)
    prev = jnp.where(row == 0, shift_ref[...], pltpu.roll(za, 1, axis=0))
    last = (t_valid - 1) % tt
    shift_ref[...] = za[last:last + 1, :]
    mix = za + (prev - za) * mu_ref[...]
    valid = (ti * tt + row) < t_valid
    r = mix[:, 0:A_WIDTH]
    k = mix[:, A_WIDTH:2 * A_WIDTH]
    v = jnp.where(valid, mix[:, 2 * A_WIDTH:3 * A_WIDTH], 0.0)
    xw = w0_ref[...] + _dot(bf(jnp.tanh(mix[:, A_OFF_W:A_OFF_A])), w2_ref[...])
    lw = jnp.where(valid, -_sigmoid(xw) * math.exp(-0.5), 0.0)
    a = _sigmoid(a0_ref[...] + _dot(bf(mix[:, A_OFF_A:A_OFF_G]), a2_ref[...]))
    g = _dot(bf(_sigmoid(mix[:, A_OFF_G:A_PAD])), g2_ref[...])
    kk = k * kk_ref[...]
    kn = jnp.where(valid, kk * lax.rsqrt(jnp.maximum(head_sum(kk * kk), 1e-24)), 0.0)
    kt = jnp.where(valid, k * (1.0 + (a - 1.0) * ka_ref[...]), 0.0)
    bonus = head_sum(r * kt * rk_ref[...]) * v
    ri = lax.broadcasted_iota(jnp.int32, (tt, tt), 0)
    ci = lax.broadcasted_iota(jnp.int32, (tt, tt), 1)
    same = (ri // c) == (ci // c)
    cum_ref[...] = _dot_exact_lhs(jnp.where(same & (ci <= ri), 1.0, 0.0).astype(BF16), lw)
    r_ref[...] = r
    v_ref[...] = v
    kt_ref[...] = kt
    kn_ref[...] = kn
    b_ref[...] = kn * a
    lw_ref[...] = lw

    lane = lax.broadcasted_iota(jnp.int32, (1, LANES), 1)
    m0 = (lane < A_HEAD).astype(F32)
    m1 = 1.0 - m0
    r2 = lax.broadcasted_iota(jnp.int32, (c2x, c2x), 0)
    c2 = lax.broadcasted_iota(jnp.int32, (c2x, c2x), 1)
    strict = (r2 % c) > (c2 % c)
    incl = (r2 % c) >= (c2 % c)

    def stack(x):
        return jnp.concatenate([x * m0, x * m1], axis=0)

    def prepare(ch):
        rows = slice(ch * c, (ch + 1) * c)
        cum = cum_ref[rows, :]
        tot = cum_ref[(ch + 1) * c - 1:(ch + 1) * c, :]
        kt_, b_ = kt_ref[rows, :], b_ref[rows, :]
        p_inv = jnp.exp(-cum)
        p_end = jnp.exp(tot - cum)
        return dict(v=v_ref[rows, :], rh=r_ref[rows, :] * jnp.exp(cum), kap=kn_ref[rows, :] * jnp.exp(cum - lw_ref[rows, :]),
                    kh=kt_ * p_inv, bh=b_ * p_inv, khp=kt_ * p_end, bhp=b_ * p_end, pt=jnp.exp(tot))

    def solve(d):
        vst = [stack(d["v"][:, cols[p]]) for p in pairs]
        s0 = [s_ref[p] for p in pairs]
        lhs = [bf(jnp.concatenate([stack(d["kap"][:, cols[p]]), stack(d["rh"][:, cols[p]])], axis=0)) for p in pairs]
        gram = [_dot_nt(lhs[p], bf(jnp.concatenate([stack(d["kh"][:, cols[p]]), stack(d["bh"][:, cols[p]])], axis=0)))
                for p in pairs]
        from_s0 = [_dot_nt(lhs[p], bf(s0[p])) for p in pairs]
        x = [from_s0[p][:c2x] + _dot(bf(jnp.where(strict, gram[p][:c2x, :c2x], 0.0)), bf(vst[p])) for p in pairs]
        lp = [jnp.where(strict, gram[p][:c2x, c2x:], 0.0) for p in pairs]
        sq = [_dot(bf(lp[p]), bf(jnp.concatenate([lp[p], x[p]], axis=1))) for p in pairs]
        lp = [sq[p][:, :c2x] for p in pairs]
        x = [x[p] - sq[p][:, c2x:] for p in pairs]
        n = 2
        while 2 * n < c:
            sq = [_dot(bf(lp[p]), bf(jnp.concatenate([lp[p], x[p]], axis=1))) for p in pairs]
            lp = [sq[p][:, :c2x] for p in pairs]
            x = [x[p] + sq[p][:, c2x:] for p in pairs]
            n *= 2
        x = [x[p] + _dot(bf(lp[p]), bf(x[p])) for p in pairs]
        ys = []
        for p in pairs:
            tri = jnp.concatenate([jnp.where(incl, gram[p][c2x:, :c2x], 0.0),
                                   jnp.where(incl, -gram[p][c2x:, c2x:], 0.0)], axis=1)
            y = from_s0[p][c2x:] + _dot(bf(tri), bf(jnp.concatenate([vst[p], x[p]], axis=0)))
            ys.append(y[:c] + y[c:])
            decayed = jnp.concatenate([stack(d["khp"][:, cols[p]]), stack(d["bhp"][:, cols[p]])], axis=0)
            s_ref[p] = s0[p] * d["pt"][:, cols[p]] + _dot_tn(bf(jnp.concatenate([vst[p], -x[p]], axis=0)), bf(decayed))
        return jnp.concatenate(ys, axis=1)

    n_chunks = tt // c
    d = prepare(0)
    for ch in range(n_chunks):
        d_next = prepare(ch + 1) if ch + 1 < n_chunks else None
        y_ref[ch * c:(ch + 1) * c, :] = solve(d)
        d = d_next

    y = y_ref[...]
    mean = head_sum(y) * (1.0 / A_HEAD)
    yc = y - mean
    var = head_sum(yc * yc) * (1.0 / A_HEAD)
    yn = yc * lax.rsqrt(var + GN_EPS) * lng_ref[...] + lnb_ref[...]
    ya_ref[...] = ((yn + bonus) * g).astype(ya_ref.dtype)


def _wkv_mix(za, shift_prev, s0, prm, *, nb, t_pad, t_valid, tt, chunk):
    nt = t_pad // tt
    assert t_pad % tt == 0 and tt % chunk == 0 and chunk % SUBLANES == 0
    vec = lambda n: pl.BlockSpec((1, n), lambda i, j: (0, 0))
    full = lambda a: pl.BlockSpec(a.shape, lambda i, j: (0,) * a.ndim)
    state_spec = pl.BlockSpec((None, A_HEADS // 2, LANES, LANES), lambda i, j: (i, 0, 0, 0))
    shift_spec = pl.BlockSpec((None, 1, A_PAD), lambda i, j: (i, 0, 0))
    vecs = [prm[n] for n in ("mu", "w0", "a0", "k_k", "k_a", "r_k", "lnx_g", "lnx_b")]
    mats = [prm[n] for n in ("w2", "a2", "g2", "seg", "segt")]
    return pl.pallas_call(
        functools.partial(_wkv_chunk_kernel, tt=tt, c=chunk, t_valid=t_valid),
        out_shape=(jax.ShapeDtypeStruct((nb * t_pad, A_WIDTH), BF16),
                   jax.ShapeDtypeStruct((nb, 1, A_PAD), F32),
                   jax.ShapeDtypeStruct((nb, A_HEADS // 2, LANES, LANES), F32)),
        grid=(nb, nt),
        in_specs=[pl.BlockSpec((tt, A_PAD), lambda i, j: (i * nt + j, 0)), shift_spec, state_spec]
                 + [vec(x.shape[1]) for x in vecs] + [full(x) for x in mats],
        out_specs=(pl.BlockSpec((tt, A_WIDTH), lambda i, j: (i * nt + j, 0)), shift_spec, state_spec),
        scratch_shapes=[pltpu.VMEM((tt, A_WIDTH), F32)] * 8,
        compiler_params=_cparams(("parallel", "arbitrary")),
        name="wkv_mix",
    )(za, shift_prev, s0, *vecs, *mats)


CONV_HALO = 32


def _conv_kernel(zb_ref, cb_ref, w_ref, b_ref, g_ref, be_ref, o_ref, st_ref, ext_ref, conv_ref,
                 *, tt, t_last):
    ti = pl.program_id(1)
    pad = CONV_HALO - (CONV_W - 1)

    @pl.when(ti == 0)
    def _():
        ext_ref[0:CONV_HALO, :] = cb_ref[...]

    zb = zb_ref[...].astype(F32)
    ext_ref[CONV_HALO:CONV_HALO + tt, :] = zb[:, :B_WIDTH] * _sigmoid(zb[:, B_WIDTH:])
    rb = min(tt, 64)
    for r0 in range(0, tt, rb):
        for c0 in range(0, B_WIDTH, LANES):
            cols = slice(c0, c0 + LANES)
            x = ext_ref[r0:r0 + rb + CONV_HALO, cols]
            n = rb + CONV_HALO
            shifted = [x] + [pltpu.roll(x, n - s, axis=0) for s in range(1, SUBLANES)]
            acc = jnp.broadcast_to(b_ref[:, cols], (rb, LANES))
            for j in range(CONV_W):
                s = (pad + j) % SUBLANES
                base = pad + j - s
                acc = acc + shifted[s][base:base + rb] * w_ref[j:j + 1, cols]
            conv_ref[r0:r0 + rb, cols] = acc
    conv = conv_ref[...]
    mean = jnp.mean(conv, axis=-1, keepdims=True)
    xc = conv - mean
    var = jnp.mean(xc * xc, axis=-1, keepdims=True)
    y = xc * lax.rsqrt(var + LN_EPS) * g_ref[...] + be_ref[...]
    o_ref[...] = (y * _sigmoid(y)).astype(o_ref.dtype)
    st_ref[...] = ext_ref[t_last + pad:t_last + pad + CONV_W - 1, :]
    if tt >= CONV_HALO:
        ext_ref[0:CONV_HALO, :] = ext_ref[tt:tt + CONV_HALO, :]


def _conv_module(zb, conv_buf, prm, *, nb, t_pad, t_valid, tt):
    nt = t_pad // tt
    assert nt == 1 or tt >= CONV_HALO
    vec = pl.BlockSpec((1, B_WIDTH), lambda i, j: (0, 0))
    return pl.pallas_call(
        functools.partial(_conv_kernel, tt=tt, t_last=t_valid - (nt - 1) * tt),
        out_shape=(jax.ShapeDtypeStruct((nb * t_pad, B_WIDTH), BF16),
                   jax.ShapeDtypeStruct((nb, CONV_W - 1, B_WIDTH), F32)),
        grid=(nb, nt),
        in_specs=[pl.BlockSpec((tt, 2 * B_WIDTH), lambda i, j: (i * nt + j, 0)),
                  pl.BlockSpec((None, CONV_HALO, B_WIDTH), lambda i, j: (i, 0, 0)),
                  pl.BlockSpec((CONV_W, B_WIDTH), lambda i, j: (0, 0)), vec, vec, vec],
        out_specs=(pl.BlockSpec((tt, B_WIDTH), lambda i, j: (i * nt + j, 0)),
                   pl.BlockSpec((None, CONV_W - 1, B_WIDTH), lambda i, j: (i, 0, 0))),
        scratch_shapes=[pltpu.VMEM((CONV_HALO + tt, B_WIDTH), F32), pltpu.VMEM((tt, B_WIDTH), F32)],
        compiler_params=_cparams(("parallel", "arbitrary")),
        name="conv_module",
    )(zb, conv_buf, prm["dw_w"], prm["dw_b"], prm["cn_g"], prm["cn_b"])


def _top_blocks(gate, blk, n_sel):
    sel = []
    for _ in range(n_sel):
        m = jnp.max(gate, axis=0, keepdims=True)
        idx = jnp.min(jnp.where(gate == m, blk, gate.shape[0]), axis=0, keepdims=True)
        sel.append(idx)
        gate = jnp.where(blk == idx, -jnp.inf, gate)
    return sel


def _moba_prompt_kernel(q_ref, k_ref, v_ref, o_ref, kb_ref, vt_ref, km_ref, qf_ref, *, t):
    kvh = pl.program_id(1)
    qi = pl.program_id(2)
    nb = t // MOBA_BLOCK
    nq = C_GROUP * Q_BLOCK

    log2e = 1.0 / math.log(2.0)
    n_feat = 3

    def slope2_of(group):
        head = (kvh * C_GROUP + group + 1).astype(F32)
        return jnp.exp(head * (-(8.0 / C_HEADS) * math.log(2.0))) * log2e

    @pl.when(qi == 0)
    def _():
        srow = slope2_of(lax.broadcasted_iota(jnp.int32, (nq, 1), 0) // Q_BLOCK)
        s_hi = srow.astype(BF16).astype(F32)
        s_mid = (srow - s_hi).astype(BF16).astype(F32)
        s_lo = (srow - s_hi - s_mid).astype(BF16).astype(F32)
        part = lax.broadcasted_iota(jnp.int32, (1, LANES), 1)
        qfeat = jnp.where(part % n_feat == 0, s_hi, jnp.where(part % n_feat == 1, s_mid, s_lo))
        qf_ref[...] = jnp.where(part < 2 * n_feat, qfeat, 0.0).astype(BF16)
        km_ref[...] = jnp.zeros_like(km_ref)
        lane = lax.broadcasted_iota(jnp.int32, (MOBA_BLOCK, LANES), 1)
        offset = lax.broadcasted_iota(jnp.int32, (MOBA_BLOCK, LANES), 0).astype(F32)
        for n in range(nb):
            kblk = k_ref[n * MOBA_BLOCK:(n + 1) * MOBA_BLOCK, :]
            feat = jnp.where(lane < n_feat, float(n * MOBA_BLOCK), jnp.where(lane < 2 * n_feat, offset, 0.0))
            kb_ref[n] = jnp.concatenate([kblk, feat], axis=1).astype(BF16)
            vt_ref[n] = v_ref[n * MOBA_BLOCK:(n + 1) * MOBA_BLOCK, :].T.astype(BF16)
            km_ref[n:n + 1, :] = jnp.mean(kblk, axis=0, keepdims=True)

    q = q_ref[...]
    qs = jnp.concatenate([q[:, g * C_HEAD_DIM:(g + 1) * C_HEAD_DIM] for g in range(C_GROUP)], axis=0)
    own = (qi * Q_BLOCK) // MOBA_BLOCK
    blk = lax.broadcasted_iota(jnp.int32, (km_ref.shape[0], 1), 0)
    gate = _dot_x3(_dot_nt, km_ref[...], qs)
    gate = jnp.where(blk < own, gate, jnp.where(blk < nb, NEG, -jnp.inf))
    sel = _top_blocks(gate, blk, MOBA_TOPK)
    sel = [jnp.where(own > r, sel[r], -1) for r in range(MOBA_TOPK)]

    col = lax.broadcasted_iota(jnp.int32, (1, nq), 1)
    pos = qi * Q_BLOCK + col % Q_BLOCK
    base = -slope2_of(col // Q_BLOCK) * pos.astype(F32)
    qb = jnp.concatenate([(qs * (log2e / math.sqrt(C_HEAD_DIM))).astype(BF16), qf_ref[...]], axis=1)
    trow = lax.broadcasted_iota(jnp.int32, (MOBA_BLOCK, 1), 0)

    def attend_upto(own_static):
        scores = []
        for n in range(own_static):
            picked = (sel[0] == n) | (sel[1] == n) | (sel[2] == n)
            scores.append(_dot_nt(kb_ref[n], qb) + jnp.where(picked, base, NEG))
        causal = trow <= col % Q_BLOCK
        scores.append(jnp.where(causal, _dot_nt(kb_ref[own_static], qb) + base, NEG))
        m = scores[-1].max(axis=0, keepdims=True)
        for s in scores[:-1]:
            m = jnp.maximum(m, jnp.max(s, axis=0, keepdims=True))
        l = jnp.zeros((1, nq), F32)
        acc = jnp.zeros((C_HEAD_DIM, nq), F32)
        for n, s in enumerate(scores):
            p = jnp.exp2(s - m)
            l = l + jnp.sum(p, axis=0, keepdims=True)
            acc = acc + _dot(vt_ref[n], p.astype(BF16))
        o = acc / l
        for g in range(C_GROUP):
            cols = slice(g * Q_BLOCK, (g + 1) * Q_BLOCK)
            o_ref[:, g * C_HEAD_DIM:(g + 1) * C_HEAD_DIM] = o[:, cols].T.astype(o_ref.dtype)

    for own_static in range(nb):
        pl.when(own == own_static)(functools.partial(attend_upto, own_static))


def _moba_prompt(z, *, nb, t):
    nq = t // Q_BLOCK
    assert t % Q_BLOCK == 0 and t // MOBA_BLOCK <= 256 and MOBA_BLOCK <= 256
    kcol = C_Q_W // C_HEAD_DIM
    vcol = (C_Q_W + C_KV_W) // C_HEAD_DIM
    return pl.pallas_call(
        functools.partial(_moba_prompt_kernel, t=t),
        out_shape=jax.ShapeDtypeStruct((nb * t, C_Q_W), BF16),
        grid=(nb, C_KV_HEADS, nq),
        in_specs=[pl.BlockSpec((Q_BLOCK, C_GROUP * C_HEAD_DIM), lambda b, h, i: (b * nq + i, h)),
                  pl.BlockSpec((t, C_HEAD_DIM), lambda b, h, i: (b, kcol + h)),
                  pl.BlockSpec((t, C_HEAD_DIM), lambda b, h, i: (b, vcol + h))],
        out_specs=pl.BlockSpec((Q_BLOCK, C_GROUP * C_HEAD_DIM), lambda b, h, i: (b * nq + i, h)),
        scratch_shapes=[pltpu.VMEM((t // MOBA_BLOCK, MOBA_BLOCK, C_HEAD_DIM + LANES), BF16),
                        pltpu.VMEM((t // MOBA_BLOCK, C_HEAD_DIM, MOBA_BLOCK), BF16),
                        pltpu.VMEM((-(-t // MOBA_BLOCK // SUBLANES) * SUBLANES, C_HEAD_DIM), F32),
                        pltpu.VMEM((C_GROUP * Q_BLOCK, LANES), BF16)],
        compiler_params=_cparams(("parallel", "parallel", "arbitrary")),
        name="moba_prompt",
    )(z, z, z)


PAGES_PER_STEP = 8 * MOBA_BLOCK // PAGE_SIZE


def _moba_decode_kernel(pt_ref, z_ref, *refs, n_blocks):
    del pt_ref
    npg = PAGES_PER_STEP
    k_refs, v_refs = refs[:npg], refs[npg:2 * npg]
    o_ref = refs[2 * npg]
    m_ref, l_ref, acc_ref, gate_ref = refs[2 * npg + 1:]
    j = pl.program_id(1)
    ppb = MOBA_BLOCK // PAGE_SIZE
    past = n_blocks * MOBA_BLOCK
    scale = 1.0 / math.sqrt(C_HEAD_DIM)
    hrow = lax.broadcasted_iota(jnp.int32, (C_HEADS, 1), 0)
    slope = jnp.exp((hrow + 1).astype(F32) * (-(8.0 / C_HEADS) * math.log(2.0)))

    z = z_ref[...]
    qrows = jnp.concatenate([z[:, h * C_HEAD_DIM:(h + 1) * C_HEAD_DIM] for h in range(C_HEADS)], axis=0)
    qb = qrows.astype(BF16)
    rows_blk = MOBA_BLOCK * C_KV_HEADS
    col = lax.broadcasted_iota(jnp.int32, (1, rows_blk), 1)
    mine = (col % C_KV_HEADS) == (hrow // C_GROUP)
    tab = jnp.where(mine, slope * (col // C_KV_HEADS).astype(F32), NEG)

    for blk in range(npg // ppb):
        n = j * (npg // ppb) + blk
        kblk = jnp.concatenate([k_refs[blk * ppb + i][...] for i in range(ppb)], axis=0)
        vblk = jnp.concatenate([v_refs[blk * ppb + i][...] for i in range(ppb)], axis=0)
        s = _dot_nt(qb, kblk.astype(BF16)) * scale + tab - slope * (past - n * MOBA_BLOCK).astype(F32)
        m = jnp.max(s, axis=-1, keepdims=True)
        p = jnp.exp(s - m)
        m_ref[pl.ds(n, 1)] = m[None]
        l_ref[pl.ds(n, 1)] = jnp.sum(p, axis=-1, keepdims=True)[None]
        acc_ref[pl.ds(n, 1)] = _dot(p.astype(BF16), vblk.astype(BF16))[None]
        ksum = jnp.sum(kblk.reshape(rows_blk // SUBLANES, SUBLANES, C_HEAD_DIM), axis=0)
        kmean = (ksum[:C_KV_HEADS] + ksum[C_KV_HEADS:]) * (1.0 / MOBA_BLOCK)
        kmrows = jnp.concatenate([jnp.broadcast_to(kmean[kvh:kvh + 1], (C_GROUP, C_HEAD_DIM))
                                  for kvh in range(C_KV_HEADS)], axis=0)
        gate_ref[pl.ds(n, 1)] = jnp.sum(qrows * kmrows, axis=-1, keepdims=True)[None]

    @pl.when(j == pl.num_programs(1) - 1)
    def _():
        gates = gate_ref[...]
        ms = m_ref[...]
        bidx = lax.broadcasted_iota(jnp.int32, gates.shape, 0)
        picked = jnp.zeros(gates.shape, jnp.bool_)
        for _ in range(min(MOBA_TOPK, n_blocks)):
            gmax = jnp.max(gates, axis=0, keepdims=True)
            first = jnp.min(jnp.where(gates == gmax, bidx, n_blocks), axis=0, keepdims=True)
            hit = bidx == first
            picked = picked | hit
            gates = jnp.where(hit, -jnp.inf, gates)
        knew = jnp.concatenate(
            [z[:, C_Q_W + kvh * C_HEAD_DIM:C_Q_W + (kvh + 1) * C_HEAD_DIM] for kvh in range(C_KV_HEADS)
             for _ in range(C_GROUP)], axis=0)
        vnew = jnp.concatenate(
            [z[:, C_Q_W + C_KV_W + kvh * C_HEAD_DIM:C_Q_W + C_KV_W + (kvh + 1) * C_HEAD_DIM]
             for kvh in range(C_KV_HEADS) for _ in range(C_GROUP)], axis=0)
        s_own = jnp.sum(qb.astype(F32) * knew.astype(BF16).astype(F32), axis=-1, keepdims=True) * scale
        mtot = jnp.maximum(jnp.max(jnp.where(picked, ms, NEG), axis=0), s_own)
        wgt = jnp.where(picked, jnp.exp(ms - mtot[None]), 0.0)
        p_own = jnp.exp(s_own - mtot)
        l = jnp.sum(wgt * l_ref[...], axis=0) + p_own
        acc = jnp.sum(wgt * acc_ref[...], axis=0) + p_own.astype(BF16).astype(F32) * vnew.astype(BF16).astype(F32)
        o_ref[...] = (acc / l).astype(o_ref.dtype)


def _moba_decode(z, cache_k, cache_v, page_table):
    nb, n_pages = page_table.shape
    assert n_pages % PAGES_PER_STEP == 0 and 2 * C_KV_HEADS == SUBLANES
    n_blocks = n_pages * PAGE_SIZE // MOBA_BLOCK
    steps = n_pages // PAGES_PER_STEP

    def page_spec(i):
        return pl.BlockSpec((None, PAGE_SIZE * C_KV_HEADS, C_HEAD_DIM),
                            lambda b, j, pt: (pt[b, j * PAGES_PER_STEP + i], 0, 0))

    pages = [page_spec(i) for i in range(PAGES_PER_STEP)]
    zw = z.shape[-1]
    return pl.pallas_call(
        functools.partial(_moba_decode_kernel, n_blocks=n_blocks),
        out_shape=jax.ShapeDtypeStruct((nb, C_HEADS, C_HEAD_DIM), BF16),
        grid_spec=pltpu.PrefetchScalarGridSpec(
            num_scalar_prefetch=1,
            grid=(nb, steps),
            in_specs=[pl.BlockSpec((None, 1, zw), lambda b, j, pt: (b, 0, 0))] + pages + pages,
            out_specs=pl.BlockSpec((None, C_HEADS, C_HEAD_DIM), lambda b, j, pt: (b, 0, 0)),
            scratch_shapes=[pltpu.VMEM((n_blocks, C_HEADS, 1), F32), pltpu.VMEM((n_blocks, C_HEADS, 1), F32),
                            pltpu.VMEM((n_blocks, C_HEADS, C_HEAD_DIM), F32),
                            pltpu.VMEM((n_blocks, C_HEADS, 1), F32)]),
        compiler_params=_cparams(("parallel", "arbitrary")),
        name="moba_decode",
    )(page_table, z, *([cache_k] * PAGES_PER_STEP), *([cache_v] * PAGES_PER_STEP))


def _pad_a_cols(x):
    i3 = 3 * A_WIDTH
    z = lambda n: jnp.zeros(x.shape[:-1] + (n,), x.dtype)
    return jnp.concatenate([
        x[..., :i3],
        x[..., i3:i3 + DECAY_RANK], z(LANES - DECAY_RANK),
        x[..., i3 + DECAY_RANK:i3 + DECAY_RANK + AAA_RANK], z(LANES - AAA_RANK),
        x[..., i3 + DECAY_RANK + AAA_RANK:], z(2 * LANES - GATE_RANK)], axis=-1)


def _unpad_a_cols(x):
    return jnp.concatenate([x[..., :A_OFF_W + DECAY_RANK], x[..., A_OFF_A:A_OFF_A + AAA_RANK],
                            x[..., A_OFF_G:A_OFF_G + GATE_RANK]], axis=-1)


def _pad_rows(w, n):
    return jnp.concatenate([w, jnp.zeros((n - w.shape[0],) + w.shape[1:], w.dtype)], axis=0)


def _ab_params(i, w_in_ab, mu_ab, w0, w2, a0, a2, g2, k_k, k_a, r_k, lnx_g, lnx_b, dw_w, dw_b, cn_g, cn_b):
    row = lambda x: x.reshape(1, -1)
    head = jnp.arange(A_WIDTH)[:, None] // A_HEAD == jnp.arange(LANES)[None, :]
    return {
        "w_a": _pad_a_cols(w_in_ab[i][:, :A_COLS]).astype(BF16),
        "w_b": w_in_ab[i][:, A_COLS:].astype(BF16),
        "mu": row(_pad_a_cols(mu_ab[i])),
        "w0": row(w0[i]), "a0": row(a0[i]), "k_k": row(k_k[i]), "k_a": row(k_a[i]), "r_k": row(r_k[i]),
        "lnx_g": row(lnx_g[i]), "lnx_b": row(lnx_b[i]),
        "w2": _pad_rows(w2[i], LANES).astype(BF16), "a2": _pad_rows(a2[i], LANES).astype(BF16),
        "g2": _pad_rows(g2[i], 2 * LANES).astype(BF16),
        "seg": head.astype(BF16), "segt": head.T.astype(BF16),
        "dw_w": dw_w[i], "dw_b": row(dw_b[i]), "cn_g": row(cn_g[i]), "cn_b": row(cn_b[i]),
    }


def _state_to_pairs(s):
    nb = s.shape[0]
    s = s.reshape(nb, A_HEADS // 2, 2, A_HEAD, A_HEAD)
    z = jnp.zeros_like(s[:, :, 0])
    top = jnp.concatenate([s[:, :, 0], z], axis=-1)
    bot = jnp.concatenate([z, s[:, :, 1]], axis=-1)
    return jnp.concatenate([top, bot], axis=-2)


def _pairs_to_state(s):
    nb = s.shape[0]
    return jnp.stack([s[:, :, :A_HEAD, :A_HEAD], s[:, :, A_HEAD:, A_HEAD:]], axis=2).reshape(
        nb, A_HEADS, A_HEAD, A_HEAD)


def _mixer_ab(x, g_pre, shift_prev, wkv0, conv_buf, prm, *, nb, t):
    if t >= WKV_CHUNK:
        assert t % WKV_CHUNK == 0
        t_pad, chunk, tt_a, tt_b, xp = t, WKV_CHUNK, min(t, 256), min(t, 128), x
    else:
        t_pad = chunk = tt_a = tt_b = -(-t // WKV_SHORT_CHUNK) * WKV_SHORT_CHUNK
        xp = jnp.pad(x.reshape(nb, t, -1), ((0, 0), (0, t_pad - t), (0, 0))).reshape(nb * t_pad, -1)
    za = _norm_matmul(xp, g_pre, prm["w_a"], BF16, tn=512)
    zb = _norm_matmul(xp, g_pre, prm["w_b"], BF16, tn=512)
    ya, shift, s_new = _wkv_mix(za, _pad_a_cols(shift_prev)[:, None, :], _state_to_pairs(wkv0), prm,
                                nb=nb, t_pad=t_pad, t_valid=t, tt=tt_a, chunk=chunk)
    hist = jnp.pad(conv_buf, ((0, 0), (CONV_HALO - (CONV_W - 1), 0), (0, 0)))
    yb, conv_new = _conv_module(zb, hist, prm, nb=nb, t_pad=t_pad, t_valid=t, tt=tt_b)
    y = jnp.concatenate([ya, yb], axis=-1)
    if t_pad != t:
        y = y.reshape(nb, t_pad, -1)[:, :t].reshape(nb * t, -1)
    return y, _unpad_a_cols(shift[:, 0]), _pairs_to_state(s_new), conv_new


def _rest_of_layer(x, l, k_arr, v_arr, k_col, v_col, wts, *, nb, t):
    (g_x_pre, g_x_post, w_xq, w_xo, g_ffn_pre, g_ffn_post, w_gu, w_down) = wts
    q = _norm_matmul(x, g_x_pre[l], w_xq, BF16, tn=X_W)
    if k_arr.ndim == 4:
        assert t == 1
        o = _cross_attention_token(q, k_arr, v_arr)
    else:
        tq = min(t, 512)
        assert t % tq == 0 and tq % BF16_ROWS == 0
        o = _cross_attention(q.reshape(nb, t, X_W), k_arr, v_arr, k_col, v_col, tq=tq).reshape(nb * t, X_W)
    x = _matmul_norm_res(o, w_xo, g_x_post[l], x)
    h = _norm_matmul(x, g_ffn_pre[l], w_gu, BF16, tn=512, swiglu_half=D_FF)
    return _matmul_norm_res(h, w_down, g_ffn_post[l], x)


def kernel(x_prompt, x_sample, cache_k, cache_v, state_wkv, state_shift, state_conv, cache_mem_k, cache_mem_v, page_table, mem_prompt, w_in_ab, w_out_ab, mu_ab, w0, w2, a0, a2, g2, k_k, k_a, r_k, lnx_g, lnx_b, dw_w, dw_b, cn_g, cn_b, w_in_c, w_out_c, g_mix_pre, g_mix_post, g_x_pre, g_x_post, g_mem, w_xq, w_xkv, w_xo, g_ffn_pre, g_ffn_post, w_ffn_gu, w_ffn_down):
    bp, tp, d = x_prompt.shape
    bs, ts, _ = x_sample.shape
    depth = g_mix_pre.shape[0]
    xp = x_prompt.reshape(bp * tp, d)
    xs = x_sample.reshape(bs * ts, d)
    mem = mem_prompt.reshape(bp * MEM_LEN, d)
    outs = {n: [] for n in ("kp", "vp", "wkvp", "shp", "cvp", "mkp", "mvp", "ks", "vs", "wkvs", "shs", "cvs")}
    for l in range(depth):
        i = l // 2
        if l % 2 == 0:
            prm = _ab_params(i, w_in_ab, mu_ab, w0, w2, a0, a2, g2, k_k, k_a, r_k, lnx_g, lnx_b,
                             dw_w, dw_b, cn_g, cn_b)
            w_out = w_out_ab[i].astype(BF16)
            fp, shp, wkvp, cvp = _mixer_ab(
                xp, g_mix_pre[l], jnp.zeros((bp, A_COLS), F32), jnp.zeros((bp, A_HEADS, A_HEAD, A_HEAD), F32),
                jnp.zeros((bp, CONV_W - 1, B_WIDTH), F32), prm, nb=bp, t=tp)
            fs, shs, wkvs, cvs = _mixer_ab(xs, g_mix_pre[l], state_shift[i], state_wkv[i], state_conv[i], prm,
                                           nb=bs, t=ts)
            for n, val in (("shp", shp), ("wkvp", wkvp), ("cvp", cvp), ("shs", shs), ("wkvs", wkvs), ("cvs", cvs)):
                outs[n].append(val)
        else:
            w_in = w_in_c[i].astype(BF16)
            w_out = w_out_c[i].astype(BF16)
            zp = _norm_matmul(xp, g_mix_pre[l], w_in, F32, tn=512)
            fp = _moba_prompt(zp, nb=bp, t=tp)
            zs = _norm_matmul(xs, g_mix_pre[l], w_in, F32, tn=512)
            n_phys = cache_k.shape[1]
            page_rows = (-1, PAGE_SIZE * C_KV_HEADS, C_HEAD_DIM)
            fs = _moba_decode(zs.reshape(bs, ts, -1), cache_k.reshape(page_rows), cache_v.reshape(page_rows),
                              page_table + i * n_phys).reshape(bs * ts, C_Q_W)
            kv = lambda z, nb, t, off: z[:, off:off + C_KV_W].reshape(nb, t, C_KV_HEADS, C_HEAD_DIM)
            outs["kp"].append(kv(zp, bp, tp, C_Q_W))
            outs["vp"].append(kv(zp, bp, tp, C_Q_W + C_KV_W))
            outs["ks"].append(kv(zs, bs, ts, C_Q_W))
            outs["vs"].append(kv(zs, bs, ts, C_Q_W + C_KV_W))
        xp = _matmul_norm_res(fp, w_out, g_mix_post[l], xp)
        xs = _matmul_norm_res(fs, w_out, g_mix_post[l], xs)
        mkv = _norm_matmul(mem, g_mem[l], w_xkv[l].astype(BF16), F32, tn=512)
        outs["mkp"].append(mkv[:, :X_W].reshape(bp, MEM_LEN, X_HEADS, X_HEAD_DIM))
        outs["mvp"].append(mkv[:, X_W:].reshape(bp, MEM_LEN, X_HEADS, X_HEAD_DIM))
        wts = (g_x_pre, g_x_post, w_xq[l].astype(BF16), w_xo[l].astype(BF16), g_ffn_pre, g_ffn_post,
               w_ffn_gu[l].astype(BF16), w_ffn_down[l].astype(BF16))
        xp = _rest_of_layer(xp, l, mkv, mkv, 0, 1, wts, nb=bp, t=tp)
        if ts == 1:
            mem_s = (cache_mem_k[l], cache_mem_v[l])
        else:
            mem_s = (cache_mem_k[l].reshape(bs * MEM_LEN, X_W), cache_mem_v[l].reshape(bs * MEM_LEN, X_W))
        xs = _rest_of_layer(xs, l, *mem_s, 0, 0, wts, nb=bs, t=ts)
    st = lambda n: jnp.stack(outs[n])
    return (xp.reshape(bp, tp, d), xs.reshape(bs, ts, d), st("kp"), st("vp"), st("wkvp"), st("shp"), st("cvp"),
            st("mkp"), st("mvp"), st("ks"), st("vs"), st("wkvs"), st("shs"), st("cvs"))
```

```python
import functools
import math

import jax
import jax.numpy as jnp
from jax import lax
from jax.experimental import pallas as pl
from jax.experimental.pallas import tpu as pltpu

F32 = jnp.float32
BF16 = jnp.bfloat16

D_MODEL = 2048
PAGE_SIZE = 128
A_WIDTH = 1024
A_HEAD = 64
A_HEADS = 16
DECAY_RANK = 64
AAA_RANK = 64
GATE_RANK = 160
A_COLS = 3 * A_WIDTH + DECAY_RANK + AAA_RANK + GATE_RANK
GN_EPS = 64e-5
B_WIDTH = 1024
CONV_W = 31
C_HEADS = 16
C_HEAD_DIM = 128
C_KV_HEADS = 4
C_GROUP = 4
C_Q_W = 2048
C_KV_W = 512
MOBA_BLOCK = 256
MOBA_TOPK = 3
Q_BLOCK = MOBA_BLOCK
MEM_LEN = 256
X_HEADS = 4
X_HEAD_DIM = 128
X_W = 512
D_FF = 5632
RMS_EPS = 1e-6
LN_EPS = 1e-5
NEG = -1e30

SUBLANES = 8
MXU_WIDTH = 256
BF16_ROWS = 16
LANES = 128
A_OFF_W = 3 * A_WIDTH
A_OFF_A = A_OFF_W + LANES
A_OFF_G = A_OFF_A + LANES
A_PAD = A_OFF_G + 2 * LANES
WKV_CHUNK = 64
WKV_SHORT_CHUNK = BF16_ROWS
VMEM_LIMIT = 56 * 1024 * 1024


def _cparams(sem):
    return pltpu.CompilerParams(dimension_semantics=sem, vmem_limit_bytes=VMEM_LIMIT)


def _dot(a, b):
    return jnp.dot(a, b, preferred_element_type=F32)


def _dot_nt(a, b):
    return lax.dot_general(a, b, (((1,), (1,)), ((), ())), preferred_element_type=F32)


def _dot_tn(a, b):
    return lax.dot_general(a, b, (((0,), (0,)), ((), ())), preferred_element_type=F32)


def _split2(x):
    hi = x.astype(BF16)
    lo = (x - hi.astype(F32)).astype(BF16)
    return hi, lo


def _dot_x3(dot, a, b):
    ah, al = _split2(a)
    bh, bl = _split2(b)
    return dot(ah, bh) + (dot(ah, bl) + dot(al, bh))


def _dot_exact_rhs(a, b_bf16):
    hi, lo = _split2(a)
    return _dot(hi, b_bf16) + _dot(lo, b_bf16)


def _dot_exact_lhs(a_bf16, b):
    hi, lo = _split2(b)
    return _dot(a_bf16, hi) + _dot(a_bf16, lo)


def _sigmoid(x):
    return 1.0 / (1.0 + jnp.exp(-x))


def _norm_swiglu_kernel(x_ref, g_ref, wg_ref, wu_ref, o_ref, xn_ref):
    x = x_ref[...]
    ms = jnp.mean(x * x, axis=-1, keepdims=True)
    xn_ref[...] = (x * lax.rsqrt(ms + RMS_EPS) * g_ref[...]).astype(BF16)
    for c0 in range(0, o_ref.shape[1], MXU_WIDTH):
        cs = slice(c0, c0 + MXU_WIDTH)
        gate = _dot(xn_ref[...], wg_ref[:, cs])
        o_ref[:, cs] = (gate * _sigmoid(gate) * _dot(xn_ref[...], wu_ref[:, cs])).astype(o_ref.dtype)


def _norm_mm_resident_kernel(x_ref, g_ref, w_ref, o_ref, xn_ref, *, tn):
    x = x_ref[...]
    ms = jnp.mean(x * x, axis=-1, keepdims=True)
    xn_ref[...] = (x * lax.rsqrt(ms + RMS_EPS) * g_ref[...]).astype(BF16)
    for c0 in range(0, o_ref.shape[1], tn):
        o_ref[:, c0:c0 + tn] = _dot(xn_ref[...], w_ref[:, c0:c0 + tn]).astype(o_ref.dtype)


RESIDENT_WEIGHT_BYTES = 16 * 1024 * 1024


def _norm_matmul(x, g, w, out_dtype, *, tn, swiglu_half=None):
    m, d = x.shape
    n = swiglu_half if swiglu_half else w.shape[1]
    if not swiglu_half and w.size * w.dtype.itemsize <= RESIDENT_WEIGHT_BYTES:
        tm = min(m, 512)
        assert m % tm == 0 and n % tn == 0
        return pl.pallas_call(
            functools.partial(_norm_mm_resident_kernel, tn=tn),
            out_shape=jax.ShapeDtypeStruct((m, n), out_dtype),
            grid=(m // tm,),
            in_specs=[pl.BlockSpec((tm, d), lambda i: (i, 0)),
                      pl.BlockSpec((1, d), lambda i: (0, 0)),
                      pl.BlockSpec((d, n), lambda i: (0, 0), pipeline_mode=pl.Buffered(1))],
            out_specs=pl.BlockSpec((tm, n), lambda i: (i, 0)),
            scratch_shapes=[pltpu.VMEM((tm, d), BF16)],
            compiler_params=_cparams(("parallel",)),
            name="norm_matmul_resident",
        )(x, g.reshape(1, d), w)
    assert swiglu_half, "a plain weight larger than RESIDENT_WEIGHT_BYTES is not supported"
    groups = next(k for k in range(1, n + 1) if n % k == 0 and (n // k) % MXU_WIDTH == 0
                  and 2 * d * (n // k) * w.dtype.itemsize <= 3 * RESIDENT_WEIGHT_BYTES // 2)
    tg = n // groups
    tm = min(m, 512)
    assert m % tm == 0
    w_spec = lambda off: pl.BlockSpec((d, tg), lambda j, i: (0, j + off), pipeline_mode=pl.Buffered(1))
    return pl.pallas_call(
        _norm_swiglu_kernel,
        out_shape=jax.ShapeDtypeStruct((m, n), out_dtype),
        grid=(groups, m // tm),
        in_specs=[pl.BlockSpec((tm, d), lambda j, i: (i, 0)),
                  pl.BlockSpec((1, d), lambda j, i: (0, 0)), w_spec(0), w_spec(groups)],
        out_specs=pl.BlockSpec((tm, tg), lambda j, i: (i, j)),
        scratch_shapes=[pltpu.VMEM((tm, d), BF16)],
        compiler_params=_cparams(("parallel", "parallel")),
        name="norm_swiglu",
    )(x, g.reshape(1, d), w, w)


def _mm_norm_res_kernel(x_ref, w_ref, g_ref, r_ref, o_ref):
    y = _dot(x_ref[...], w_ref[...])
    ms = jnp.mean(y * y, axis=-1, keepdims=True)
    o_ref[...] = r_ref[...] + y * lax.rsqrt(ms + RMS_EPS) * g_ref[...]


def _matmul_norm_res(x, w, g, res):
    m, kd = x.shape
    n = w.shape[1]
    tm = min(m, 512 if kd * n * 2 <= 8 * 1024 * 1024 else 256)
    assert m % tm == 0
    return pl.pallas_call(
        _mm_norm_res_kernel,
        out_shape=jax.ShapeDtypeStruct((m, n), F32),
        grid=(m // tm,),
        in_specs=[pl.BlockSpec((tm, kd), lambda i: (i, 0)),
                  pl.BlockSpec((kd, n), lambda i: (0, 0), pipeline_mode=pl.Buffered(1)),
                  pl.BlockSpec((1, n), lambda i: (0, 0)),
                  pl.BlockSpec((tm, n), lambda i: (i, 0))],
        out_specs=pl.BlockSpec((tm, n), lambda i: (i, 0)),
        compiler_params=_cparams(("parallel",)),
        name="matmul_norm_res",
    )(x, w, g.reshape(1, n), res)


def _xattn_kernel(q_ref, k_ref, v_ref, o_ref):
    scale = 1.0 / math.sqrt(X_HEAD_DIM)
    for h in range(X_HEADS):
        sl = slice(h * X_HEAD_DIM, (h + 1) * X_HEAD_DIM)
        s = _dot_nt(q_ref[:, sl], k_ref[:, sl].astype(BF16)) * scale
        p = jnp.exp(s - jnp.max(s, axis=-1, keepdims=True))
        l = jnp.sum(p, axis=-1, keepdims=True)
        o = _dot(p.astype(BF16), v_ref[:, sl].astype(BF16))
        o_ref[:, sl] = (o / l).astype(o_ref.dtype)


def _cross_attention(q, k_arr, v_arr, k_col, v_col, *, tq):
    b, t, _ = q.shape
    assert t % tq == 0
    return pl.pallas_call(
        _xattn_kernel,
        out_shape=jax.ShapeDtypeStruct((b, t, X_W), BF16),
        grid=(b, t // tq),
        in_specs=[pl.BlockSpec((None, tq, X_W), lambda i, j: (i, j, 0)),
                  pl.BlockSpec((MEM_LEN, X_W), lambda i, j: (i, k_col)),
                  pl.BlockSpec((MEM_LEN, X_W), lambda i, j: (i, v_col))],
        out_specs=pl.BlockSpec((None, tq, X_W), lambda i, j: (i, j, 0)),
        compiler_params=_cparams(("parallel", "arbitrary")),
        name="cross_attention",
    )(q, k_arr, v_arr)


XATTN_SEQS_PER_STEP = 4
XATTN_HEAD_ROWS = BF16_ROWS


def _xattn_token_kernel(q_ref, k_ref, v_ref, o_ref):
    rows = MEM_LEN * X_HEADS
    scale = 1.0 / math.sqrt(X_HEAD_DIM)
    col = lax.broadcasted_iota(jnp.int32, (1, rows), 1)
    head = lax.broadcasted_iota(jnp.int32, (XATTN_HEAD_ROWS, 1), 0)
    mine = (col % X_HEADS) == head
    for s in range(XATTN_SEQS_PER_STEP):
        span = slice(s * rows, (s + 1) * rows)
        sc = jnp.where(mine, _dot_nt(q_ref[s], k_ref[span, :].astype(BF16)) * scale, NEG)
        p = jnp.exp(sc - jnp.max(sc, axis=-1, keepdims=True))
        l = jnp.sum(p, axis=-1, keepdims=True)
        o_ref[s] = (_dot(p.astype(BF16), v_ref[span, :].astype(BF16)) / l).astype(o_ref.dtype)


def _cross_attention_token(q, mem_k, mem_v):
    b = q.shape[0]
    assert b % XATTN_SEQS_PER_STEP == 0
    rows = MEM_LEN * X_HEADS
    qh = jnp.pad(q.reshape(b, X_HEADS, X_HEAD_DIM), ((0, 0), (0, XATTN_HEAD_ROWS - X_HEADS), (0, 0)))
    q_spec = pl.BlockSpec((XATTN_SEQS_PER_STEP, XATTN_HEAD_ROWS, X_HEAD_DIM), lambda i: (i, 0, 0))
    kv_spec = pl.BlockSpec((XATTN_SEQS_PER_STEP * rows, X_HEAD_DIM), lambda i: (i, 0))
    o = pl.pallas_call(
        _xattn_token_kernel,
        out_shape=jax.ShapeDtypeStruct((b, XATTN_HEAD_ROWS, X_HEAD_DIM), BF16),
        grid=(b // XATTN_SEQS_PER_STEP,),
        in_specs=[q_spec, kv_spec, kv_spec],
        out_specs=q_spec,
        compiler_params=_cparams(("parallel",)),
        name="cross_attention_token",
    )(qh, mem_k.reshape(b * rows, X_HEAD_DIM), mem_v.reshape(b * rows, X_HEAD_DIM))
    return o[:, :X_HEADS].reshape(b, X_W)


def _wkv_chunk_kernel(za_ref, sp_ref, s0_ref, mu_ref, w0_ref, a0_ref, kk_ref, ka_ref, rk_ref,
                      lng_ref, lnb_ref, w2_ref, a2_ref, g2_ref, seg_ref, segt_ref,
                      ya_ref, shift_ref, s_ref,
                      r_ref, v_ref, kt_ref, kn_ref, b_ref, lw_ref, cum_ref, y_ref, *, tt, c, t_valid):
    c2x = 2 * c
    ti = pl.program_id(1)
    pairs = range(A_HEADS // 2)
    cols = [slice(p * LANES, (p + 1) * LANES) for p in pairs]
    bf = lambda x: x.astype(BF16)

    @pl.when(ti == 0)
    def _():
        s_ref[...] = s0_ref[...]
        shift_ref[...] = sp_ref[...]

    seg = seg_ref[...]
    segt = segt_ref[...]

    def head_sum(x):
        return _dot_exact_rhs(_dot_exact_rhs(x, seg), segt)

    za = za_ref[...].astype(F32)
    row = lax.broadcasted_iota(jnp.int32, (tt, 1), 0)
    prev = jnp.where(row == 0, shift_ref[...], pltpu.roll(za, 1, axis=0))
    last = (t_valid - 1) % tt
    shift_ref[...] = za[last:last + 1, :]
    mix = za + (prev - za) * mu_ref[...]
    valid = (ti * tt + row) < t_valid
    r = mix[:, 0:A_WIDTH]
    k = mix[:, A_WIDTH:2 * A_WIDTH]
    v = jnp.where(valid, mix[:, 2 * A_WIDTH:3 * A_WIDTH], 0.0)
    xw = w0_ref[...] + _dot(bf(jnp.tanh(mix[:, A_OFF_W:A_OFF_A])), w2_ref[...])
    lw = jnp.where(valid, -_sigmoid(xw) * math.exp(-0.5), 0.0)
    a = _sigmoid(a0_ref[...] + _dot(bf(mix[:, A_OFF_A:A_OFF_G]), a2_ref[...]))
    g = _dot(bf(_sigmoid(mix[:, A_OFF_G:A_PAD])), g2_ref[...])
    kk = k * kk_ref[...]
    kn = jnp.where(valid, kk * lax.rsqrt(jnp.maximum(head_sum(kk * kk), 1e-24)), 0.0)
    kt = jnp.where(valid, k * (1.0 + (a - 1.0) * ka_ref[...]), 0.0)
    bonus = head_sum(r * kt * rk_ref[...]) * v
    ri = lax.broadcasted_iota(jnp.int32, (tt, tt), 0)
    ci = lax.broadcasted_iota(jnp.int32, (tt, tt), 1)
    same = (ri // c) == (ci // c)
    cum_ref[...] = _dot_exact_lhs(jnp.where(same & (ci <= ri), 1.0, 0.0).astype(BF16), lw)
    r_ref[...] = r
    v_ref[...] = v
    kt_ref[...] = kt
    kn_ref[...] = kn
    b_ref[...] = kn * a
    lw_ref[...] = lw

    lane = lax.broadcasted_iota(jnp.int32, (1, LANES), 1)
    m0 = (lane < A_HEAD).astype(F32)
    m1 = 1.0 - m0
    r2 = lax.broadcasted_iota(jnp.int32, (c2x, c2x), 0)
    c2 = lax.broadcasted_iota(jnp.int32, (c2x, c2x), 1)
    strict = (r2 % c) > (c2 % c)
    incl = (r2 % c) >= (c2 % c)

    def stack(x):
        return jnp.concatenate([x * m0, x * m1], axis=0)

    def prepare(ch):
        rows = slice(ch * c, (ch + 1) * c)
        cum = cum_ref[rows, :]
        tot = cum_ref[(ch + 1) * c - 1:(ch + 1) * c, :]
        kt_, b_ = kt_ref[rows, :], b_ref[rows, :]
        p_inv = jnp.exp(-cum)
        p_end = jnp.exp(tot - cum)
        return dict(v=v_ref[rows, :], rh=r_ref[rows, :] * jnp.exp(cum), kap=kn_ref[rows, :] * jnp.exp(cum - lw_ref[rows, :]),
                    kh=kt_ * p_inv, bh=b_ * p_inv, khp=kt_ * p_end, bhp=b_ * p_end, pt=jnp.exp(tot))

    def solve(d):
        vst = [stack(d["v"][:, cols[p]]) for p in pairs]
        s0 = [s_ref[p] for p in pairs]
        lhs = [bf(jnp.concatenate([stack(d["kap"][:, cols[p]]), stack(d["rh"][:, cols[p]])], axis=0)) for p in pairs]
        gram = [_dot_nt(lhs[p], bf(jnp.concatenate([stack(d["kh"][:, cols[p]]), stack(d["bh"][:, cols[p]])], axis=0)))
                for p in pairs]
        from_s0 = [_dot_nt(lhs[p], bf(s0[p])) for p in pairs]
        x = [from_s0[p][:c2x] + _dot(bf(jnp.where(strict, gram[p][:c2x, :c2x], 0.0)), bf(vst[p])) for p in pairs]
        lp = [jnp.where(strict, gram[p][:c2x, c2x:], 0.0) for p in pairs]
        sq = [_dot(bf(lp[p]), bf(jnp.concatenate([lp[p], x[p]], axis=1))) for p in pairs]
        lp = [sq[p][:, :c2x] for p in pairs]
        x = [x[p] - sq[p][:, c2x:] for p in pairs]
        n = 2
        while 2 * n < c:
            sq = [_dot(bf(lp[p]), bf(jnp.concatenate([lp[p], x[p]], axis=1))) for p in pairs]
            lp = [sq[p][:, :c2x] for p in pairs]
            x = [x[p] + sq[p][:, c2x:] for p in pairs]
            n *= 2
        x = [x[p] + _dot(bf(lp[p]), bf(x[p])) for p in pairs]
        ys = []
        for p in pairs:
            tri = jnp.concatenate([jnp.where(incl, gram[p][c2x:, :c2x], 0.0),
                                   jnp.where(incl, -gram[p][c2x:, c2x:], 0.0)], axis=1)
            y = from_s0[p][c2x:] + _dot(bf(tri), bf(jnp.concatenate([vst[p], x[p]], axis=0)))
            ys.append(y[:c] + y[c:])
            decayed = jnp.concatenate([stack(d["khp"][:, cols[p]]), stack(d["bhp"][:, cols[p]])], axis=0)
            s_ref[p] = s0[p] * d["pt"][:, cols[p]] + _dot_tn(bf(jnp.concatenate([vst[p], -x[p]], axis=0)), bf(decayed))
        return jnp.concatenate(ys, axis=1)

    n_chunks = tt // c
    d = prepare(0)
    for ch in range(n_chunks):
        d_next = prepare(ch + 1) if ch + 1 < n_chunks else None
        y_ref[ch * c:(ch + 1) * c, :] = solve(d)
        d = d_next

    y = y_ref[...]
    mean = head_sum(y) * (1.0 / A_HEAD)
    yc = y - mean
    var = head_sum(yc * yc) * (1.0 / A_HEAD)
    yn = yc * lax.rsqrt(var + GN_EPS) * lng_ref[...] + lnb_ref[...]
    ya_ref[...] = ((yn + bonus) * g).astype(ya_ref.dtype)


def _wkv_mix(za, shift_prev, s0, prm, *, nb, t_pad, t_valid, tt, chunk):
    nt = t_pad // tt
    assert t_pad % tt == 0 and tt % chunk == 0 and chunk % SUBLANES == 0
    vec = lambda n: pl.BlockSpec((1, n), lambda i, j: (0, 0))
    full = lambda a: pl.BlockSpec(a.shape, lambda i, j: (0,) * a.ndim)
    state_spec = pl.BlockSpec((None, A_HEADS // 2, LANES, LANES), lambda i, j: (i, 0, 0, 0))
    shift_spec = pl.BlockSpec((None, 1, A_PAD), lambda i, j: (i, 0, 0))
    vecs = [prm[n] for n in ("mu", "w0", "a0", "k_k", "k_a", "r_k", "lnx_g", "lnx_b")]
    mats = [prm[n] for n in ("w2", "a2", "g2", "seg", "segt")]
    return pl.pallas_call(
        functools.partial(_wkv_chunk_kernel, tt=tt, c=chunk, t_valid=t_valid),
        out_shape=(jax.ShapeDtypeStruct((nb * t_pad, A_WIDTH), BF16),
                   jax.ShapeDtypeStruct((nb, 1, A_PAD), F32),
                   jax.ShapeDtypeStruct((nb, A_HEADS // 2, LANES, LANES), F32)),
        grid=(nb, nt),
        in_specs=[pl.BlockSpec((tt, A_PAD), lambda i, j: (i * nt + j, 0)), shift_spec, state_spec]
                 + [vec(x.shape[1]) for x in vecs] + [full(x) for x in mats],
        out_specs=(pl.BlockSpec((tt, A_WIDTH), lambda i, j: (i * nt + j, 0)), shift_spec, state_spec),
        scratch_shapes=[pltpu.VMEM((tt, A_WIDTH), F32)] * 8,
        compiler_params=_cparams(("parallel", "arbitrary")),
        name="wkv_mix",
    )(za, shift_prev, s0, *vecs, *mats)


CONV_HALO = 32


def _conv_kernel(zb_ref, cb_ref, w_ref, b_ref, g_ref, be_ref, o_ref, st_ref, ext_ref, conv_ref,
                 *, tt, t_last):
    ti = pl.program_id(1)
    pad = CONV_HALO - (CONV_W - 1)

    @pl.when(ti == 0)
    def _():
        ext_ref[0:CONV_HALO, :] = cb_ref[...]

    zb = zb_ref[...].astype(F32)
    ext_ref[CONV_HALO:CONV_HALO + tt, :] = zb[:, :B_WIDTH] * _sigmoid(zb[:, B_WIDTH:])
    rb = min(tt, 64)
    for r0 in range(0, tt, rb):
        for c0 in range(0, B_WIDTH, LANES):
            cols = slice(c0, c0 + LANES)
            x = ext_ref[r0:r0 + rb + CONV_HALO, cols]
            n = rb + CONV_HALO
            shifted = [x] + [pltpu.roll(x, n - s, axis=0) for s in range(1, SUBLANES)]
            acc = jnp.broadcast_to(b_ref[:, cols], (rb, LANES))
            for j in range(CONV_W):
                s = (pad + j) % SUBLANES
                base = pad + j - s
                acc = acc + shifted[s][base:base + rb] * w_ref[j:j + 1, cols]
            conv_ref[r0:r0 + rb, cols] = acc
    conv = conv_ref[...]
    mean = jnp.mean(conv, axis=-1, keepdims=True)
    xc = conv - mean
    var = jnp.mean(xc * xc, axis=-1, keepdims=True)
    y = xc * lax.rsqrt(var + LN_EPS) * g_ref[...] + be_ref[...]
    o_ref[...] = (y * _sigmoid(y)).astype(o_ref.dtype)
    st_ref[...] = ext_ref[t_last + pad:t_last + pad + CONV_W - 1, :]
    if tt >= CONV_HALO:
        ext_ref[0:CONV_HALO, :] = ext_ref[tt:tt + CONV_HALO, :]


def _conv_module(zb, conv_buf, prm, *, nb, t_pad, t_valid, tt):
    nt = t_pad // tt
    assert nt == 1 or tt >= CONV_HALO
    vec = pl.BlockSpec((1, B_WIDTH), lambda i, j: (0, 0))
    return pl.pallas_call(
        functools.partial(_conv_kernel, tt=tt, t_last=t_valid - (nt - 1) * tt),
        out_shape=(jax.ShapeDtypeStruct((nb * t_pad, B_WIDTH), BF16),
                   jax.ShapeDtypeStruct((nb, CONV_W - 1, B_WIDTH), F32)),
        grid=(nb, nt),
        in_specs=[pl.BlockSpec((tt, 2 * B_WIDTH), lambda i, j: (i * nt + j, 0)),
                  pl.BlockSpec((None, CONV_HALO, B_WIDTH), lambda i, j: (i, 0, 0)),
                  pl.BlockSpec((CONV_W, B_WIDTH), lambda i, j: (0, 0)), vec, vec, vec],
        out_specs=(pl.BlockSpec((tt, B_WIDTH), lambda i, j: (i * nt + j, 0)),
                   pl.BlockSpec((None, CONV_W - 1, B_WIDTH), lambda i, j: (i, 0, 0))),
        scratch_shapes=[pltpu.VMEM((CONV_HALO + tt, B_WIDTH), F32), pltpu.VMEM((tt, B_WIDTH), F32)],
        compiler_params=_cparams(("parallel", "arbitrary")),
        name="conv_module",
    )(zb, conv_buf, prm["dw_w"], prm["dw_b"], prm["cn_g"], prm["cn_b"])


def _top_blocks(gate, blk, n_sel):
    sel = []
    for _ in range(n_sel):
        m = jnp.max(gate, axis=0, keepdims=True)
        idx = jnp.min(jnp.where(gate == m, blk, gate.shape[0]), axis=0, keepdims=True)
        sel.append(idx)
        gate = jnp.where(blk == idx, -jnp.inf, gate)
    return sel


def _moba_prompt_kernel(q_ref, k_ref, v_ref, o_ref, kb_ref, vt_ref, km_ref, qf_ref, *, t):
    kvh = pl.program_id(1)
    qi = pl.program_id(2)
    nb = t // MOBA_BLOCK
    nq = C_GROUP * Q_BLOCK

    log2e = 1.0 / math.log(2.0)
    n_feat = 3

    def slope2_of(group):
        head = (kvh * C_GROUP + group + 1).astype(F32)
        return jnp.exp(head * (-(8.0 / C_HEADS) * math.log(2.0))) * log2e

    @pl.when(qi == 0)
    def _():
        srow = slope2_of(lax.broadcasted_iota(jnp.int32, (nq, 1), 0) // Q_BLOCK)
        s_hi = srow.astype(BF16).astype(F32)
        s_mid = (srow - s_hi).astype(BF16).astype(F32)
        s_lo = (srow - s_hi - s_mid).astype(BF16).astype(F32)
        part = lax.broadcasted_iota(jnp.int32, (1, LANES), 1)
        qfeat = jnp.where(part % n_feat == 0, s_hi, jnp.where(part % n_feat == 1, s_mid, s_lo))
        qf_ref[...] = jnp.where(part < 2 * n_feat, qfeat, 0.0).astype(BF16)
        km_ref[...] = jnp.zeros_like(km_ref)
        lane = lax.broadcasted_iota(jnp.int32, (MOBA_BLOCK, LANES), 1)
        offset = lax.broadcasted_iota(jnp.int32, (MOBA_BLOCK, LANES), 0).astype(F32)
        for n in range(nb):
            kblk = k_ref[n * MOBA_BLOCK:(n + 1) * MOBA_BLOCK, :]
            feat = jnp.where(lane < n_feat, float(n * MOBA_BLOCK), jnp.where(lane < 2 * n_feat, offset, 0.0))
            kb_ref[n] = jnp.concatenate([kblk, feat], axis=1).astype(BF16)
            vt_ref[n] = v_ref[n * MOBA_BLOCK:(n + 1) * MOBA_BLOCK, :].T.astype(BF16)
            km_ref[n:n + 1, :] = jnp.mean(kblk, axis=0, keepdims=True)

    q = q_ref[...]
    qs = jnp.concatenate([q[:, g * C_HEAD_DIM:(g + 1) * C_HEAD_DIM] for g in range(C_GROUP)], axis=0)
    own = (qi * Q_BLOCK) // MOBA_BLOCK
    blk = lax.broadcasted_iota(jnp.int32, (km_ref.shape[0], 1), 0)
    km_hi, km_lo = _split2(km_ref[...])
    gate2 = _dot_nt(jnp.concatenate([km_hi, km_lo], axis=0), qs.astype(BF16))
    gate = gate2[:km_ref.shape[0]] + gate2[km_ref.shape[0]:]
    gate = jnp.where(blk < own, gate, jnp.where(blk < nb, NEG, -jnp.inf))
    sel = _top_blocks(gate, blk, MOBA_TOPK)
    sel = [jnp.where(own > r, sel[r], -1) for r in range(MOBA_TOPK)]

    col = lax.broadcasted_iota(jnp.int32, (1, nq), 1)
    pos = qi * Q_BLOCK + col % Q_BLOCK
    base = -slope2_of(col // Q_BLOCK) * pos.astype(F32)
    qb = jnp.concatenate([(qs * (log2e / math.sqrt(C_HEAD_DIM))).astype(BF16), qf_ref[...]], axis=1)
    trow = lax.broadcasted_iota(jnp.int32, (MOBA_BLOCK, 1), 0)

    def attend_upto(own_static):
        scores = []
        for n in range(own_static):
            picked = (sel[0] == n) | (sel[1] == n) | (sel[2] == n)
            scores.append(_dot_nt(kb_ref[n], qb) + jnp.where(picked, base, NEG))
        causal = trow <= col % Q_BLOCK
        scores.append(jnp.where(causal, _dot_nt(kb_ref[own_static], qb) + base, NEG))
        m = scores[-1].max(axis=0, keepdims=True)
        for s in scores[:-1]:
            m = jnp.maximum(m, jnp.max(s, axis=0, keepdims=True))
        l = jnp.zeros((1, nq), F32)
        acc = jnp.zeros((C_HEAD_DIM, nq), F32)
        for n, s in enumerate(scores):
            p = jnp.exp2(s - m)
            l = l + jnp.sum(p, axis=0, keepdims=True)
            acc = acc + _dot(vt_ref[n], p.astype(BF16))
        o = acc / l
        for g in range(C_GROUP):
            cols = slice(g * Q_BLOCK, (g + 1) * Q_BLOCK)
            o_ref[:, g * C_HEAD_DIM:(g + 1) * C_HEAD_DIM] = o[:, cols].T.astype(o_ref.dtype)

    for own_static in range(nb):
        pl.when(own == own_static)(functools.partial(attend_upto, own_static))


def _moba_prompt(z, *, nb, t):
    nq = t // Q_BLOCK
    assert t % Q_BLOCK == 0 and t // MOBA_BLOCK <= 256 and MOBA_BLOCK <= 256
    kcol = C_Q_W // C_HEAD_DIM
    vcol = (C_Q_W + C_KV_W) // C_HEAD_DIM
    return pl.pallas_call(
        functools.partial(_moba_prompt_kernel, t=t),
        out_shape=jax.ShapeDtypeStruct((nb * t, C_Q_W), BF16),
        grid=(nb, C_KV_HEADS, nq),
        in_specs=[pl.BlockSpec((Q_BLOCK, C_GROUP * C_HEAD_DIM), lambda b, h, i: (b * nq + i, h)),
                  pl.BlockSpec((t, C_HEAD_DIM), lambda b, h, i: (b, kcol + h)),
                  pl.BlockSpec((t, C_HEAD_DIM), lambda b, h, i: (b, vcol + h))],
        out_specs=pl.BlockSpec((Q_BLOCK, C_GROUP * C_HEAD_DIM), lambda b, h, i: (b * nq + i, h)),
        scratch_shapes=[pltpu.VMEM((t // MOBA_BLOCK, MOBA_BLOCK, C_HEAD_DIM + LANES), BF16),
                        pltpu.VMEM((t // MOBA_BLOCK, C_HEAD_DIM, MOBA_BLOCK), BF16),
                        pltpu.VMEM((-(-t // MOBA_BLOCK // SUBLANES) * SUBLANES, C_HEAD_DIM), F32),
                        pltpu.VMEM((C_GROUP * Q_BLOCK, LANES), BF16)],
        compiler_params=_cparams(("parallel", "parallel", "arbitrary")),
        name="moba_prompt",
    )(z, z, z)


PAGES_PER_STEP = 16 * MOBA_BLOCK // PAGE_SIZE


def _moba_decode_kernel(pt_ref, z_ref, *refs, n_blocks):
    del pt_ref
    npg = PAGES_PER_STEP
    k_refs, v_refs = refs[:npg], refs[npg:2 * npg]
    o_ref = refs[2 * npg]
    m_ref, l_ref, acc_ref, gate_ref = refs[2 * npg + 1:]
    j = pl.program_id(1)
    ppb = MOBA_BLOCK // PAGE_SIZE
    past = n_blocks * MOBA_BLOCK
    scale = 1.0 / math.sqrt(C_HEAD_DIM)
    hrow = lax.broadcasted_iota(jnp.int32, (C_HEADS, 1), 0)
    slope = jnp.exp((hrow + 1).astype(F32) * (-(8.0 / C_HEADS) * math.log(2.0)))

    z = z_ref[...]
    qrows = jnp.concatenate([z[:, h * C_HEAD_DIM:(h + 1) * C_HEAD_DIM] for h in range(C_HEADS)], axis=0)
    qb = qrows.astype(BF16)
    rows_blk = MOBA_BLOCK * C_KV_HEADS
    col = lax.broadcasted_iota(jnp.int32, (1, rows_blk), 1)
    mine = (col % C_KV_HEADS) == (hrow // C_GROUP)
    tab = jnp.where(mine, slope * (col // C_KV_HEADS).astype(F32), NEG)

    for blk in range(npg // ppb):
        n = j * (npg // ppb) + blk
        kblk = jnp.concatenate([k_refs[blk * ppb + i][...] for i in range(ppb)], axis=0)
        vblk = jnp.concatenate([v_refs[blk * ppb + i][...] for i in range(ppb)], axis=0)
        s = _dot_nt(qb, kblk.astype(BF16)) * scale + tab - slope * (past - n * MOBA_BLOCK).astype(F32)
        m = jnp.max(s, axis=-1, keepdims=True)
        p = jnp.exp(s - m)
        m_ref[pl.ds(n, 1)] = m[None]
        l_ref[pl.ds(n, 1)] = jnp.sum(p, axis=-1, keepdims=True)[None]
        acc_ref[pl.ds(n, 1)] = _dot(p.astype(BF16), vblk.astype(BF16))[None]
        ksum = jnp.sum(kblk.reshape(rows_blk // SUBLANES, SUBLANES, C_HEAD_DIM), axis=0)
        kmean = (ksum[:C_KV_HEADS] + ksum[C_KV_HEADS:]) * (1.0 / MOBA_BLOCK)
        kmrows = jnp.concatenate([jnp.broadcast_to(kmean[kvh:kvh + 1], (C_GROUP, C_HEAD_DIM))
                                  for kvh in range(C_KV_HEADS)], axis=0)
        gate_ref[pl.ds(n, 1)] = jnp.sum(qrows * kmrows, axis=-1, keepdims=True)[None]

    @pl.when(j == pl.num_programs(1) - 1)
    def _():
        gates = gate_ref[...]
        ms = m_ref[...]
        bidx = lax.broadcasted_iota(jnp.int32, gates.shape, 0)
        picked = jnp.zeros(gates.shape, jnp.bool_)
        for _ in range(min(MOBA_TOPK, n_blocks)):
            gmax = jnp.max(gates, axis=0, keepdims=True)
            first = jnp.min(jnp.where(gates == gmax, bidx, n_blocks), axis=0, keepdims=True)
            hit = bidx == first
            picked = picked | hit
            gates = jnp.where(hit, -jnp.inf, gates)
        knew = jnp.concatenate(
            [z[:, C_Q_W + kvh * C_HEAD_DIM:C_Q_W + (kvh + 1) * C_HEAD_DIM] for kvh in range(C_KV_HEADS)
             for _ in range(C_GROUP)], axis=0)
        vnew = jnp.concatenate(
            [z[:, C_Q_W + C_KV_W + kvh * C_HEAD_DIM:C_Q_W + C_KV_W + (kvh + 1) * C_HEAD_DIM]
             for kvh in range(C_KV_HEADS) for _ in range(C_GROUP)], axis=0)
        s_own = jnp.sum(qb.astype(F32) * knew.astype(BF16).astype(F32), axis=-1, keepdims=True) * scale
        mtot = jnp.maximum(jnp.max(jnp.where(picked, ms, NEG), axis=0), s_own)
        wgt = jnp.where(picked, jnp.exp(ms - mtot[None]), 0.0)
        p_own = jnp.exp(s_own - mtot)
        l = jnp.sum(wgt * l_ref[...], axis=0) + p_own
        acc = jnp.sum(wgt * acc_ref[...], axis=0) + p_own.astype(BF16).astype(F32) * vnew.astype(BF16).astype(F32)
        o_ref[...] = (acc / l).astype(o_ref.dtype)


def _moba_decode(z, cache_k, cache_v, page_table):
    nb, n_pages = page_table.shape
    assert n_pages % PAGES_PER_STEP == 0 and 2 * C_KV_HEADS == SUBLANES
    n_blocks = n_pages * PAGE_SIZE // MOBA_BLOCK
    steps = n_pages // PAGES_PER_STEP

    def page_spec(i):
        return pl.BlockSpec((None, PAGE_SIZE * C_KV_HEADS, C_HEAD_DIM),
                            lambda b, j, pt: (pt[b, j * PAGES_PER_STEP + i], 0, 0))

    pages = [page_spec(i) for i in range(PAGES_PER_STEP)]
    zw = z.shape[-1]
    return pl.pallas_call(
        functools.partial(_moba_decode_kernel, n_blocks=n_blocks),
        out_shape=jax.ShapeDtypeStruct((nb, C_HEADS, C_HEAD_DIM), BF16),
        grid_spec=pltpu.PrefetchScalarGridSpec(
            num_scalar_prefetch=1,
            grid=(nb, steps),
            in_specs=[pl.BlockSpec((None, 1, zw), lambda b, j, pt: (b, 0, 0))] + pages + pages,
            out_specs=pl.BlockSpec((None, C_HEADS, C_HEAD_DIM), lambda b, j, pt: (b, 0, 0)),
            scratch_shapes=[pltpu.VMEM((n_blocks, C_HEADS, 1), F32), pltpu.VMEM((n_blocks, C_HEADS, 1), F32),
                            pltpu.VMEM((n_blocks, C_HEADS, C_HEAD_DIM), F32),
                            pltpu.VMEM((n_blocks, C_HEADS, 1), F32)]),
        compiler_params=_cparams(("parallel", "arbitrary")),
        name="moba_decode",
    )(page_table, z, *([cache_k] * PAGES_PER_STEP), *([cache_v] * PAGES_PER_STEP))


def _pad_a_cols(x):
    i3 = 3 * A_WIDTH
    z = lambda n: jnp.zeros(x.shape[:-1] + (n,), x.dtype)
    return jnp.concatenate([
        x[..., :i3],
        x[..., i3:i3 + DECAY_RANK], z(LANES - DECAY_RANK),
        x[..., i3 + DECAY_RANK:i3 + DECAY_RANK + AAA_RANK], z(LANES - AAA_RANK),
        x[..., i3 + DECAY_RANK + AAA_RANK:], z(2 * LANES - GATE_RANK)], axis=-1)


def _unpad_a_cols(x):
    return jnp.concatenate([x[..., :A_OFF_W + DECAY_RANK], x[..., A_OFF_A:A_OFF_A + AAA_RANK],
                            x[..., A_OFF_G:A_OFF_G + GATE_RANK]], axis=-1)


def _pad_rows(w, n):
    return jnp.concatenate([w, jnp.zeros((n - w.shape[0],) + w.shape[1:], w.dtype)], axis=0)


def _ab_params(i, w_in_ab, mu_ab, w0, w2, a0, a2, g2, k_k, k_a, r_k, lnx_g, lnx_b, dw_w, dw_b, cn_g, cn_b):
    row = lambda x: x.reshape(1, -1)
    head = jnp.arange(A_WIDTH)[:, None] // A_HEAD == jnp.arange(LANES)[None, :]
    return {
        "w_a": _pad_a_cols(w_in_ab[i][:, :A_COLS]).astype(BF16),
        "w_b": w_in_ab[i][:, A_COLS:].astype(BF16),
        "mu": row(_pad_a_cols(mu_ab[i])),
        "w0": row(w0[i]), "a0": row(a0[i]), "k_k": row(k_k[i]), "k_a": row(k_a[i]), "r_k": row(r_k[i]),
        "lnx_g": row(lnx_g[i]), "lnx_b": row(lnx_b[i]),
        "w2": _pad_rows(w2[i], LANES).astype(BF16), "a2": _pad_rows(a2[i], LANES).astype(BF16),
        "g2": _pad_rows(g2[i], 2 * LANES).astype(BF16),
        "seg": head.astype(BF16), "segt": head.T.astype(BF16),
        "dw_w": dw_w[i], "dw_b": row(dw_b[i]), "cn_g": row(cn_g[i]), "cn_b": row(cn_b[i]),
    }


def _state_to_pairs(s):
    nb = s.shape[0]
    s = s.reshape(nb, A_HEADS // 2, 2, A_HEAD, A_HEAD)
    z = jnp.zeros_like(s[:, :, 0])
    top = jnp.concatenate([s[:, :, 0], z], axis=-1)
    bot = jnp.concatenate([z, s[:, :, 1]], axis=-1)
    return jnp.concatenate([top, bot], axis=-2)


def _pairs_to_state(s):
    nb = s.shape[0]
    return jnp.stack([s[:, :, :A_HEAD, :A_HEAD], s[:, :, A_HEAD:, A_HEAD:]], axis=2).reshape(
        nb, A_HEADS, A_HEAD, A_HEAD)


def _mixer_ab(x, g_pre, shift_prev, wkv0, conv_buf, prm, *, nb, t):
    if t >= WKV_CHUNK:
        assert t % WKV_CHUNK == 0
        t_pad, chunk, tt_a, tt_b, xp = t, WKV_CHUNK, min(t, 256), min(t, 128), x
    else:
        t_pad = chunk = tt_a = tt_b = -(-t // WKV_SHORT_CHUNK) * WKV_SHORT_CHUNK
        xp = jnp.pad(x.reshape(nb, t, -1), ((0, 0), (0, t_pad - t), (0, 0))).reshape(nb * t_pad, -1)
    za = _norm_matmul(xp, g_pre, prm["w_a"], BF16, tn=512)
    zb = _norm_matmul(xp, g_pre, prm["w_b"], BF16, tn=512)
    ya, shift, s_new = _wkv_mix(za, _pad_a_cols(shift_prev)[:, None, :], _state_to_pairs(wkv0), prm,
                                nb=nb, t_pad=t_pad, t_valid=t, tt=tt_a, chunk=chunk)
    hist = jnp.pad(conv_buf, ((0, 0), (CONV_HALO - (CONV_W - 1), 0), (0, 0)))
    yb, conv_new = _conv_module(zb, hist, prm, nb=nb, t_pad=t_pad, t_valid=t, tt=tt_b)
    y = jnp.concatenate([ya, yb], axis=-1)
    if t_pad != t:
        y = y.reshape(nb, t_pad, -1)[:, :t].reshape(nb * t, -1)
    return y, _unpad_a_cols(shift[:, 0]), _pairs_to_state(s_new), conv_new


def _rest_of_layer(x, l, k_arr, v_arr, k_col, v_col, wts, *, nb, t):
    (g_x_pre, g_x_post, w_xq, w_xo, g_ffn_pre, g_ffn_post, w_gu, w_down) = wts
    q = _norm_matmul(x, g_x_pre[l], w_xq, BF16, tn=X_W)
    if k_arr.ndim == 4:
        assert t == 1
        o = _cross_attention_token(q, k_arr, v_arr)
    else:
        tq = min(t, 512)
        assert t % tq == 0 and tq % BF16_ROWS == 0
        o = _cross_attention(q.reshape(nb, t, X_W), k_arr, v_arr, k_col, v_col, tq=tq).reshape(nb * t, X_W)
    x = _matmul_norm_res(o, w_xo, g_x_post[l], x)
    h = _norm_matmul(x, g_ffn_pre[l], w_gu, BF16, tn=512, swiglu_half=D_FF)
    return _matmul_norm_res(h, w_down, g_ffn_post[l], x)


def kernel(x_prompt, x_sample, cache_k, cache_v, state_wkv, state_shift, state_conv, cache_mem_k, cache_mem_v, page_table, mem_prompt, w_in_ab, w_out_ab, mu_ab, w0, w2, a0, a2, g2, k_k, k_a, r_k, lnx_g, lnx_b, dw_w, dw_b, cn_g, cn_b, w_in_c, w_out_c, g_mix_pre, g_mix_post, g_x_pre, g_x_post, g_mem, w_xq, w_xkv, w_xo, g_ffn_pre, g_ffn_post, w_ffn_gu, w_ffn_down):
    bp, tp, d = x_prompt.shape
    bs, ts, _ = x_sample.shape
    depth = g_mix_pre.shape[0]
    xp = x_prompt.reshape(bp * tp, d)
    xs = x_sample.reshape(bs * ts, d)
    mem = mem_prompt.reshape(bp * MEM_LEN, d)
    outs = {n: [] for n in ("kp", "vp", "wkvp", "shp", "cvp", "mkp", "mvp", "ks", "vs", "wkvs", "shs", "cvs")}
    for l in range(depth):
        i = l // 2
        if l % 2 == 0:
            prm = _ab_params(i, w_in_ab, mu_ab, w0, w2, a0, a2, g2, k_k, k_a, r_k, lnx_g, lnx_b,
                             dw_w, dw_b, cn_g, cn_b)
            w_out = w_out_ab[i].astype(BF16)
            fp, shp, wkvp, cvp = _mixer_ab(
                xp, g_mix_pre[l], jnp.zeros((bp, A_COLS), F32), jnp.zeros((bp, A_HEADS, A_HEAD, A_HEAD), F32),
                jnp.zeros((bp, CONV_W - 1, B_WIDTH), F32), prm, nb=bp, t=tp)
            fs, shs, wkvs, cvs = _mixer_ab(xs, g_mix_pre[l], state_shift[i], state_wkv[i], state_conv[i], prm,
                                           nb=bs, t=ts)
            for n, val in (("shp", shp), ("wkvp", wkvp), ("cvp", cvp), ("shs", shs), ("wkvs", wkvs), ("cvs", cvs)):
                outs[n].append(val)
        else:
            w_in = w_in_c[i].astype(BF16)
            w_out = w_out_c[i].astype(BF16)
            zp = _norm_matmul(xp, g_mix_pre[l], w_in, F32, tn=512)
            fp = _moba_prompt(zp, nb=bp, t=tp)
            zs = _norm_matmul(xs, g_mix_pre[l], w_in, F32, tn=512)
            n_phys = cache_k.shape[1]
            page_rows = (-1, PAGE_SIZE * C_KV_HEADS, C_HEAD_DIM)
            fs = _moba_decode(zs.reshape(bs, ts, -1), cache_k.reshape(page_rows), cache_v.reshape(page_rows),
                              page_table + i * n_phys).reshape(bs * ts, C_Q_W)
            kv = lambda z, nb, t, off: z[:, off:off + C_KV_W].reshape(nb, t, C_KV_HEADS, C_HEAD_DIM)
            outs["kp"].append(kv(zp, bp, tp, C_Q_W))
            outs["vp"].append(kv(zp, bp, tp, C_Q_W + C_KV_W))
            outs["ks"].append(kv(zs, bs, ts, C_Q_W))
            outs["vs"].append(kv(zs, bs, ts, C_Q_W + C_KV_W))
        xp = _matmul_norm_res(fp, w_out, g_mix_post[l], xp)
        xs = _matmul_norm_res(fs, w_out, g_mix_post[l], xs)
        mkv = _norm_matmul(mem, g_mem[l], w_xkv[l].astype(BF16), F32, tn=512)
        outs["mkp"].append(mkv[:, :X_W].reshape(bp, MEM_LEN, X_HEADS, X_HEAD_DIM))
        outs["mvp"].append(mkv[:, X_W:].reshape(bp, MEM_LEN, X_HEADS, X_HEAD_DIM))
        wts = (g_x_pre, g_x_post, w_xq[l].astype(BF16), w_xo[l].astype(BF16), g_ffn_pre, g_ffn_post,
               w_ffn_gu[l].astype(BF16), w_ffn_down[l].astype(BF16))
        xp = _rest_of_layer(xp, l, mkv, mkv, 0, 1, wts, nb=bp, t=tp)
        if ts == 1:
            mem_s = (cache_mem_k[l], cache_mem_v[l])
        else:
            mem_s = (cache_mem_k[l].reshape(bs * MEM_LEN, X_W), cache_mem_v[l].reshape(bs * MEM_LEN, X_W))
        xs = _rest_of_layer(xs, l, *mem_s, 0, 0, wts, nb=bs, t=ts)
    st = lambda n: jnp.stack(outs[n])
    return (xp.reshape(bp, tp, d), xs.reshape(bs, ts, d), st("kp"), st("vp"), st("wkvp"), st("shp"), st("cvp"),
            st("mkp"), st("mvp"), st("ks"), st("vs"), st("wkvs"), st("shs"), st("cvs"))
```

```python
import functools
import math

import jax
import jax.numpy as jnp
from jax import lax
from jax.experimental import pallas as pl
from jax.experimental.pallas import tpu as pltpu

F32 = jnp.float32
BF16 = jnp.bfloat16

D_MODEL = 2048
PAGE_SIZE = 128
A_WIDTH = 1024
A_HEAD = 64
A_HEADS = 16
DECAY_RANK = 64
AAA_RANK = 64
GATE_RANK = 160
A_COLS = 3 * A_WIDTH + DECAY_RANK + AAA_RANK + GATE_RANK
GN_EPS = 64e-5
B_WIDTH = 1024
CONV_W = 31
C_HEADS = 16
C_HEAD_DIM = 128
C_KV_HEADS = 4
C_GROUP = 4
C_Q_W = 2048
C_KV_W = 512
MOBA_BLOCK = 256
MOBA_TOPK = 3
Q_BLOCK = MOBA_BLOCK
MEM_LEN = 256
X_HEADS = 4
X_HEAD_DIM = 128
X_W = 512
D_FF = 5632
RMS_EPS = 1e-6
LN_EPS = 1e-5
NEG = -1e30

SUBLANES = 8
MXU_WIDTH = 256
BF16_ROWS = 16
LANES = 128
A_OFF_W = 3 * A_WIDTH
A_OFF_A = A_OFF_W + LANES
A_OFF_G = A_OFF_A + LANES
A_PAD = A_OFF_G + 2 * LANES
WKV_CHUNK = 64
WKV_SHORT_CHUNK = BF16_ROWS
VMEM_LIMIT = 56 * 1024 * 1024


def _cparams(sem):
    return pltpu.CompilerParams(dimension_semantics=sem, vmem_limit_bytes=VMEM_LIMIT)


def _dot(a, b):
    return jnp.dot(a, b, preferred_element_type=F32)


def _dot_nt(a, b):
    return lax.dot_general(a, b, (((1,), (1,)), ((), ())), preferred_element_type=F32)


def _dot_tn(a, b):
    return lax.dot_general(a, b, (((0,), (0,)), ((), ())), preferred_element_type=F32)


def _split2(x):
    hi = x.astype(BF16)
    lo = (x - hi.astype(F32)).astype(BF16)
    return hi, lo


def _dot_x3(dot, a, b):
    ah, al = _split2(a)
    bh, bl = _split2(b)
    return dot(ah, bh) + (dot(ah, bl) + dot(al, bh))


def _dot_exact_rhs(a, b_bf16):
    hi, lo = _split2(a)
    return _dot(hi, b_bf16) + _dot(lo, b_bf16)


def _dot_exact_lhs(a_bf16, b):
    hi, lo = _split2(b)
    return _dot(a_bf16, hi) + _dot(a_bf16, lo)


def _sigmoid(x):
    return 1.0 / (1.0 + jnp.exp(-x))


def _norm_swiglu_kernel(x_ref, g_ref, wg_ref, wu_ref, o_ref, xn_ref):
    x = x_ref[...]
    ms = jnp.mean(x * x, axis=-1, keepdims=True)
    xn_ref[...] = (x * lax.rsqrt(ms + RMS_EPS) * g_ref[...]).astype(BF16)
    for c0 in range(0, o_ref.shape[1], MXU_WIDTH):
        cs = slice(c0, c0 + MXU_WIDTH)
        gate = _dot(xn_ref[...], wg_ref[:, cs])
        o_ref[:, cs] = (gate * _sigmoid(gate) * _dot(xn_ref[...], wu_ref[:, cs])).astype(o_ref.dtype)


def _norm_mm_resident_kernel(x_ref, g_ref, w_ref, o_ref, xn_ref, *, tn):
    x = x_ref[...]
    ms = jnp.mean(x * x, axis=-1, keepdims=True)
    xn_ref[...] = (x * lax.rsqrt(ms + RMS_EPS) * g_ref[...]).astype(BF16)
    for c0 in range(0, o_ref.shape[1], tn):
        o_ref[:, c0:c0 + tn] = _dot(xn_ref[...], w_ref[:, c0:c0 + tn]).astype(o_ref.dtype)


RESIDENT_WEIGHT_BYTES = 16 * 1024 * 1024


def _norm_matmul(x, g, w, out_dtype, *, tn, swiglu_half=None):
    m, d = x.shape
    n = swiglu_half if swiglu_half else w.shape[1]
    if not swiglu_half and w.size * w.dtype.itemsize <= RESIDENT_WEIGHT_BYTES:
        tm = min(m, 512)
        assert m % tm == 0 and n % tn == 0
        return pl.pallas_call(
            functools.partial(_norm_mm_resident_kernel, tn=tn),
            out_shape=jax.ShapeDtypeStruct((m, n), out_dtype),
            grid=(m // tm,),
            in_specs=[pl.BlockSpec((tm, d), lambda i: (i, 0)),
                      pl.BlockSpec((1, d), lambda i: (0, 0)),
                      pl.BlockSpec((d, n), lambda i: (0, 0), pipeline_mode=pl.Buffered(1))],
            out_specs=pl.BlockSpec((tm, n), lambda i: (i, 0)),
            scratch_shapes=[pltpu.VMEM((tm, d), BF16)],
            compiler_params=_cparams(("parallel",)),
            name="norm_matmul_resident",
        )(x, g.reshape(1, d), w)
    assert swiglu_half, "a plain weight larger than RESIDENT_WEIGHT_BYTES is not supported"
    groups = next(k for k in range(1, n + 1) if n % k == 0 and (n // k) % MXU_WIDTH == 0
                  and 2 * d * (n // k) * w.dtype.itemsize <= 3 * RESIDENT_WEIGHT_BYTES // 2)
    tg = n // groups
    tm = min(m, 512)
    assert m % tm == 0
    w_spec = lambda off: pl.BlockSpec((d, tg), lambda j, i: (0, j + off), pipeline_mode=pl.Buffered(1))
    return pl.pallas_call(
        _norm_swiglu_kernel,
        out_shape=jax.ShapeDtypeStruct((m, n), out_dtype),
        grid=(groups, m // tm),
        in_specs=[pl.BlockSpec((tm, d), lambda j, i: (i, 0)),
                  pl.BlockSpec((1, d), lambda j, i: (0, 0)), w_spec(0), w_spec(groups)],
        out_specs=pl.BlockSpec((tm, tg), lambda j, i: (i, j)),
        scratch_shapes=[pltpu.VMEM((tm, d), BF16)],
        compiler_params=_cparams(("parallel", "parallel")),
        name="norm_swiglu",
    )(x, g.reshape(1, d), w, w)


def _mm_norm_res_kernel(x_ref, w_ref, g_ref, r_ref, o_ref):
    y = _dot(x_ref[...], w_ref[...])
    ms = jnp.mean(y * y, axis=-1, keepdims=True)
    o_ref[...] = r_ref[...] + y * lax.rsqrt(ms + RMS_EPS) * g_ref[...]


def _matmul_norm_res(x, w, g, res):
    m, kd = x.shape
    n = w.shape[1]
    tm = min(m, 512 if kd * n * 2 <= 8 * 1024 * 1024 else 256)
    assert m % tm == 0
    return pl.pallas_call(
        _mm_norm_res_kernel,
        out_shape=jax.ShapeDtypeStruct((m, n), F32),
        grid=(m // tm,),
        in_specs=[pl.BlockSpec((tm, kd), lambda i: (i, 0)),
                  pl.BlockSpec((kd, n), lambda i: (0, 0), pipeline_mode=pl.Buffered(1)),
                  pl.BlockSpec((1, n), lambda i: (0, 0)),
                  pl.BlockSpec((tm, n), lambda i: (i, 0))],
        out_specs=pl.BlockSpec((tm, n), lambda i: (i, 0)),
        compiler_params=_cparams(("parallel",)),
        name="matmul_norm_res",
    )(x, w, g.reshape(1, n), res)


def _xattn_kernel(q_ref, k_ref, v_ref, o_ref):
    scale = 1.0 / math.sqrt(X_HEAD_DIM)
    for h in range(X_HEADS):
        sl = slice(h * X_HEAD_DIM, (h + 1) * X_HEAD_DIM)
        s = _dot_nt(q_ref[:, sl], k_ref[:, sl].astype(BF16)) * scale
        p = jnp.exp(s - jnp.max(s, axis=-1, keepdims=True))
        l = jnp.sum(p, axis=-1, keepdims=True)
        o = _dot(p.astype(BF16), v_ref[:, sl].astype(BF16))
        o_ref[:, sl] = (o / l).astype(o_ref.dtype)


def _cross_attention(q, k_arr, v_arr, k_col, v_col, *, tq):
    b, t, _ = q.shape
    assert t % tq == 0
    return pl.pallas_call(
        _xattn_kernel,
        out_shape=jax.ShapeDtypeStruct((b, t, X_W), BF16),
        grid=(b, t // tq),
        in_specs=[pl.BlockSpec((None, tq, X_W), lambda i, j: (i, j, 0)),
                  pl.BlockSpec((MEM_LEN, X_W), lambda i, j: (i, k_col)),
                  pl.BlockSpec((MEM_LEN, X_W), lambda i, j: (i, v_col))],
        out_specs=pl.BlockSpec((None, tq, X_W), lambda i, j: (i, j, 0)),
        compiler_params=_cparams(("parallel", "arbitrary")),
        name="cross_attention",
    )(q, k_arr, v_arr)


XATTN_SEQS_PER_STEP = 4
XATTN_HEAD_ROWS = BF16_ROWS


def _xattn_token_kernel(q_ref, k_ref, v_ref, o_ref):
    rows = MEM_LEN * X_HEADS
    scale = 1.0 / math.sqrt(X_HEAD_DIM)
    col = lax.broadcasted_iota(jnp.int32, (1, rows), 1)
    head = lax.broadcasted_iota(jnp.int32, (XATTN_HEAD_ROWS, 1), 0)
    mine = (col % X_HEADS) == head
    for s in range(XATTN_SEQS_PER_STEP):
        span = slice(s * rows, (s + 1) * rows)
        sc = jnp.where(mine, _dot_nt(q_ref[s], k_ref[span, :].astype(BF16)) * scale, NEG)
        p = jnp.exp(sc - jnp.max(sc, axis=-1, keepdims=True))
        l = jnp.sum(p, axis=-1, keepdims=True)
        o_ref[s] = (_dot(p.astype(BF16), v_ref[span, :].astype(BF16)) / l).astype(o_ref.dtype)


def _cross_attention_token(q, mem_k, mem_v):
    b = q.shape[0]
    assert b % XATTN_SEQS_PER_STEP == 0
    rows = MEM_LEN * X_HEADS
    qh = jnp.pad(q.reshape(b, X_HEADS, X_HEAD_DIM), ((0, 0), (0, XATTN_HEAD_ROWS - X_HEADS), (0, 0)))
    q_spec = pl.BlockSpec((XATTN_SEQS_PER_STEP, XATTN_HEAD_ROWS, X_HEAD_DIM), lambda i: (i, 0, 0))
    kv_spec = pl.BlockSpec((XATTN_SEQS_PER_STEP * rows, X_HEAD_DIM), lambda i: (i, 0))
    o = pl.pallas_call(
        _xattn_token_kernel,
        out_shape=jax.ShapeDtypeStruct((b, XATTN_HEAD_ROWS, X_HEAD_DIM), BF16),
        grid=(b // XATTN_SEQS_PER_STEP,),
        in_specs=[q_spec, kv_spec, kv_spec],
        out_specs=q_spec,
        compiler_params=_cparams(("parallel",)),
        name="cross_attention_token",
    )(qh, mem_k.reshape(b * rows, X_HEAD_DIM), mem_v.reshape(b * rows, X_HEAD_DIM))
    return o[:, :X_HEADS].reshape(b, X_W)


def _wkv_chunk_kernel(za_ref, sp_ref, s0_ref, mu_ref, w0_ref, a0_ref, kk_ref, ka_ref, rk_ref,
                      lng_ref, lnb_ref, w2_ref, a2_ref, g2_ref, seg_ref, segt_ref,
                      ya_ref, shift_ref, s_ref,
                      r_ref, v_ref, kt_ref, kn_ref, b_ref, lw_ref, cum_ref, y_ref, *, tt, c, t_valid):
    c2x = 2 * c
    ti = pl.program_id(1)
    pairs = range(A_HEADS // 2)
    cols = [slice(p * LANES, (p + 1) * LANES) for p in pairs]
    bf = lambda x: x.astype(BF16)

    @pl.when(ti == 0)
    def _():
        s_ref[...] = s0_ref[...]
        shift_ref[...] = sp_ref[...]

    seg = seg_ref[...]
    segt = segt_ref[...]

    def head_sum(x):
        return _dot_exact_rhs(_dot(x.astype(BF16), seg), segt)

    za = za_ref[...].astype(F32)
    row = lax.broadcasted_iota(jnp.int32, (tt, 1), 0)
    prev = jnp.where(row == 0, shift_ref[...], pltpu.roll(za, 1, axis=0))
    last = (t_valid - 1) % tt
    shift_ref[...] = za[last:last + 1, :]
    mix = za + (prev - za) * mu_ref[...]
    valid = (ti * tt + row) < t_valid
    r = mix[:, 0:A_WIDTH]
    k = mix[:, A_WIDTH:2 * A_WIDTH]
    v = jnp.where(valid, mix[:, 2 * A_WIDTH:3 * A_WIDTH], 0.0)
    xw = w0_ref[...] + _dot(bf(jnp.tanh(mix[:, A_OFF_W:A_OFF_A])), w2_ref[...])
    lw = jnp.where(valid, -_sigmoid(xw) * math.exp(-0.5), 0.0)
    a = _sigmoid(a0_ref[...] + _dot(bf(mix[:, A_OFF_A:A_OFF_G]), a2_ref[...]))
    g = _dot(bf(_sigmoid(mix[:, A_OFF_G:A_PAD])), g2_ref[...])
    kk = k * kk_ref[...]
    kn = jnp.where(valid, kk * lax.rsqrt(jnp.maximum(head_sum(kk * kk), 1e-24)), 0.0)
    kt = jnp.where(valid, k * (1.0 + (a - 1.0) * ka_ref[...]), 0.0)
    bonus = head_sum(r * kt * rk_ref[...]) * v
    ri = lax.broadcasted_iota(jnp.int32, (tt, tt), 0)
    ci = lax.broadcasted_iota(jnp.int32, (tt, tt), 1)
    same = (ri // c) == (ci // c)
    cum_ref[...] = _dot_exact_lhs(jnp.where(same & (ci <= ri), 1.0, 0.0).astype(BF16), lw)
    r_ref[...] = r
    v_ref[...] = v
    kt_ref[...] = kt
    kn_ref[...] = kn
    b_ref[...] = kn * a
    lw_ref[...] = lw

    lane = lax.broadcasted_iota(jnp.int32, (1, LANES), 1)
    m0 = (lane < A_HEAD).astype(F32)
    m1 = 1.0 - m0
    r2 = lax.broadcasted_iota(jnp.int32, (c2x, c2x), 0)
    c2 = lax.broadcasted_iota(jnp.int32, (c2x, c2x), 1)
    strict = (r2 % c) > (c2 % c)
    incl = (r2 % c) >= (c2 % c)

    def stack(x):
        return jnp.concatenate([x * m0, x * m1], axis=0)

    def prepare(ch):
        rows = slice(ch * c, (ch + 1) * c)
        cum = cum_ref[rows, :]
        tot = cum_ref[(ch + 1) * c - 1:(ch + 1) * c, :]
        kt_, b_ = kt_ref[rows, :], b_ref[rows, :]
        p_inv = jnp.exp(-cum)
        p_end = jnp.exp(tot - cum)
        return dict(v=v_ref[rows, :], rh=r_ref[rows, :] * jnp.exp(cum), kap=kn_ref[rows, :] * jnp.exp(cum - lw_ref[rows, :]),
                    kh=kt_ * p_inv, bh=b_ * p_inv, khp=kt_ * p_end, bhp=b_ * p_end, pt=jnp.exp(tot))

    def solve(d):
        vst = [stack(d["v"][:, cols[p]]) for p in pairs]
        s0 = [s_ref[p] for p in pairs]
        lhs = [bf(jnp.concatenate([stack(d["kap"][:, cols[p]]), stack(d["rh"][:, cols[p]])], axis=0)) for p in pairs]
        gram = [_dot_nt(lhs[p], bf(jnp.concatenate([stack(d["kh"][:, cols[p]]), stack(d["bh"][:, cols[p]])], axis=0)))
                for p in pairs]
        from_s0 = [_dot_nt(lhs[p], bf(s0[p])) for p in pairs]
        x = [from_s0[p][:c2x] + _dot(bf(jnp.where(strict, gram[p][:c2x, :c2x], 0.0)), bf(vst[p])) for p in pairs]
        lp = [jnp.where(strict, gram[p][:c2x, c2x:], 0.0) for p in pairs]
        sq = [_dot(bf(lp[p]), bf(jnp.concatenate([lp[p], x[p]], axis=1))) for p in pairs]
        lp = [sq[p][:, :c2x] for p in pairs]
        x = [x[p] - sq[p][:, c2x:] for p in pairs]
        n = 2
        while 2 * n < c:
            sq = [_dot(bf(lp[p]), bf(jnp.concatenate([lp[p], x[p]], axis=1))) for p in pairs]
            lp = [sq[p][:, :c2x] for p in pairs]
            x = [x[p] + sq[p][:, c2x:] for p in pairs]
            n *= 2
        x = [x[p] + _dot(bf(lp[p]), bf(x[p])) for p in pairs]
        ys = []
        for p in pairs:
            tri = jnp.concatenate([jnp.where(incl, gram[p][c2x:, :c2x], 0.0),
                                   jnp.where(incl, -gram[p][c2x:, c2x:], 0.0)], axis=1)
            y = from_s0[p][c2x:] + _dot(bf(tri), bf(jnp.concatenate([vst[p], x[p]], axis=0)))
            ys.append(y[:c] + y[c:])
            decayed = jnp.concatenate([stack(d["khp"][:, cols[p]]), stack(d["bhp"][:, cols[p]])], axis=0)
            s_ref[p] = s0[p] * d["pt"][:, cols[p]] + _dot_tn(bf(jnp.concatenate([vst[p], -x[p]], axis=0)), bf(decayed))
        return jnp.concatenate(ys, axis=1)

    n_chunks = tt // c
    d = prepare(0)
    for ch in range(n_chunks):
        d_next = prepare(ch + 1) if ch + 1 < n_chunks else None
        y_ref[ch * c:(ch + 1) * c, :] = solve(d)
        d = d_next

    y = y_ref[...]
    mean = head_sum(y) * (1.0 / A_HEAD)
    yc = y - mean
    var = head_sum(yc * yc) * (1.0 / A_HEAD)
    yn = yc * lax.rsqrt(var + GN_EPS) * lng_ref[...] + lnb_ref[...]
    ya_ref[...] = ((yn + bonus) * g).astype(ya_ref.dtype)


def _wkv_mix(za, shift_prev, s0, prm, *, nb, t_pad, t_valid, tt, chunk):
    nt = t_pad // tt
    assert t_pad % tt == 0 and tt % chunk == 0 and chunk % SUBLANES == 0
    vec = lambda n: pl.BlockSpec((1, n), lambda i, j: (0, 0))
    full = lambda a: pl.BlockSpec(a.shape, lambda i, j: (0,) * a.ndim)
    state_spec = pl.BlockSpec((None, A_HEADS // 2, LANES, LANES), lambda i, j: (i, 0, 0, 0))
    shift_spec = pl.BlockSpec((None, 1, A_PAD), lambda i, j: (i, 0, 0))
    vecs = [prm[n] for n in ("mu", "w0", "a0", "k_k", "k_a", "r_k", "lnx_g", "lnx_b")]
    mats = [prm[n] for n in ("w2", "a2", "g2", "seg", "segt")]
    return pl.pallas_call(
        functools.partial(_wkv_chunk_kernel, tt=tt, c=chunk, t_valid=t_valid),
        out_shape=(jax.ShapeDtypeStruct((nb * t_pad, A_WIDTH), BF16),
                   jax.ShapeDtypeStruct((nb, 1, A_PAD), F32),
                   jax.ShapeDtypeStruct((nb, A_HEADS // 2, LANES, LANES), F32)),
        grid=(nb, nt),
        in_specs=[pl.BlockSpec((tt, A_PAD), lambda i, j: (i * nt + j, 0)), shift_spec, state_spec]
                 + [vec(x.shape[1]) for x in vecs] + [full(x) for x in mats],
        out_specs=(pl.BlockSpec((tt, A_WIDTH), lambda i, j: (i * nt + j, 0)), shift_spec, state_spec),
        scratch_shapes=[pltpu.VMEM((tt, A_WIDTH), F32)] * 8,
        compiler_params=_cparams(("parallel", "arbitrary")),
        name="wkv_mix",
    )(za, shift_prev, s0, *vecs, *mats)


CONV_HALO = 32


def _conv_kernel(zb_ref, cb_ref, w_ref, b_ref, g_ref, be_ref, o_ref, st_ref, ext_ref, conv_ref,
                 *, tt, t_last):
    ti = pl.program_id(1)
    pad = CONV_HALO - (CONV_W - 1)

    @pl.when(ti == 0)
    def _():
        ext_ref[0:CONV_HALO, :] = cb_ref[...]

    zb = zb_ref[...].astype(F32)
    ext_ref[CONV_HALO:CONV_HALO + tt, :] = zb[:, :B_WIDTH] * _sigmoid(zb[:, B_WIDTH:])
    rb = min(tt, 64)
    for r0 in range(0, tt, rb):
        for c0 in range(0, B_WIDTH, LANES):
            cols = slice(c0, c0 + LANES)
            x = ext_ref[r0:r0 + rb + CONV_HALO, cols]
            n = rb + CONV_HALO
            shifted = [x] + [pltpu.roll(x, n - s, axis=0) for s in range(1, SUBLANES)]
            acc = jnp.broadcast_to(b_ref[:, cols], (rb, LANES))
            for j in range(CONV_W):
                s = (pad + j) % SUBLANES
                base = pad + j - s
                acc = acc + shifted[s][base:base + rb] * w_ref[j:j + 1, cols]
            conv_ref[r0:r0 + rb, cols] = acc
    conv = conv_ref[...]
    mean = jnp.mean(conv, axis=-1, keepdims=True)
    xc = conv - mean
    var = jnp.mean(xc * xc, axis=-1, keepdims=True)
    y = xc * lax.rsqrt(var + LN_EPS) * g_ref[...] + be_ref[...]
    o_ref[...] = (y * _sigmoid(y)).astype(o_ref.dtype)
    st_ref[...] = ext_ref[t_last + pad:t_last + pad + CONV_W - 1, :]
    if tt >= CONV_HALO:
        ext_ref[0:CONV_HALO, :] = ext_ref[tt:tt + CONV_HALO, :]


def _conv_module(zb, conv_buf, prm, *, nb, t_pad, t_valid, tt):
    nt = t_pad // tt
    assert nt == 1 or tt >= CONV_HALO
    vec = pl.BlockSpec((1, B_WIDTH), lambda i, j: (0, 0))
    return pl.pallas_call(
        functools.partial(_conv_kernel, tt=tt, t_last=t_valid - (nt - 1) * tt),
        out_shape=(jax.ShapeDtypeStruct((nb * t_pad, B_WIDTH), BF16),
                   jax.ShapeDtypeStruct((nb, CONV_W - 1, B_WIDTH), F32)),
        grid=(nb, nt),
        in_specs=[pl.BlockSpec((tt, 2 * B_WIDTH), lambda i, j: (i * nt + j, 0)),
                  pl.BlockSpec((None, CONV_HALO, B_WIDTH), lambda i, j: (i, 0, 0)),
                  pl.BlockSpec((CONV_W, B_WIDTH), lambda i, j: (0, 0)), vec, vec, vec],
        out_specs=(pl.BlockSpec((tt, B_WIDTH), lambda i, j: (i * nt + j, 0)),
                   pl.BlockSpec((None, CONV_W - 1, B_WIDTH), lambda i, j: (i, 0, 0))),
        scratch_shapes=[pltpu.VMEM((CONV_HALO + tt, B_WIDTH), F32), pltpu.VMEM((tt, B_WIDTH), F32)],
        compiler_params=_cparams(("parallel", "arbitrary")),
        name="conv_module",
    )(zb, conv_buf, prm["dw_w"], prm["dw_b"], prm["cn_g"], prm["cn_b"])


def _top_blocks(gate, blk, n_sel):
    sel = []
    for _ in range(n_sel):
        m = jnp.max(gate, axis=0, keepdims=True)
        idx = jnp.min(jnp.where(gate == m, blk, gate.shape[0]), axis=0, keepdims=True)
        sel.append(idx)
        gate = jnp.where(blk == idx, -jnp.inf, gate)
    return sel


def _moba_prompt_kernel(q_ref, k_ref, v_ref, o_ref, kb_ref, vt_ref, km_ref, qf_ref, *, t):
    kvh = pl.program_id(1)
    qi = pl.program_id(2)
    nb = t // MOBA_BLOCK
    nq = C_GROUP * Q_BLOCK

    log2e = 1.0 / math.log(2.0)
    n_feat = 3

    def slope2_of(group):
        head = (kvh * C_GROUP + group + 1).astype(F32)
        return jnp.exp(head * (-(8.0 / C_HEADS) * math.log(2.0))) * log2e

    @pl.when(qi == 0)
    def _():
        srow = slope2_of(lax.broadcasted_iota(jnp.int32, (nq, 1), 0) // Q_BLOCK)
        s_hi = srow.astype(BF16).astype(F32)
        s_mid = (srow - s_hi).astype(BF16).astype(F32)
        s_lo = (srow - s_hi - s_mid).astype(BF16).astype(F32)
        part = lax.broadcasted_iota(jnp.int32, (1, LANES), 1)
        qfeat = jnp.where(part % n_feat == 0, s_hi, jnp.where(part % n_feat == 1, s_mid, s_lo))
        qf_ref[...] = jnp.where(part < 2 * n_feat, qfeat, 0.0).astype(BF16)
        km_ref[...] = jnp.zeros_like(km_ref)
        lane = lax.broadcasted_iota(jnp.int32, (MOBA_BLOCK, LANES), 1)
        offset = lax.broadcasted_iota(jnp.int32, (MOBA_BLOCK, LANES), 0).astype(F32)
        for n in range(nb):
            kblk = k_ref[n * MOBA_BLOCK:(n + 1) * MOBA_BLOCK, :]
            feat = jnp.where(lane < n_feat, float(n * MOBA_BLOCK), jnp.where(lane < 2 * n_feat, offset, 0.0))
            kb_ref[n] = jnp.concatenate([kblk, feat], axis=1).astype(BF16)
            vt_ref[n] = v_ref[n * MOBA_BLOCK:(n + 1) * MOBA_BLOCK, :].T.astype(BF16)
            km_ref[n:n + 1, :] = jnp.mean(kblk, axis=0, keepdims=True)

    q = q_ref[...]
    qs = jnp.concatenate([q[:, g * C_HEAD_DIM:(g + 1) * C_HEAD_DIM] for g in range(C_GROUP)], axis=0)
    own = (qi * Q_BLOCK) // MOBA_BLOCK
    blk = lax.broadcasted_iota(jnp.int32, (km_ref.shape[0], 1), 0)
    km_hi, km_lo = _split2(km_ref[...])
    gate2 = _dot_nt(jnp.concatenate([km_hi, km_lo], axis=0), qs.astype(BF16))
    gate = gate2[:km_ref.shape[0]] + gate2[km_ref.shape[0]:]
    gate = jnp.where(blk < own, gate, jnp.where(blk < nb, NEG, -jnp.inf))
    sel = _top_blocks(gate, blk, MOBA_TOPK)
    sel = [jnp.where(own > r, sel[r], -1) for r in range(MOBA_TOPK)]

    col = lax.broadcasted_iota(jnp.int32, (1, nq), 1)
    pos = qi * Q_BLOCK + col % Q_BLOCK
    base = -slope2_of(col // Q_BLOCK) * pos.astype(F32)
    qb = jnp.concatenate([(qs * (log2e / math.sqrt(C_HEAD_DIM))).astype(BF16), qf_ref[...]], axis=1)
    trow = lax.broadcasted_iota(jnp.int32, (MOBA_BLOCK, 1), 0)

    def attend_upto(own_static):
        scores = []
        for n in range(own_static):
            picked = (sel[0] == n) | (sel[1] == n) | (sel[2] == n)
            scores.append(_dot_nt(kb_ref[n], qb) + jnp.where(picked, base, NEG))
        causal = trow <= col % Q_BLOCK
        scores.append(jnp.where(causal, _dot_nt(kb_ref[own_static], qb) + base, NEG))
        m = scores[-1].max(axis=0, keepdims=True)
        for s in scores[:-1]:
            m = jnp.maximum(m, jnp.max(s, axis=0, keepdims=True))
        l = jnp.zeros((1, nq), F32)
        acc = jnp.zeros((C_HEAD_DIM, nq), F32)
        for n, s in enumerate(scores):
            p = jnp.exp2(s - m)
            l = l + jnp.sum(p, axis=0, keepdims=True)
            acc = acc + _dot(vt_ref[n], p.astype(BF16))
        o = acc / l
        for g in range(C_GROUP):
            cols = slice(g * Q_BLOCK, (g + 1) * Q_BLOCK)
            o_ref[:, g * C_HEAD_DIM:(g + 1) * C_HEAD_DIM] = o[:, cols].T.astype(o_ref.dtype)

    for own_static in range(nb):
        pl.when(own == own_static)(functools.partial(attend_upto, own_static))


def _moba_prompt(z, *, nb, t):
    nq = t // Q_BLOCK
    assert t % Q_BLOCK == 0 and t // MOBA_BLOCK <= 256 and MOBA_BLOCK <= 256
    kcol = C_Q_W // C_HEAD_DIM
    vcol = (C_Q_W + C_KV_W) // C_HEAD_DIM
    return pl.pallas_call(
        functools.partial(_moba_prompt_kernel, t=t),
        out_shape=jax.ShapeDtypeStruct((nb * t, C_Q_W), BF16),
        grid=(nb, C_KV_HEADS, nq),
        in_specs=[pl.BlockSpec((Q_BLOCK, C_GROUP * C_HEAD_DIM), lambda b, h, i: (b * nq + i, h)),
                  pl.BlockSpec((t, C_HEAD_DIM), lambda b, h, i: (b, kcol + h)),
                  pl.BlockSpec((t, C_HEAD_DIM), lambda b, h, i: (b, vcol + h))],
        out_specs=pl.BlockSpec((Q_BLOCK, C_GROUP * C_HEAD_DIM), lambda b, h, i: (b * nq + i, h)),
        scratch_shapes=[pltpu.VMEM((t // MOBA_BLOCK, MOBA_BLOCK, C_HEAD_DIM + LANES), BF16),
                        pltpu.VMEM((t // MOBA_BLOCK, C_HEAD_DIM, MOBA_BLOCK), BF16),
                        pltpu.VMEM((-(-t // MOBA_BLOCK // SUBLANES) * SUBLANES, C_HEAD_DIM), F32),
                        pltpu.VMEM((C_GROUP * Q_BLOCK, LANES), BF16)],
        compiler_params=_cparams(("parallel", "parallel", "arbitrary")),
        name="moba_prompt",
    )(z, z, z)


PAGES_PER_STEP = 16 * MOBA_BLOCK // PAGE_SIZE


def _moba_decode_kernel(pt_ref, z_ref, *refs, n_blocks):
    del pt_ref
    npg = PAGES_PER_STEP
    k_refs, v_refs = refs[:npg], refs[npg:2 * npg]
    o_ref = refs[2 * npg]
    m_ref, l_ref, acc_ref, gate_ref = refs[2 * npg + 1:]
    j = pl.program_id(1)
    ppb = MOBA_BLOCK // PAGE_SIZE
    past = n_blocks * MOBA_BLOCK
    scale = 1.0 / math.sqrt(C_HEAD_DIM)
    hrow = lax.broadcasted_iota(jnp.int32, (C_HEADS, 1), 0)
    slope = jnp.exp((hrow + 1).astype(F32) * (-(8.0 / C_HEADS) * math.log(2.0)))

    z = z_ref[...]
    qrows = jnp.concatenate([z[:, h * C_HEAD_DIM:(h + 1) * C_HEAD_DIM] for h in range(C_HEADS)], axis=0)
    qb = qrows.astype(BF16)
    rows_blk = MOBA_BLOCK * C_KV_HEADS
    col = lax.broadcasted_iota(jnp.int32, (1, rows_blk), 1)
    mine = (col % C_KV_HEADS) == (hrow // C_GROUP)
    tab = jnp.where(mine, slope * (col // C_KV_HEADS).astype(F32), NEG)

    for blk in range(npg // ppb):
        n = j * (npg // ppb) + blk
        kblk = jnp.concatenate([k_refs[blk * ppb + i][...] for i in range(ppb)], axis=0)
        vblk = jnp.concatenate([v_refs[blk * ppb + i][...] for i in range(ppb)], axis=0)
        s = _dot_nt(qb, kblk.astype(BF16)) * scale + tab - slope * (past - n * MOBA_BLOCK).astype(F32)
        m = jnp.max(s, axis=-1, keepdims=True)
        p = jnp.exp(s - m)
        m_ref[pl.ds(n, 1)] = m[None]
        l_ref[pl.ds(n, 1)] = jnp.sum(p, axis=-1, keepdims=True)[None]
        acc_ref[pl.ds(n, 1)] = _dot(p.astype(BF16), vblk.astype(BF16))[None]
        ksum = jnp.sum(kblk.reshape(rows_blk // SUBLANES, SUBLANES, C_HEAD_DIM), axis=0)
        kmean = (ksum[:C_KV_HEADS] + ksum[C_KV_HEADS:]) * (1.0 / MOBA_BLOCK)
        kmrows = jnp.concatenate([jnp.broadcast_to(kmean[kvh:kvh + 1], (C_GROUP, C_HEAD_DIM))
                                  for kvh in range(C_KV_HEADS)], axis=0)
        gate_ref[pl.ds(n, 1)] = jnp.sum(qrows * kmrows, axis=-1, keepdims=True)[None]

    @pl.when(j == pl.num_programs(1) - 1)
    def _():
        gates = gate_ref[...]
        ms = m_ref[...]
        bidx = lax.broadcasted_iota(jnp.int32, gates.shape, 0)
        picked = jnp.zeros(gates.shape, jnp.bool_)
        for _ in range(min(MOBA_TOPK, n_blocks)):
            gmax = jnp.max(gates, axis=0, keepdims=True)
            first = jnp.min(jnp.where(gates == gmax, bidx, n_blocks), axis=0, keepdims=True)
            hit = bidx == first
            picked = picked | hit
            gates = jnp.where(hit, -jnp.inf, gates)
        knew = jnp.concatenate(
            [z[:, C_Q_W + kvh * C_HEAD_DIM:C_Q_W + (kvh + 1) * C_HEAD_DIM] for kvh in range(C_KV_HEADS)
             for _ in range(C_GROUP)], axis=0)
        vnew = jnp.concatenate(
            [z[:, C_Q_W + C_KV_W + kvh * C_HEAD_DIM:C_Q_W + C_KV_W + (kvh + 1) * C_HEAD_DIM]
             for kvh in range(C_KV_HEADS) for _ in range(C_GROUP)], axis=0)
        s_own = jnp.sum(qb.astype(F32) * knew.astype(BF16).astype(F32), axis=-1, keepdims=True) * scale
        mtot = jnp.maximum(jnp.max(jnp.where(picked, ms, NEG), axis=0), s_own)
        wgt = jnp.where(picked, jnp.exp(ms - mtot[None]), 0.0)
        p_own = jnp.exp(s_own - mtot)
        l = jnp.sum(wgt * l_ref[...], axis=0) + p_own
        acc = jnp.sum(wgt * acc_ref[...], axis=0) + p_own.astype(BF16).astype(F32) * vnew.astype(BF16).astype(F32)
        o_ref[...] = (acc / l).astype(o_ref.dtype)


def _moba_decode(z, cache_k, cache_v, page_table):
    nb, n_pages = page_table.shape
    assert n_pages % PAGES_PER_STEP == 0 and 2 * C_KV_HEADS == SUBLANES
    n_blocks = n_pages * PAGE_SIZE // MOBA_BLOCK
    steps = n_pages // PAGES_PER_STEP

    def page_spec(i):
        return pl.BlockSpec((None, PAGE_SIZE * C_KV_HEADS, C_HEAD_DIM),
                            lambda b, j, pt: (pt[b, j * PAGES_PER_STEP + i], 0, 0))

    pages = [page_spec(i) for i in range(PAGES_PER_STEP)]
    zw = z.shape[-1]
    return pl.pallas_call(
        functools.partial(_moba_decode_kernel, n_blocks=n_blocks),
        out_shape=jax.ShapeDtypeStruct((nb, C_HEADS, C_HEAD_DIM), BF16),
        grid_spec=pltpu.PrefetchScalarGridSpec(
            num_scalar_prefetch=1,
            grid=(nb, steps),
            in_specs=[pl.BlockSpec((None, 1, zw), lambda b, j, pt: (b, 0, 0))] + pages + pages,
            out_specs=pl.BlockSpec((None, C_HEADS, C_HEAD_DIM), lambda b, j, pt: (b, 0, 0)),
            scratch_shapes=[pltpu.VMEM((n_blocks, C_HEADS, 1), F32), pltpu.VMEM((n_blocks, C_HEADS, 1), F32),
                            pltpu.VMEM((n_blocks, C_HEADS, C_HEAD_DIM), F32),
                            pltpu.VMEM((n_blocks, C_HEADS, 1), F32)]),
        compiler_params=_cparams(("parallel", "arbitrary")),
        name="moba_decode",
    )(page_table, z, *([cache_k] * PAGES_PER_STEP), *([cache_v] * PAGES_PER_STEP))


def _pad_a_cols(x):
    i3 = 3 * A_WIDTH
    z = lambda n: jnp.zeros(x.shape[:-1] + (n,), x.dtype)
    return jnp.concatenate([
        x[..., :i3],
        x[..., i3:i3 + DECAY_RANK], z(LANES - DECAY_RANK),
        x[..., i3 + DECAY_RANK:i3 + DECAY_RANK + AAA_RANK], z(LANES - AAA_RANK),
        x[..., i3 + DECAY_RANK + AAA_RANK:], z(2 * LANES - GATE_RANK)], axis=-1)


def _unpad_a_cols(x):
    return jnp.concatenate([x[..., :A_OFF_W + DECAY_RANK], x[..., A_OFF_A:A_OFF_A + AAA_RANK],
                            x[..., A_OFF_G:A_OFF_G + GATE_RANK]], axis=-1)


def _pad_rows(w, n):
    return jnp.concatenate([w, jnp.zeros((n - w.shape[0],) + w.shape[1:], w.dtype)], axis=0)


def _ab_params(i, w_in_ab, mu_ab, w0, w2, a0, a2, g2, k_k, k_a, r_k, lnx_g, lnx_b, dw_w, dw_b, cn_g, cn_b):
    row = lambda x: x.reshape(1, -1)
    head = jnp.arange(A_WIDTH)[:, None] // A_HEAD == jnp.arange(LANES)[None, :]
    return {
        "w_a": _pad_a_cols(w_in_ab[i][:, :A_COLS]).astype(BF16),
        "w_b": w_in_ab[i][:, A_COLS:].astype(BF16),
        "mu": row(_pad_a_cols(mu_ab[i])),
        "w0": row(w0[i]), "a0": row(a0[i]), "k_k": row(k_k[i]), "k_a": row(k_a[i]), "r_k": row(r_k[i]),
        "lnx_g": row(lnx_g[i]), "lnx_b": row(lnx_b[i]),
        "w2": _pad_rows(w2[i], LANES).astype(BF16), "a2": _pad_rows(a2[i], LANES).astype(BF16),
        "g2": _pad_rows(g2[i], 2 * LANES).astype(BF16),
        "seg": head.astype(BF16), "segt": head.T.astype(BF16),
        "dw_w": dw_w[i], "dw_b": row(dw_b[i]), "cn_g": row(cn_g[i]), "cn_b": row(cn_b[i]),
    }


def _state_to_pairs(s):
    nb = s.shape[0]
    s = s.reshape(nb, A_HEADS // 2, 2, A_HEAD, A_HEAD)
    z = jnp.zeros_like(s[:, :, 0])
    top = jnp.concatenate([s[:, :, 0], z], axis=-1)
    bot = jnp.concatenate([z, s[:, :, 1]], axis=-1)
    return jnp.concatenate([top, bot], axis=-2)


def _pairs_to_state(s):
    nb = s.shape[0]
    return jnp.stack([s[:, :, :A_HEAD, :A_HEAD], s[:, :, A_HEAD:, A_HEAD:]], axis=2).reshape(
        nb, A_HEADS, A_HEAD, A_HEAD)


def _mixer_ab(x, g_pre, shift_prev, wkv0, conv_buf, prm, *, nb, t):
    if t >= WKV_CHUNK:
        assert t % WKV_CHUNK == 0
        t_pad, chunk, tt_a, tt_b, xp = t, WKV_CHUNK, min(t, 256), min(t, 128), x
    else:
        t_pad = chunk = tt_a = tt_b = -(-t // WKV_SHORT_CHUNK) * WKV_SHORT_CHUNK
        xp = jnp.pad(x.reshape(nb, t, -1), ((0, 0), (0, t_pad - t), (0, 0))).reshape(nb * t_pad, -1)
    za = _norm_matmul(xp, g_pre, prm["w_a"], BF16, tn=512)
    zb = _norm_matmul(xp, g_pre, prm["w_b"], BF16, tn=512)
    ya, shift, s_new = _wkv_mix(za, _pad_a_cols(shift_prev)[:, None, :], _state_to_pairs(wkv0), prm,
                                nb=nb, t_pad=t_pad, t_valid=t, tt=tt_a, chunk=chunk)
    hist = jnp.pad(conv_buf, ((0, 0), (CONV_HALO - (CONV_W - 1), 0), (0, 0)))
    yb, conv_new = _conv_module(zb, hist, prm, nb=nb, t_pad=t_pad, t_valid=t, tt=tt_b)
    y = jnp.concatenate([ya, yb], axis=-1)
    if t_pad != t:
        y = y.reshape(nb, t_pad, -1)[:, :t].reshape(nb * t, -1)
    return y, _unpad_a_cols(shift[:, 0]), _pairs_to_state(s_new), conv_new


def _rest_of_layer(x, l, k_arr, v_arr, k_col, v_col, wts, *, nb, t):
    (g_x_pre, g_x_post, w_xq, w_xo, g_ffn_pre, g_ffn_post, w_gu, w_down) = wts
    q = _norm_matmul(x, g_x_pre[l], w_xq, BF16, tn=X_W)
    if k_arr.ndim == 4:
        assert t == 1
        o = _cross_attention_token(q, k_arr, v_arr)
    else:
        tq = min(t, 512)
        assert t % tq == 0 and tq % BF16_ROWS == 0
        o = _cross_attention(q.reshape(nb, t, X_W), k_arr, v_arr, k_col, v_col, tq=tq).reshape(nb * t, X_W)
    x = _matmul_norm_res(o, w_xo, g_x_post[l], x)
    h = _norm_matmul(x, g_ffn_pre[l], w_gu, BF16, tn=512, swiglu_half=D_FF)
    return _matmul_norm_res(h, w_down, g_ffn_post[l], x)


def kernel(x_prompt, x_sample, cache_k, cache_v, state_wkv, state_shift, state_conv, cache_mem_k, cache_mem_v, page_table, mem_prompt, w_in_ab, w_out_ab, mu_ab, w0, w2, a0, a2, g2, k_k, k_a, r_k, lnx_g, lnx_b, dw_w, dw_b, cn_g, cn_b, w_in_c, w_out_c, g_mix_pre, g_mix_post, g_x_pre, g_x_post, g_mem, w_xq, w_xkv, w_xo, g_ffn_pre, g_ffn_post, w_ffn_gu, w_ffn_down):
    bp, tp, d = x_prompt.shape
    bs, ts, _ = x_sample.shape
    depth = g_mix_pre.shape[0]
    xp = x_prompt.reshape(bp * tp, d)
    xs = x_sample.reshape(bs * ts, d)
    mem = mem_prompt.reshape(bp * MEM_LEN, d)
    outs = {n: [] for n in ("kp", "vp", "wkvp", "shp", "cvp", "mkp", "mvp", "ks", "vs", "wkvs", "shs", "cvs")}
    for l in range(depth):
        i = l // 2
        if l % 2 == 0:
            prm = _ab_params(i, w_in_ab, mu_ab, w0, w2, a0, a2, g2, k_k, k_a, r_k, lnx_g, lnx_b,
                             dw_w, dw_b, cn_g, cn_b)
            w_out = w_out_ab[i].astype(BF16)
            fp, shp, wkvp, cvp = _mixer_ab(
                xp, g_mix_pre[l], jnp.zeros((bp, A_COLS), F32), jnp.zeros((bp, A_HEADS, A_HEAD, A_HEAD), F32),
                jnp.zeros((bp, CONV_W - 1, B_WIDTH), F32), prm, nb=bp, t=tp)
            fs, shs, wkvs, cvs = _mixer_ab(xs, g_mix_pre[l], state_shift[i], state_wkv[i], state_conv[i], prm,
                                           nb=bs, t=ts)
            for n, val in (("shp", shp), ("wkvp", wkvp), ("cvp", cvp), ("shs", shs), ("wkvs", wkvs), ("cvs", cvs)):
                outs[n].append(val)
        else:
            w_in = w_in_c[i].astype(BF16)
            w_out = w_out_c[i].astype(BF16)
            zp = _norm_matmul(xp, g_mix_pre[l], w_in, F32, tn=512)
            fp = _moba_prompt(zp, nb=bp, t=tp)
            zs = _norm_matmul(xs, g_mix_pre[l], w_in, F32, tn=512)
            n_phys = cache_k.shape[1]
            page_rows = (-1, PAGE_SIZE * C_KV_HEADS, C_HEAD_DIM)
            fs = _moba_decode(zs.reshape(bs, ts, -1), cache_k.reshape(page_rows), cache_v.reshape(page_rows),
                              page_table + i * n_phys).reshape(bs * ts, C_Q_W)
            kv = lambda z, nb, t, off: z[:, off:off + C_KV_W].reshape(nb, t, C_KV_HEADS, C_HEAD_DIM)
            outs["kp"].append(kv(zp, bp, tp, C_Q_W))
            outs["vp"].append(kv(zp, bp, tp, C_Q_W + C_KV_W))
            outs["ks"].append(kv(zs, bs, ts, C_Q_W))
            outs["vs"].append(kv(zs, bs, ts, C_Q_W + C_KV_W))
        xp = _matmul_norm_res(fp, w_out, g_mix_post[l], xp)
        xs = _matmul_norm_res(fs, w_out, g_mix_post[l], xs)
        mkv = _norm_matmul(mem, g_mem[l], w_xkv[l].astype(BF16), F32, tn=512)
        outs["mkp"].append(mkv[:, :X_W].reshape(bp, MEM_LEN, X_HEADS, X_HEAD_DIM))
        outs["mvp"].append(mkv[:, X_W:].reshape(bp, MEM_LEN, X_HEADS, X_HEAD_DIM))
        wts = (g_x_pre, g_x_post, w_xq[l].astype(BF16), w_xo[l].astype(BF16), g_ffn_pre, g_ffn_post,
               w_ffn_gu[l].astype(BF16), w_ffn_down[l].astype(BF16))
        xp = _rest_of_layer(xp, l, mkv, mkv, 0, 1, wts, nb=bp, t=tp)
        if ts == 1:
            mem_s = (cache_mem_k[l], cache_mem_v[l])
        else:
            mem_s = (cache_mem_k[l].reshape(bs * MEM_LEN, X_W), cache_mem_v[l].reshape(bs * MEM_LEN, X_W))
        xs = _rest_of_layer(xs, l, *mem_s, 0, 0, wts, nb=bs, t=ts)
    st = lambda n: jnp.stack(outs[n])
    return (xp.reshape(bp, tp, d), xs.reshape(bs, ts, d), st("kp"), st("vp"), st("wkvp"), st("shp"), st("cvp"),
            st("mkp"), st("mvp"), st("ks"), st("vs"), st("wkvs"), st("shs"), st("cvs"))
```

```python
import functools
import math

import jax
import jax.numpy as jnp
from jax import lax
from jax.experimental import pallas as pl
from jax.experimental.pallas import tpu as pltpu

F32 = jnp.float32
BF16 = jnp.bfloat16

D_MODEL = 2048
PAGE_SIZE = 128
A_WIDTH = 1024
A_HEAD = 64
A_HEADS = 16
DECAY_RANK = 64
AAA_RANK = 64
GATE_RANK = 160
A_COLS = 3 * A_WIDTH + DECAY_RANK + AAA_RANK + GATE_RANK
GN_EPS = 64e-5
B_WIDTH = 1024
CONV_W = 31
C_HEADS = 16
C_HEAD_DIM = 128
C_KV_HEADS = 4
C_GROUP = 4
C_Q_W = 2048
C_KV_W = 512
MOBA_BLOCK = 256
MOBA_TOPK = 3
Q_BLOCK = MOBA_BLOCK
MEM_LEN = 256
X_HEADS = 4
X_HEAD_DIM = 128
X_W = 512
D_FF = 5632
RMS_EPS = 1e-6
LN_EPS = 1e-5
NEG = -1e30

SUBLANES = 8
MXU_WIDTH = 256
BF16_ROWS = 16
LANES = 128
A_OFF_W = 3 * A_WIDTH
A_OFF_A = A_OFF_W + LANES
A_OFF_G = A_OFF_A + LANES
A_PAD = A_OFF_G + 2 * LANES
WKV_CHUNK = 64
WKV_SHORT_CHUNK = BF16_ROWS
VMEM_LIMIT = 56 * 1024 * 1024


def _cparams(sem):
    return pltpu.CompilerParams(dimension_semantics=sem, vmem_limit_bytes=VMEM_LIMIT)


def _dot(a, b):
    return jnp.dot(a, b, preferred_element_type=F32)


def _dot_nt(a, b):
    return lax.dot_general(a, b, (((1,), (1,)), ((), ())), preferred_element_type=F32)


def _dot_tn(a, b):
    return lax.dot_general(a, b, (((0,), (0,)), ((), ())), preferred_element_type=F32)


def _split2(x):
    hi = x.astype(BF16)
    lo = (x - hi.astype(F32)).astype(BF16)
    return hi, lo


def _dot_x3(dot, a, b):
    ah, al = _split2(a)
    bh, bl = _split2(b)
    return dot(ah, bh) + (dot(ah, bl) + dot(al, bh))


def _dot_exact_rhs(a, b_bf16):
    hi, lo = _split2(a)
    return _dot(hi, b_bf16) + _dot(lo, b_bf16)


def _dot_exact_lhs(a_bf16, b):
    hi, lo = _split2(b)
    return _dot(a_bf16, hi) + _dot(a_bf16, lo)


def _sigmoid(x):
    return 1.0 / (1.0 + jnp.exp(-x))


def _norm_swiglu_kernel(x_ref, g_ref, wg_ref, wu_ref, o_ref, xn_ref):
    x = x_ref[...]
    ms = jnp.mean(x * x, axis=-1, keepdims=True)
    xn_ref[...] = (x * lax.rsqrt(ms + RMS_EPS) * g_ref[...]).astype(BF16)
    for c0 in range(0, o_ref.shape[1], MXU_WIDTH):
        cs = slice(c0, c0 + MXU_WIDTH)
        gate = _dot(xn_ref[...], wg_ref[:, cs])
        o_ref[:, cs] = (gate * _sigmoid(gate) * _dot(xn_ref[...], wu_ref[:, cs])).astype(o_ref.dtype)


def _norm_mm_resident_kernel(x_ref, g_ref, w_ref, o_ref, xn_ref, *, tn):
    x = x_ref[...]
    ms = jnp.mean(x * x, axis=-1, keepdims=True)
    xn_ref[...] = (x * lax.rsqrt(ms + RMS_EPS) * g_ref[...]).astype(BF16)
    for c0 in range(0, o_ref.shape[1], tn):
        o_ref[:, c0:c0 + tn] = _dot(xn_ref[...], w_ref[:, c0:c0 + tn]).astype(o_ref.dtype)


RESIDENT_WEIGHT_BYTES = 16 * 1024 * 1024


def _norm_matmul(x, g, w, out_dtype, *, tn, swiglu_half=None):
    m, d = x.shape
    n = swiglu_half if swiglu_half else w.shape[1]
    if not swiglu_half and w.size * w.dtype.itemsize <= RESIDENT_WEIGHT_BYTES:
        tm = min(m, 1024 if n <= 1024 else 512)
        assert m % tm == 0 and n % tn == 0
        return pl.pallas_call(
            functools.partial(_norm_mm_resident_kernel, tn=tn),
            out_shape=jax.ShapeDtypeStruct((m, n), out_dtype),
            grid=(m // tm,),
            in_specs=[pl.BlockSpec((tm, d), lambda i: (i, 0)),
                      pl.BlockSpec((1, d), lambda i: (0, 0)),
                      pl.BlockSpec((d, n), lambda i: (0, 0), pipeline_mode=pl.Buffered(1))],
            out_specs=pl.BlockSpec((tm, n), lambda i: (i, 0)),
            scratch_shapes=[pltpu.VMEM((tm, d), BF16)],
            compiler_params=_cparams(("parallel",)),
            name="norm_matmul_resident",
        )(x, g.reshape(1, d), w)
    assert swiglu_half, "a plain weight larger than RESIDENT_WEIGHT_BYTES is not supported"
    groups = next(k for k in range(1, n + 1) if n % k == 0 and (n // k) % MXU_WIDTH == 0
                  and 2 * d * (n // k) * w.dtype.itemsize <= 3 * RESIDENT_WEIGHT_BYTES // 2)
    tg = n // groups
    tm = min(m, 512)
    assert m % tm == 0
    w_spec = lambda off: pl.BlockSpec((d, tg), lambda j, i: (0, j + off), pipeline_mode=pl.Buffered(1))
    return pl.pallas_call(
        _norm_swiglu_kernel,
        out_shape=jax.ShapeDtypeStruct((m, n), out_dtype),
        grid=(groups, m // tm),
        in_specs=[pl.BlockSpec((tm, d), lambda j, i: (i, 0)),
                  pl.BlockSpec((1, d), lambda j, i: (0, 0)), w_spec(0), w_spec(groups)],
        out_specs=pl.BlockSpec((tm, tg), lambda j, i: (i, j)),
        scratch_shapes=[pltpu.VMEM((tm, d), BF16)],
        compiler_params=_cparams(("parallel", "parallel")),
        name="norm_swiglu",
    )(x, g.reshape(1, d), w, w)


def _mm_norm_res_kernel(x_ref, w_ref, g_ref, r_ref, o_ref):
    y = _dot(x_ref[...], w_ref[...])
    ms = jnp.mean(y * y, axis=-1, keepdims=True)
    o_ref[...] = r_ref[...] + y * lax.rsqrt(ms + RMS_EPS) * g_ref[...]


def _matmul_norm_res(x, w, g, res):
    m, kd = x.shape
    n = w.shape[1]
    tm = min(m, 512 if kd * n * 2 <= 8 * 1024 * 1024 else 256)
    assert m % tm == 0
    return pl.pallas_call(
        _mm_norm_res_kernel,
        out_shape=jax.ShapeDtypeStruct((m, n), F32),
        grid=(m // tm,),
        in_specs=[pl.BlockSpec((tm, kd), lambda i: (i, 0)),
                  pl.BlockSpec((kd, n), lambda i: (0, 0), pipeline_mode=pl.Buffered(1)),
                  pl.BlockSpec((1, n), lambda i: (0, 0)),
                  pl.BlockSpec((tm, n), lambda i: (i, 0))],
        out_specs=pl.BlockSpec((tm, n), lambda i: (i, 0)),
        compiler_params=_cparams(("parallel",)),
        name="matmul_norm_res",
    )(x, w, g.reshape(1, n), res)


def _xattn_kernel(q_ref, k_ref, v_ref, o_ref):
    scale = 1.0 / math.sqrt(X_HEAD_DIM)
    for h in range(X_HEADS):
        sl = slice(h * X_HEAD_DIM, (h + 1) * X_HEAD_DIM)
        s = _dot_nt(q_ref[:, sl], k_ref[:, sl].astype(BF16)) * scale
        p = jnp.exp(s - jnp.max(s, axis=-1, keepdims=True))
        l = jnp.sum(p, axis=-1, keepdims=True)
        o = _dot(p.astype(BF16), v_ref[:, sl].astype(BF16))
        o_ref[:, sl] = (o / l).astype(o_ref.dtype)


def _cross_attention(q, k_arr, v_arr, k_col, v_col, *, tq):
    b, t, _ = q.shape
    assert t % tq == 0
    return pl.pallas_call(
        _xattn_kernel,
        out_shape=jax.ShapeDtypeStruct((b, t, X_W), BF16),
        grid=(b, t // tq),
        in_specs=[pl.BlockSpec((None, tq, X_W), lambda i, j: (i, j, 0)),
                  pl.BlockSpec((MEM_LEN, X_W), lambda i, j: (i, k_col)),
                  pl.BlockSpec((MEM_LEN, X_W), lambda i, j: (i, v_col))],
        out_specs=pl.BlockSpec((None, tq, X_W), lambda i, j: (i, j, 0)),
        compiler_params=_cparams(("parallel", "arbitrary")),
        name="cross_attention",
    )(q, k_arr, v_arr)


XATTN_SEQS_PER_STEP = 4
XATTN_HEAD_ROWS = BF16_ROWS


def _xattn_token_kernel(q_ref, k_ref, v_ref, o_ref):
    rows = MEM_LEN * X_HEADS
    scale = 1.0 / math.sqrt(X_HEAD_DIM)
    col = lax.broadcasted_iota(jnp.int32, (1, rows), 1)
    head = lax.broadcasted_iota(jnp.int32, (XATTN_HEAD_ROWS, 1), 0)
    mine = (col % X_HEADS) == head
    for s in range(XATTN_SEQS_PER_STEP):
        span = slice(s * rows, (s + 1) * rows)
        sc = jnp.where(mine, _dot_nt(q_ref[s], k_ref[span, :].astype(BF16)) * scale, NEG)
        p = jnp.exp(sc - jnp.max(sc, axis=-1, keepdims=True))
        l = jnp.sum(p, axis=-1, keepdims=True)
        o_ref[s] = (_dot(p.astype(BF16), v_ref[span, :].astype(BF16)) / l).astype(o_ref.dtype)


def _cross_attention_token(q, mem_k, mem_v):
    b = q.shape[0]
    assert b % XATTN_SEQS_PER_STEP == 0
    rows = MEM_LEN * X_HEADS
    qh = jnp.pad(q.reshape(b, X_HEADS, X_HEAD_DIM), ((0, 0), (0, XATTN_HEAD_ROWS - X_HEADS), (0, 0)))
    q_spec = pl.BlockSpec((XATTN_SEQS_PER_STEP, XATTN_HEAD_ROWS, X_HEAD_DIM), lambda i: (i, 0, 0))
    kv_spec = pl.BlockSpec((XATTN_SEQS_PER_STEP * rows, X_HEAD_DIM), lambda i: (i, 0))
    o = pl.pallas_call(
        _xattn_token_kernel,
        out_shape=jax.ShapeDtypeStruct((b, XATTN_HEAD_ROWS, X_HEAD_DIM), BF16),
        grid=(b // XATTN_SEQS_PER_STEP,),
        in_specs=[q_spec, kv_spec, kv_spec],
        out_specs=q_spec,
        compiler_params=_cparams(("parallel",)),
        name="cross_attention_token",
    )(qh, mem_k.reshape(b * rows, X_HEAD_DIM), mem_v.reshape(b * rows, X_HEAD_DIM))
    return o[:, :X_HEADS].reshape(b, X_W)


def _wkv_chunk_kernel(za_ref, sp_ref, s0_ref, mu_ref, w0_ref, a0_ref, kk_ref, ka_ref, rk_ref,
                      lng_ref, lnb_ref, w2_ref, a2_ref, g2_ref, seg_ref, segt_ref,
                      ya_ref, shift_ref, s_ref,
                      r_ref, v_ref, kt_ref, kn_ref, b_ref, lw_ref, cum_ref, y_ref, *, tt, c, t_valid):
    c2x = 2 * c
    ti = pl.program_id(1)
    pairs = range(A_HEADS // 2)
    cols = [slice(p * LANES, (p + 1) * LANES) for p in pairs]
    bf = lambda x: x.astype(BF16)

    @pl.when(ti == 0)
    def _():
        s_ref[...] = s0_ref[...]
        shift_ref[...] = sp_ref[...]

    seg = seg_ref[...]
    segt = segt_ref[...]

    def head_sum(x):
        return _dot_exact_rhs(_dot(x.astype(BF16), seg), segt)

    za = za_ref[...].astype(F32)
    row = lax.broadcasted_iota(jnp.int32, (tt, 1), 0)
    prev = jnp.where(row == 0, shift_ref[...], pltpu.roll(za, 1, axis=0))
    last = (t_valid - 1) % tt
    shift_ref[...] = za[last:last + 1, :]
    mix = za + (prev - za) * mu_ref[...]
    valid = (ti * tt + row) < t_valid
    r = mix[:, 0:A_WIDTH]
    k = mix[:, A_WIDTH:2 * A_WIDTH]
    v = jnp.where(valid, mix[:, 2 * A_WIDTH:3 * A_WIDTH], 0.0)
    xw = w0_ref[...] + _dot(bf(jnp.tanh(mix[:, A_OFF_W:A_OFF_A])), w2_ref[...])
    lw = jnp.where(valid, -_sigmoid(xw) * math.exp(-0.5), 0.0)
    a = _sigmoid(a0_ref[...] + _dot(bf(mix[:, A_OFF_A:A_OFF_G]), a2_ref[...]))
    g = _dot(bf(_sigmoid(mix[:, A_OFF_G:A_PAD])), g2_ref[...])
    kk = k * kk_ref[...]
    kn = jnp.where(valid, kk * lax.rsqrt(jnp.maximum(head_sum(kk * kk), 1e-24)), 0.0)
    kt = jnp.where(valid, k * (1.0 + (a - 1.0) * ka_ref[...]), 0.0)
    bonus = head_sum(r * kt * rk_ref[...]) * v
    ri = lax.broadcasted_iota(jnp.int32, (tt, tt), 0)
    ci = lax.broadcasted_iota(jnp.int32, (tt, tt), 1)
    same = (ri // c) == (ci // c)
    cum_ref[...] = _dot_exact_lhs(jnp.where(same & (ci <= ri), 1.0, 0.0).astype(BF16), lw)
    r_ref[...] = r
    v_ref[...] = v
    kt_ref[...] = kt
    kn_ref[...] = kn
    b_ref[...] = kn * a
    lw_ref[...] = lw

    lane = lax.broadcasted_iota(jnp.int32, (1, LANES), 1)
    m0 = (lane < A_HEAD).astype(F32)
    m1 = 1.0 - m0
    r2 = lax.broadcasted_iota(jnp.int32, (c2x, c2x), 0)
    c2 = lax.broadcasted_iota(jnp.int32, (c2x, c2x), 1)
    strict = (r2 % c) > (c2 % c)
    incl = (r2 % c) >= (c2 % c)

    def stack(x):
        return jnp.concatenate([x * m0, x * m1], axis=0)

    def prepare(ch):
        rows = slice(ch * c, (ch + 1) * c)
        cum = cum_ref[rows, :]
        tot = cum_ref[(ch + 1) * c - 1:(ch + 1) * c, :]
        kt_, b_ = kt_ref[rows, :], b_ref[rows, :]
        p_inv = jnp.exp(-cum)
        p_end = jnp.exp(tot - cum)
        return dict(v=v_ref[rows, :], rh=r_ref[rows, :] * jnp.exp(cum), kap=kn_ref[rows, :] * jnp.exp(cum - lw_ref[rows, :]),
                    kh=kt_ * p_inv, bh=b_ * p_inv, khp=kt_ * p_end, bhp=b_ * p_end, pt=jnp.exp(tot))

    def solve(d):
        vst = [stack(d["v"][:, cols[p]]) for p in pairs]
        s0 = [s_ref[p] for p in pairs]
        lhs = [bf(jnp.concatenate([stack(d["kap"][:, cols[p]]), stack(d["rh"][:, cols[p]])], axis=0)) for p in pairs]
        gram = [_dot_nt(lhs[p], bf(jnp.concatenate([stack(d["kh"][:, cols[p]]), stack(d["bh"][:, cols[p]])], axis=0)))
                for p in pairs]
        from_s0 = [_dot_nt(lhs[p], bf(s0[p])) for p in pairs]
        x = [from_s0[p][:c2x] + _dot(bf(jnp.where(strict, gram[p][:c2x, :c2x], 0.0)), bf(vst[p])) for p in pairs]
        lp = [jnp.where(strict, gram[p][:c2x, c2x:], 0.0) for p in pairs]
        sq = [_dot(bf(lp[p]), bf(jnp.concatenate([lp[p], x[p]], axis=1))) for p in pairs]
        lp = [sq[p][:, :c2x] for p in pairs]
        x = [x[p] - sq[p][:, c2x:] for p in pairs]
        n = 2
        while 2 * n < c:
            sq = [_dot(bf(lp[p]), bf(jnp.concatenate([lp[p], x[p]], axis=1))) for p in pairs]
            lp = [sq[p][:, :c2x] for p in pairs]
            x = [x[p] + sq[p][:, c2x:] for p in pairs]
            n *= 2
        x = [x[p] + _dot(bf(lp[p]), bf(x[p])) for p in pairs]
        ys = []
        for p in pairs:
            tri = jnp.concatenate([jnp.where(incl, gram[p][c2x:, :c2x], 0.0),
                                   jnp.where(incl, -gram[p][c2x:, c2x:], 0.0)], axis=1)
            y = from_s0[p][c2x:] + _dot(bf(tri), bf(jnp.concatenate([vst[p], x[p]], axis=0)))
            ys.append(y[:c] + y[c:])
            decayed = jnp.concatenate([stack(d["khp"][:, cols[p]]), stack(d["bhp"][:, cols[p]])], axis=0)
            s_ref[p] = s0[p] * d["pt"][:, cols[p]] + _dot_tn(bf(jnp.concatenate([vst[p], -x[p]], axis=0)), bf(decayed))
        return jnp.concatenate(ys, axis=1)

    n_chunks = tt // c
    d = prepare(0)
    for ch in range(n_chunks):
        d_next = prepare(ch + 1) if ch + 1 < n_chunks else None
        y_ref[ch * c:(ch + 1) * c, :] = solve(d)
        d = d_next

    y = y_ref[...]
    mean = head_sum(y) * (1.0 / A_HEAD)
    yc = y - mean
    var = head_sum(yc * yc) * (1.0 / A_HEAD)
    yn = yc * lax.rsqrt(var + GN_EPS) * lng_ref[...] + lnb_ref[...]
    ya_ref[...] = ((yn + bonus) * g).astype(ya_ref.dtype)


def _wkv_mix(za, shift_prev, s0, prm, *, nb, t_pad, t_valid, tt, chunk):
    nt = t_pad // tt
    assert t_pad % tt == 0 and tt % chunk == 0 and chunk % SUBLANES == 0
    vec = lambda n: pl.BlockSpec((1, n), lambda i, j: (0, 0))
    full = lambda a: pl.BlockSpec(a.shape, lambda i, j: (0,) * a.ndim)
    state_spec = pl.BlockSpec((None, A_HEADS // 2, LANES, LANES), lambda i, j: (i, 0, 0, 0))
    shift_spec = pl.BlockSpec((None, 1, A_PAD), lambda i, j: (i, 0, 0))
    vecs = [prm[n] for n in ("mu", "w0", "a0", "k_k", "k_a", "r_k", "lnx_g", "lnx_b")]
    mats = [prm[n] for n in ("w2", "a2", "g2", "seg", "segt")]
    return pl.pallas_call(
        functools.partial(_wkv_chunk_kernel, tt=tt, c=chunk, t_valid=t_valid),
        out_shape=(jax.ShapeDtypeStruct((nb * t_pad, A_WIDTH), BF16),
                   jax.ShapeDtypeStruct((nb, 1, A_PAD), F32),
                   jax.ShapeDtypeStruct((nb, A_HEADS // 2, LANES, LANES), F32)),
        grid=(nb, nt),
        in_specs=[pl.BlockSpec((tt, A_PAD), lambda i, j: (i * nt + j, 0)), shift_spec, state_spec]
                 + [vec(x.shape[1]) for x in vecs] + [full(x) for x in mats],
        out_specs=(pl.BlockSpec((tt, A_WIDTH), lambda i, j: (i * nt + j, 0)), shift_spec, state_spec),
        scratch_shapes=[pltpu.VMEM((tt, A_WIDTH), F32)] * 8,
        compiler_params=_cparams(("parallel", "arbitrary")),
        name="wkv_mix",
    )(za, shift_prev, s0, *vecs, *mats)


CONV_HALO = 32


def _conv_kernel(zb_ref, cb_ref, w_ref, b_ref, g_ref, be_ref, o_ref, st_ref, ext_ref, conv_ref,
                 *, tt, t_last):
    ti = pl.program_id(1)
    pad = CONV_HALO - (CONV_W - 1)

    @pl.when(ti == 0)
    def _():
        ext_ref[0:CONV_HALO, :] = cb_ref[...]

    zb = zb_ref[...].astype(F32)
    ext_ref[CONV_HALO:CONV_HALO + tt, :] = zb[:, :B_WIDTH] * _sigmoid(zb[:, B_WIDTH:])
    rb = min(tt, 64)
    for r0 in range(0, tt, rb):
        for c0 in range(0, B_WIDTH, LANES):
            cols = slice(c0, c0 + LANES)
            x = ext_ref[r0:r0 + rb + CONV_HALO, cols]
            n = rb + CONV_HALO
            shifted = [x] + [pltpu.roll(x, n - s, axis=0) for s in range(1, SUBLANES)]
            acc = jnp.broadcast_to(b_ref[:, cols], (rb, LANES))
            for j in range(CONV_W):
                s = (pad + j) % SUBLANES
                base = pad + j - s
                acc = acc + shifted[s][base:base + rb] * w_ref[j:j + 1, cols]
            conv_ref[r0:r0 + rb, cols] = acc
    conv = conv_ref[...]
    mean = jnp.mean(conv, axis=-1, keepdims=True)
    xc = conv - mean
    var = jnp.mean(xc * xc, axis=-1, keepdims=True)
    y = xc * lax.rsqrt(var + LN_EPS) * g_ref[...] + be_ref[...]
    o_ref[...] = (y * _sigmoid(y)).astype(o_ref.dtype)
    st_ref[...] = ext_ref[t_last + pad:t_last + pad + CONV_W - 1, :]
    if tt >= CONV_HALO:
        ext_ref[0:CONV_HALO, :] = ext_ref[tt:tt + CONV_HALO, :]


def _conv_module(zb, conv_buf, prm, *, nb, t_pad, t_valid, tt):
    nt = t_pad // tt
    assert nt == 1 or tt >= CONV_HALO
    vec = pl.BlockSpec((1, B_WIDTH), lambda i, j: (0, 0))
    return pl.pallas_call(
        functools.partial(_conv_kernel, tt=tt, t_last=t_valid - (nt - 1) * tt),
        out_shape=(jax.ShapeDtypeStruct((nb * t_pad, B_WIDTH), BF16),
                   jax.ShapeDtypeStruct((nb, CONV_W - 1, B_WIDTH), F32)),
        grid=(nb, nt),
        in_specs=[pl.BlockSpec((tt, 2 * B_WIDTH), lambda i, j: (i * nt + j, 0)),
                  pl.BlockSpec((None, CONV_HALO, B_WIDTH), lambda i, j: (i, 0, 0)),
                  pl.BlockSpec((CONV_W, B_WIDTH), lambda i, j: (0, 0)), vec, vec, vec],
        out_specs=(pl.BlockSpec((tt, B_WIDTH), lambda i, j: (i * nt + j, 0)),
                   pl.BlockSpec((None, CONV_W - 1, B_WIDTH), lambda i, j: (i, 0, 0))),
        scratch_shapes=[pltpu.VMEM((CONV_HALO + tt, B_WIDTH), F32), pltpu.VMEM((tt, B_WIDTH), F32)],
        compiler_params=_cparams(("parallel", "arbitrary")),
        name="conv_module",
    )(zb, conv_buf, prm["dw_w"], prm["dw_b"], prm["cn_g"], prm["cn_b"])


def _top_blocks(gate, blk, n_sel):
    sel = []
    for _ in range(n_sel):
        m = jnp.max(gate, axis=0, keepdims=True)
        idx = jnp.min(jnp.where(gate == m, blk, gate.shape[0]), axis=0, keepdims=True)
        sel.append(idx)
        gate = jnp.where(blk == idx, -jnp.inf, gate)
    return sel


def _moba_prompt_kernel(q_ref, k_ref, v_ref, o_ref, kb_ref, vt_ref, km_ref, qf_ref, *, t):
    kvh = pl.program_id(1)
    qi = pl.program_id(2)
    nb = t // MOBA_BLOCK
    nq = C_GROUP * Q_BLOCK

    log2e = 1.0 / math.log(2.0)
    n_feat = 3

    def slope2_of(group):
        head = (kvh * C_GROUP + group + 1).astype(F32)
        return jnp.exp(head * (-(8.0 / C_HEADS) * math.log(2.0))) * log2e

    @pl.when(qi == 0)
    def _():
        srow = slope2_of(lax.broadcasted_iota(jnp.int32, (nq, 1), 0) // Q_BLOCK)
        s_hi = srow.astype(BF16).astype(F32)
        s_mid = (srow - s_hi).astype(BF16).astype(F32)
        s_lo = (srow - s_hi - s_mid).astype(BF16).astype(F32)
        part = lax.broadcasted_iota(jnp.int32, (1, LANES), 1)
        qfeat = jnp.where(part % n_feat == 0, s_hi, jnp.where(part % n_feat == 1, s_mid, s_lo))
        qf_ref[...] = jnp.where(part < 2 * n_feat, qfeat, 0.0).astype(BF16)
        km_ref[...] = jnp.zeros_like(km_ref)
        lane = lax.broadcasted_iota(jnp.int32, (MOBA_BLOCK, LANES), 1)
        offset = lax.broadcasted_iota(jnp.int32, (MOBA_BLOCK, LANES), 0).astype(F32)
        for n in range(nb):
            kblk = k_ref[n * MOBA_BLOCK:(n + 1) * MOBA_BLOCK, :]
            feat = jnp.where(lane < n_feat, float(n * MOBA_BLOCK), jnp.where(lane < 2 * n_feat, offset, 0.0))
            kb_ref[n] = jnp.concatenate([kblk, feat], axis=1).astype(BF16)
            vt_ref[n] = v_ref[n * MOBA_BLOCK:(n + 1) * MOBA_BLOCK, :].T.astype(BF16)
            km_ref[n:n + 1, :] = jnp.mean(kblk, axis=0, keepdims=True)

    q = q_ref[...]
    qs = jnp.concatenate([q[:, g * C_HEAD_DIM:(g + 1) * C_HEAD_DIM] for g in range(C_GROUP)], axis=0)
    own = (qi * Q_BLOCK) // MOBA_BLOCK
    blk = lax.broadcasted_iota(jnp.int32, (km_ref.shape[0], 1), 0)
    km_hi, km_lo = _split2(km_ref[...])
    gate2 = _dot_nt(jnp.concatenate([km_hi, km_lo], axis=0), qs.astype(BF16))
    gate = gate2[:km_ref.shape[0]] + gate2[km_ref.shape[0]:]
    gate = jnp.where(blk < own, gate, jnp.where(blk < nb, NEG, -jnp.inf))
    sel = _top_blocks(gate, blk, MOBA_TOPK)
    sel = [jnp.where(own > r, sel[r], -1) for r in range(MOBA_TOPK)]

    col = lax.broadcasted_iota(jnp.int32, (1, nq), 1)
    pos = qi * Q_BLOCK + col % Q_BLOCK
    base = -slope2_of(col // Q_BLOCK) * pos.astype(F32)
    qb = jnp.concatenate([(qs * (log2e / math.sqrt(C_HEAD_DIM))).astype(BF16), qf_ref[...]], axis=1)
    trow = lax.broadcasted_iota(jnp.int32, (MOBA_BLOCK, 1), 0)

    def attend_upto(own_static):
        scores = []
        for n in range(own_static):
            picked = (sel[0] == n) | (sel[1] == n) | (sel[2] == n)
            scores.append(_dot_nt(kb_ref[n], qb) + jnp.where(picked, base, NEG))
        causal = trow <= col % Q_BLOCK
        scores.append(jnp.where(causal, _dot_nt(kb_ref[own_static], qb) + base, NEG))
        m = scores[-1].max(axis=0, keepdims=True)
        for s in scores[:-1]:
            m = jnp.maximum(m, jnp.max(s, axis=0, keepdims=True))
        l = jnp.zeros((1, nq), F32)
        acc = jnp.zeros((C_HEAD_DIM, nq), F32)
        for n, s in enumerate(scores):
            p = jnp.exp2(s - m)
            l = l + jnp.sum(p, axis=0, keepdims=True)
            acc = acc + _dot(vt_ref[n], p.astype(BF16))
        o = acc / l
        for g in range(C_GROUP):
            cols = slice(g * Q_BLOCK, (g + 1) * Q_BLOCK)
            o_ref[:, g * C_HEAD_DIM:(g + 1) * C_HEAD_DIM] = o[:, cols].T.astype(o_ref.dtype)

    for own_static in range(nb):
        pl.when(own == own_static)(functools.partial(attend_upto, own_static))


def _moba_prompt(z, *, nb, t):
    nq = t // Q_BLOCK
    assert t % Q_BLOCK == 0 and t // MOBA_BLOCK <= 256 and MOBA_BLOCK <= 256
    kcol = C_Q_W // C_HEAD_DIM
    vcol = (C_Q_W + C_KV_W) // C_HEAD_DIM
    return pl.pallas_call(
        functools.partial(_moba_prompt_kernel, t=t),
        out_shape=jax.ShapeDtypeStruct((nb * t, C_Q_W), BF16),
        grid=(nb, C_KV_HEADS, nq),
        in_specs=[pl.BlockSpec((Q_BLOCK, C_GROUP * C_HEAD_DIM), lambda b, h, i: (b * nq + i, h)),
                  pl.BlockSpec((t, C_HEAD_DIM), lambda b, h, i: (b, kcol + h)),
                  pl.BlockSpec((t, C_HEAD_DIM), lambda b, h, i: (b, vcol + h))],
        out_specs=pl.BlockSpec((Q_BLOCK, C_GROUP * C_HEAD_DIM), lambda b, h, i: (b * nq + i, h)),
        scratch_shapes=[pltpu.VMEM((t // MOBA_BLOCK, MOBA_BLOCK, C_HEAD_DIM + LANES), BF16),
                        pltpu.VMEM((t // MOBA_BLOCK, C_HEAD_DIM, MOBA_BLOCK), BF16),
                        pltpu.VMEM((-(-t // MOBA_BLOCK // SUBLANES) * SUBLANES, C_HEAD_DIM), F32),
                        pltpu.VMEM((C_GROUP * Q_BLOCK, LANES), BF16)],
        compiler_params=_cparams(("parallel", "parallel", "arbitrary")),
        name="moba_prompt",
    )(z, z, z)


PAGES_PER_STEP = 16 * MOBA_BLOCK // PAGE_SIZE


def _moba_decode_kernel(pt_ref, z_ref, *refs, n_blocks):
    del pt_ref
    npg = PAGES_PER_STEP
    k_refs, v_refs = refs[:npg], refs[npg:2 * npg]
    o_ref = refs[2 * npg]
    m_ref, l_ref, acc_ref, gate_ref = refs[2 * npg + 1:]
    j = pl.program_id(1)
    ppb = MOBA_BLOCK // PAGE_SIZE
    past = n_blocks * MOBA_BLOCK
    scale = 1.0 / math.sqrt(C_HEAD_DIM)
    hrow = lax.broadcasted_iota(jnp.int32, (C_HEADS, 1), 0)
    slope = jnp.exp((hrow + 1).astype(F32) * (-(8.0 / C_HEADS) * math.log(2.0)))

    z = z_ref[...]
    qrows = jnp.concatenate([z[:, h * C_HEAD_DIM:(h + 1) * C_HEAD_DIM] for h in range(C_HEADS)], axis=0)
    qb = qrows.astype(BF16)
    rows_blk = MOBA_BLOCK * C_KV_HEADS
    col = lax.broadcasted_iota(jnp.int32, (1, rows_blk), 1)
    mine = (col % C_KV_HEADS) == (hrow // C_GROUP)
    tab = jnp.where(mine, slope * (col // C_KV_HEADS).astype(F32), NEG)

    for blk in range(npg // ppb):
        n = j * (npg // ppb) + blk
        kblk = jnp.concatenate([k_refs[blk * ppb + i][...] for i in range(ppb)], axis=0)
        vblk = jnp.concatenate([v_refs[blk * ppb + i][...] for i in range(ppb)], axis=0)
        s = _dot_nt(qb, kblk.astype(BF16)) * scale + tab - slope * (past - n * MOBA_BLOCK).astype(F32)
        m = jnp.max(s, axis=-1, keepdims=True)
        p = jnp.exp(s - m)
        m_ref[pl.ds(n, 1)] = m[None]
        l_ref[pl.ds(n, 1)] = jnp.sum(p, axis=-1, keepdims=True)[None]
        acc_ref[pl.ds(n, 1)] = _dot(p.astype(BF16), vblk.astype(BF16))[None]
        ksum = jnp.sum(kblk.reshape(rows_blk // SUBLANES, SUBLANES, C_HEAD_DIM), axis=0)
        kmean = (ksum[:C_KV_HEADS] + ksum[C_KV_HEADS:]) * (1.0 / MOBA_BLOCK)
        kmrows = jnp.concatenate([jnp.broadcast_to(kmean[kvh:kvh + 1], (C_GROUP, C_HEAD_DIM))
                                  for kvh in range(C_KV_HEADS)], axis=0)
        gate_ref[pl.ds(n, 1)] = jnp.sum(qrows * kmrows, axis=-1, keepdims=True)[None]

    @pl.when(j == pl.num_programs(1) - 1)
    def _():
        gates = gate_ref[...]
        ms = m_ref[...]
        bidx = lax.broadcasted_iota(jnp.int32, gates.shape, 0)
        picked = jnp.zeros(gates.shape, jnp.bool_)
        for _ in range(min(MOBA_TOPK, n_blocks)):
            gmax = jnp.max(gates, axis=0, keepdims=True)
            first = jnp.min(jnp.where(gates == gmax, bidx, n_blocks), axis=0, keepdims=True)
            hit = bidx == first
            picked = picked | hit
            gates = jnp.where(hit, -jnp.inf, gates)
        knew = jnp.concatenate(
            [z[:, C_Q_W + kvh * C_HEAD_DIM:C_Q_W + (kvh + 1) * C_HEAD_DIM] for kvh in range(C_KV_HEADS)
             for _ in range(C_GROUP)], axis=0)
        vnew = jnp.concatenate(
            [z[:, C_Q_W + C_KV_W + kvh * C_HEAD_DIM:C_Q_W + C_KV_W + (kvh + 1) * C_HEAD_DIM]
             for kvh in range(C_KV_HEADS) for _ in range(C_GROUP)], axis=0)
        s_own = jnp.sum(qb.astype(F32) * knew.astype(BF16).astype(F32), axis=-1, keepdims=True) * scale
        mtot = jnp.maximum(jnp.max(jnp.where(picked, ms, NEG), axis=0), s_own)
        wgt = jnp.where(picked, jnp.exp(ms - mtot[None]), 0.0)
        p_own = jnp.exp(s_own - mtot)
        l = jnp.sum(wgt * l_ref[...], axis=0) + p_own
        acc = jnp.sum(wgt * acc_ref[...], axis=0) + p_own.astype(BF16).astype(F32) * vnew.astype(BF16).astype(F32)
        o_ref[...] = (acc / l).astype(o_ref.dtype)


def _moba_decode(z, cache_k, cache_v, page_table):
    nb, n_pages = page_table.shape
    assert n_pages % PAGES_PER_STEP == 0 and 2 * C_KV_HEADS == SUBLANES
    n_blocks = n_pages * PAGE_SIZE // MOBA_BLOCK
    steps = n_pages // PAGES_PER_STEP

    def page_spec(i):
        return pl.BlockSpec((None, PAGE_SIZE * C_KV_HEADS, C_HEAD_DIM),
                            lambda b, j, pt: (pt[b, j * PAGES_PER_STEP + i], 0, 0))

    pages = [page_spec(i) for i in range(PAGES_PER_STEP)]
    zw = z.shape[-1]
    return pl.pallas_call(
        functools.partial(_moba_decode_kernel, n_blocks=n_blocks),
        out_shape=jax.ShapeDtypeStruct((nb, C_HEADS, C_HEAD_DIM), BF16),
        grid_spec=pltpu.PrefetchScalarGridSpec(
            num_scalar_prefetch=1,
            grid=(nb, steps),
            in_specs=[pl.BlockSpec((None, 1, zw), lambda b, j, pt: (b, 0, 0))] + pages + pages,
            out_specs=pl.BlockSpec((None, C_HEADS, C_HEAD_DIM), lambda b, j, pt: (b, 0, 0)),
            scratch_shapes=[pltpu.VMEM((n_blocks, C_HEADS, 1), F32), pltpu.VMEM((n_blocks, C_HEADS, 1), F32),
                            pltpu.VMEM((n_blocks, C_HEADS, C_HEAD_DIM), F32),
                            pltpu.VMEM((n_blocks, C_HEADS, 1), F32)]),
        compiler_params=_cparams(("parallel", "arbitrary")),
        name="moba_decode",
    )(page_table, z, *([cache_k] * PAGES_PER_STEP), *([cache_v] * PAGES_PER_STEP))


def _pad_a_cols(x):
    i3 = 3 * A_WIDTH
    z = lambda n: jnp.zeros(x.shape[:-1] + (n,), x.dtype)
    return jnp.concatenate([
        x[..., :i3],
        x[..., i3:i3 + DECAY_RANK], z(LANES - DECAY_RANK),
        x[..., i3 + DECAY_RANK:i3 + DECAY_RANK + AAA_RANK], z(LANES - AAA_RANK),
        x[..., i3 + DECAY_RANK + AAA_RANK:], z(2 * LANES - GATE_RANK)], axis=-1)


def _unpad_a_cols(x):
    return jnp.concatenate([x[..., :A_OFF_W + DECAY_RANK], x[..., A_OFF_A:A_OFF_A + AAA_RANK],
                            x[..., A_OFF_G:A_OFF_G + GATE_RANK]], axis=-1)


def _pad_rows(w, n):
    return jnp.concatenate([w, jnp.zeros((n - w.shape[0],) + w.shape[1:], w.dtype)], axis=0)


def _ab_params(i, w_in_ab, mu_ab, w0, w2, a0, a2, g2, k_k, k_a, r_k, lnx_g, lnx_b, dw_w, dw_b, cn_g, cn_b):
    row = lambda x: x.reshape(1, -1)
    head = jnp.arange(A_WIDTH)[:, None] // A_HEAD == jnp.arange(LANES)[None, :]
    return {
        "w_a": _pad_a_cols(w_in_ab[i][:, :A_COLS]).astype(BF16),
        "w_b": w_in_ab[i][:, A_COLS:].astype(BF16),
        "mu": row(_pad_a_cols(mu_ab[i])),
        "w0": row(w0[i]), "a0": row(a0[i]), "k_k": row(k_k[i]), "k_a": row(k_a[i]), "r_k": row(r_k[i]),
        "lnx_g": row(lnx_g[i]), "lnx_b": row(lnx_b[i]),
        "w2": _pad_rows(w2[i], LANES).astype(BF16), "a2": _pad_rows(a2[i], LANES).astype(BF16),
        "g2": _pad_rows(g2[i], 2 * LANES).astype(BF16),
        "seg": head.astype(BF16), "segt": head.T.astype(BF16),
        "dw_w": dw_w[i], "dw_b": row(dw_b[i]), "cn_g": row(cn_g[i]), "cn_b": row(cn_b[i]),
    }


def _state_to_pairs(s):
    nb = s.shape[0]
    s = s.reshape(nb, A_HEADS // 2, 2, A_HEAD, A_HEAD)
    z = jnp.zeros_like(s[:, :, 0])
    top = jnp.concatenate([s[:, :, 0], z], axis=-1)
    bot = jnp.concatenate([z, s[:, :, 1]], axis=-1)
    return jnp.concatenate([top, bot], axis=-2)


def _pairs_to_state(s):
    nb = s.shape[0]
    return jnp.stack([s[:, :, :A_HEAD, :A_HEAD], s[:, :, A_HEAD:, A_HEAD:]], axis=2).reshape(
        nb, A_HEADS, A_HEAD, A_HEAD)


def _mixer_ab(x, g_pre, shift_prev, wkv0, conv_buf, prm, *, nb, t):
    if t >= WKV_CHUNK:
        assert t % WKV_CHUNK == 0
        t_pad, chunk, tt_a, tt_b, xp = t, WKV_CHUNK, min(t, 256), min(t, 128), x
    else:
        t_pad = chunk = tt_a = tt_b = -(-t // WKV_SHORT_CHUNK) * WKV_SHORT_CHUNK
        xp = jnp.pad(x.reshape(nb, t, -1), ((0, 0), (0, t_pad - t), (0, 0))).reshape(nb * t_pad, -1)
    za = _norm_matmul(xp, g_pre, prm["w_a"], BF16, tn=512)
    zb = _norm_matmul(xp, g_pre, prm["w_b"], BF16, tn=512)
    ya, shift, s_new = _wkv_mix(za, _pad_a_cols(shift_prev)[:, None, :], _state_to_pairs(wkv0), prm,
                                nb=nb, t_pad=t_pad, t_valid=t, tt=tt_a, chunk=chunk)
    hist = jnp.pad(conv_buf, ((0, 0), (CONV_HALO - (CONV_W - 1), 0), (0, 0)))
    yb, conv_new = _conv_module(zb, hist, prm, nb=nb, t_pad=t_pad, t_valid=t, tt=tt_b)
    y = jnp.concatenate([ya, yb], axis=-1)
    if t_pad != t:
        y = y.reshape(nb, t_pad, -1)[:, :t].reshape(nb * t, -1)
    return y, _unpad_a_cols(shift[:, 0]), _pairs_to_state(s_new), conv_new


def _rest_of_layer(x, l, k_arr, v_arr, k_col, v_col, wts, *, nb, t):
    (g_x_pre, g_x_post, w_xq, w_xo, g_ffn_pre, g_ffn_post, w_gu, w_down) = wts
    q = _norm_matmul(x, g_x_pre[l], w_xq, BF16, tn=X_W)
    if k_arr.ndim == 4:
        assert t == 1
        o = _cross_attention_token(q, k_arr, v_arr)
    else:
        tq = min(t, 512)
        assert t % tq == 0 and tq % BF16_ROWS == 0
        o = _cross_attention(q.reshape(nb, t, X_W), k_arr, v_arr, k_col, v_col, tq=tq).reshape(nb * t, X_W)
    x = _matmul_norm_res(o, w_xo, g_x_post[l], x)
    h = _norm_matmul(x, g_ffn_pre[l], w_gu, BF16, tn=512, swiglu_half=D_FF)
    return _matmul_norm_res(h, w_down, g_ffn_post[l], x)


def kernel(x_prompt, x_sample, cache_k, cache_v, state_wkv, state_shift, state_conv, cache_mem_k, cache_mem_v, page_table, mem_prompt, w_in_ab, w_out_ab, mu_ab, w0, w2, a0, a2, g2, k_k, k_a, r_k, lnx_g, lnx_b, dw_w, dw_b, cn_g, cn_b, w_in_c, w_out_c, g_mix_pre, g_mix_post, g_x_pre, g_x_post, g_mem, w_xq, w_xkv, w_xo, g_ffn_pre, g_ffn_post, w_ffn_gu, w_ffn_down):
    bp, tp, d = x_prompt.shape
    bs, ts, _ = x_sample.shape
    depth = g_mix_pre.shape[0]
    xp = x_prompt.reshape(bp * tp, d)
    xs = x_sample.reshape(bs * ts, d)
    mem = mem_prompt.reshape(bp * MEM_LEN, d)
    outs = {n: [] for n in ("kp", "vp", "wkvp", "shp", "cvp", "mkp", "mvp", "ks", "vs", "wkvs", "shs", "cvs")}
    for l in range(depth):
        i = l // 2
        if l % 2 == 0:
            prm = _ab_params(i, w_in_ab, mu_ab, w0, w2, a0, a2, g2, k_k, k_a, r_k, lnx_g, lnx_b,
                             dw_w, dw_b, cn_g, cn_b)
            w_out = w_out_ab[i].astype(BF16)
            fp, shp, wkvp, cvp = _mixer_ab(
                xp, g_mix_pre[l], jnp.zeros((bp, A_COLS), F32), jnp.zeros((bp, A_HEADS, A_HEAD, A_HEAD), F32),
                jnp.zeros((bp, CONV_W - 1, B_WIDTH), F32), prm, nb=bp, t=tp)
            fs, shs, wkvs, cvs = _mixer_ab(xs, g_mix_pre[l], state_shift[i], state_wkv[i], state_conv[i], prm,
                                           nb=bs, t=ts)
            for n, val in (("shp", shp), ("wkvp", wkvp), ("cvp", cvp), ("shs", shs), ("wkvs", wkvs), ("cvs", cvs)):
                outs[n].append(val)
        else:
            w_in = w_in_c[i].astype(BF16)
            w_out = w_out_c[i].astype(BF16)
            zp = _norm_matmul(xp, g_mix_pre[l], w_in, F32, tn=512)
            fp = _moba_prompt(zp, nb=bp, t=tp)
            zs = _norm_matmul(xs, g_mix_pre[l], w_in, F32, tn=512)
            n_phys = cache_k.shape[1]
            page_rows = (-1, PAGE_SIZE * C_KV_HEADS, C_HEAD_DIM)
            fs = _moba_decode(zs.reshape(bs, ts, -1), cache_k.reshape(page_rows), cache_v.reshape(page_rows),
                              page_table + i * n_phys).reshape(bs * ts, C_Q_W)
            kv = lambda z, nb, t, off: z[:, off:off + C_KV_W].reshape(nb, t, C_KV_HEADS, C_HEAD_DIM)
            outs["kp"].append(kv(zp, bp, tp, C_Q_W))
            outs["vp"].append(kv(zp, bp, tp, C_Q_W + C_KV_W))
            outs["ks"].append(kv(zs, bs, ts, C_Q_W))
            outs["vs"].append(kv(zs, bs, ts, C_Q_W + C_KV_W))
        xp = _matmul_norm_res(fp, w_out, g_mix_post[l], xp)
        xs = _matmul_norm_res(fs, w_out, g_mix_post[l], xs)
        mkv = _norm_matmul(mem, g_mem[l], w_xkv[l].astype(BF16), F32, tn=512)
        outs["mkp"].append(mkv[:, :X_W].reshape(bp, MEM_LEN, X_HEADS, X_HEAD_DIM))
        outs["mvp"].append(mkv[:, X_W:].reshape(bp, MEM_LEN, X_HEADS, X_HEAD_DIM))
        wts = (g_x_pre, g_x_post, w_xq[l].astype(BF16), w_xo[l].astype(BF16), g_ffn_pre, g_ffn_post,
               w_ffn_gu[l].astype(BF16), w_ffn_down[l].astype(BF16))
        xp = _rest_of_layer(xp, l, mkv, mkv, 0, 1, wts, nb=bp, t=tp)
        if ts == 1:
            mem_s = (cache_mem_k[l], cache_mem_v[l])
        else:
            mem_s = (cache_mem_k[l].reshape(bs * MEM_LEN, X_W), cache_mem_v[l].reshape(bs * MEM_LEN, X_W))
        xs = _rest_of_layer(xs, l, *mem_s, 0, 0, wts, nb=bs, t=ts)
    st = lambda n: jnp.stack(outs[n])
    return (xp.reshape(bp, tp, d), xs.reshape(bs, ts, d), st("kp"), st("vp"), st("wkvp"), st("shp"), st("cvp"),
            st("mkp"), st("mvp"), st("ks"), st("vs"), st("wkvs"), st("shs"), st("cvs"))
```
